```python
import math, functools
import jax, jax.numpy as jnp
from jax import lax
import numpy as np

D_MODEL = 1024
BATCH = 32
SEQ = 256
DEPTH = 2
DEC_BATCH = 8
DEC_SEQ = 2048
PAST_LEN = 512

GRID_W = 64
HEAD_DIM = 64
N_HEADS = D_MODEL // HEAD_DIM
DECAY_LORA = 64
A_LORA = 64
GATE_LORA = 128
D_FF = 2816
N_MIXERS = 2
N_RWKV_LAYERS = (DEPTH + 1) // 2
N_POOL_LAYERS = DEPTH // 2
POOL_WINDOWS = (2, 4, 8, 16)
N_POOL_GROUPS = 4
POOL_GC = D_MODEL // N_POOL_GROUPS
N_MOD = 9
RMS_EPS = 1e-6
GN_EPS = 64e-5

kernel_name = 'hybrid_rwkv7_pool_diffusion_step'


def rms_norm(x, g):
    xf = x.astype(jnp.float32)
    y = xf * lax.rsqrt(jnp.mean(xf * xf, axis=-1, keepdims=True) + RMS_EPS)
    return (y * g.astype(jnp.float32)).astype(x.dtype)


def premod(x, g_pre, shift, scale):
    return rms_norm(x, g_pre) * (1 + scale) + shift


def swiglu(h, w1, w3, w2):
    return (jax.nn.silu(h @ w1) * (h @ w3)) @ w2


def wkv_scan(S0, r, w, k, v, kk, a, reverse):
    seq = tuple(jnp.moveaxis(t.astype(jnp.float32), 1, 0) for t in (r, w, k, v, kk, a))

    def step(S, inp):
        r_t, w_t, k_t, v_t, kk_t, a_t = inp
        sa = jnp.einsum('bhij,bhj->bhi', S, -kk_t)
        S = (S * w_t[:, :, None, :]
             + sa[..., None] * (kk_t * a_t)[:, :, None, :]
             + v_t[..., None] * k_t[:, :, None, :])
        return S, jnp.einsum('bhij,bhj->bhi', S, r_t)

    S, ys = lax.scan(step, S0.astype(jnp.float32), seq, reverse=reverse)
    return jnp.moveaxis(ys, 0, 1), S


def head_group_norm(y, w, b):
    yf = y.astype(jnp.float32)
    mean = jnp.mean(yf, axis=-1, keepdims=True)
    var = jnp.mean(jnp.square(yf - mean), axis=-1, keepdims=True)
    yn = (yf - mean) * lax.rsqrt(var + GN_EPS)
    return yn.reshape(y.shape[0], y.shape[1], -1) * w + b


def rwkv_time_mix(h, S0f, S0b, mu, w_rkv, w0, w1, w2, a0, a1, a2, g1, g2, k_k, k_a, r_k, gn_w, gn_b, w_o):
    B, T, D = h.shape
    heads = lambda t: t.reshape(B, T, N_HEADS, HEAD_DIM)
    hp = jnp.pad(h, ((0, 0), (1, 1), (0, 0)))
    xx = 0.5 * (hp[:, :-2] + hp[:, 2:]) - h
    xs = h[None] + xx[None] * mu[:, None, None, :]
    xr, xw, xk, xv, xa, xg = xs[0], xs[1], xs[2], xs[3], xs[4], xs[5]
    rkv = jnp.einsum('nbtd,nde->nbte', jnp.stack([xr, xk, xv]), w_rkv)
    r, k, v = heads(rkv[0]), rkv[1], heads(rkv[2])
    g = jax.nn.sigmoid(xg @ g1) @ g2
    kk = heads(k * k_k).astype(jnp.float32)
    kk = kk / jnp.maximum(jnp.sqrt(jnp.sum(kk * kk, axis=-1, keepdims=True)), 1e-12)
    ys, bonuses, finals = [], [], []
    for d, S0 in enumerate((S0f, S0b)):
        wlog = -jax.nn.softplus(-(w0[d] + jnp.tanh(xw @ w1[d]) @ w2[d])) - 0.5
        decay = jnp.exp(-jnp.exp(wlog.astype(jnp.float32)))
        a = jax.nn.sigmoid(a0[d] + (xa @ a1[d]) @ a2[d])
        kd = heads(k * (1 + (a - 1) * k_a))
        y_d, S_d = wkv_scan(S0, r, heads(decay), kd, v, kk, heads(a), reverse=(d == 1))
        ys.append(y_d)
        bonuses.append(jnp.sum(r * kd * r_k, axis=-1, keepdims=True) * v)
        finals.append(S_d)
    y = head_group_norm(ys[0] + ys[1], gn_w, gn_b) + (bonuses[0] + bonuses[1]).reshape(B, T, D)
    out = (y.astype(h.dtype) * g) @ w_o
    return out, finals[0], finals[1]


def centred_box_mean(x, window, axis):
    L = x.shape[axis]
    cs = jnp.cumsum(x.astype(jnp.float32), axis=axis)
    pad_cfg = [(0, 0)] * x.ndim
    pad_cfg[axis] = (1, 0)
    cs = jnp.pad(cs, pad_cfg)
    t = jnp.arange(L)
    lo = jnp.clip(t - window // 2, 0, L)
    hi = jnp.clip(t + window - window // 2, 0, L)
    s = jnp.take(cs, hi, axis=axis) - jnp.take(cs, lo, axis=axis)
    cshape = [1] * x.ndim
    cshape[axis] = L
    cnt = (hi - lo).astype(jnp.float32).reshape(cshape)
    return (s / cnt).astype(x.dtype)


def pool_mix(h, pool_w, pool_scale, grid):
    B, T, D = h.shape
    if grid:
        rows = T // GRID_W
        hg = h.reshape(B, rows, GRID_W, D)
        axes = (1, 2)
    else:
        hg = h
        axes = (1,)
    outs = []
    for gi, win in enumerate(POOL_WINDOWS):
        part = hg[..., gi * POOL_GC:(gi + 1) * POOL_GC]
        m = part
        for ax in axes:
            m = centred_box_mean(m, win, ax)
        outs.append(m - part)
    dlt = jnp.stack(outs, axis=-2)
    y = jnp.einsum('...gc,gce->...ge', dlt, pool_w)
    return y.reshape(B, T, D) * pool_scale


def trunk(x, cond, S0f_all, S0b_all, grid, W):
    B = x.shape[0]
    finals_f, finals_b = [], []
    for i in range(DEPTH):
        mod = jax.nn.silu(cond) @ W['w_mod'][i] + W['b_mod'][i]
        m = jnp.split(mod[:, None, :], N_MOD, axis=-1)
        h = premod(x, W['norm_pre'][i, 0], m[0], m[1])
        f = swiglu(h, W['ffn_w1'][i, 0], W['ffn_w3'][i, 0], W['ffn_w2'][i, 0])
        x = x + 0.5 * m[2] * rms_norm(f, W['norm_post'][i, 0])
        h = premod(x, W['norm_pre'][i, 1], m[3], m[4])
        j = i // N_MIXERS
        if i % N_MIXERS == 0:
            if S0f_all is None:
                s0f = jnp.zeros((B, N_HEADS, HEAD_DIM, HEAD_DIM), jnp.float32)
                s0b = s0f
            else:
                s0f, s0b = S0f_all[:, j], S0b_all[:, j]
            out, sf, sb = rwkv_time_mix(
                h, s0f, s0b, W['rwkv_mu'][j], W['rwkv_w_rkv'][j], W['rwkv_w0'][j], W['rwkv_w1'][j],
                W['rwkv_w2'][j], W['rwkv_a0'][j], W['rwkv_a1'][j], W['rwkv_a2'][j], W['rwkv_g1'][j],
                W['rwkv_g2'][j], W['rwkv_k_k'][j], W['rwkv_k_a'][j], W['rwkv_r_k'][j], W['rwkv_gn_w'][j],
                W['rwkv_gn_b'][j], W['rwkv_w_o'][j])
            finals_f.append(sf.astype(x.dtype))
            finals_b.append(sb.astype(x.dtype))
        else:
            out = pool_mix(h, W['pool_w'][j], W['pool_scale'][j], grid)
        x = x + m[5] * rms_norm(out, W['norm_post'][i, 1])
        h = premod(x, W['norm_pre'][i, 2], m[6], m[7])
        f = swiglu(h, W['ffn_w1'][i, 1], W['ffn_w3'][i, 1], W['ffn_w2'][i, 1])
        x = x + 0.5 * m[8] * rms_norm(f, W['norm_post'][i, 2])
    return x, jnp.stack(finals_f, axis=1), jnp.stack(finals_b, axis=1)


def setup_inputs(seed: int = 0) -> dict:
    key = jax.random.key(seed)
    ks = jax.random.split(key, 40)
    D = D_MODEL
    NR, NP = N_RWKV_LAYERS, N_POOL_LAYERS

    def nrm(k, shape, scale):
        return jax.random.normal(k, shape, jnp.float32) * scale

    return {
        'x_prompt': nrm(ks[0], (BATCH, SEQ, D), 1.0),
        'x_sample': nrm(ks[1], (DEC_BATCH, DEC_SEQ, D), 1.0),
        'c': nrm(ks[2], (DEC_BATCH, D), 1.0),
        'state_ctx_fwd': nrm(ks[3], (DEC_BATCH, NR, N_HEADS, HEAD_DIM, HEAD_DIM), 0.5),
        'state_ctx_bwd': nrm(ks[4], (DEC_BATCH, NR, N_HEADS, HEAD_DIM, HEAD_DIM), 0.5),
        'c_ctx': nrm(ks[5], (D,), 1.0),
        'w_mod': nrm(ks[6], (DEPTH, D, N_MOD * D), 0.5 * D ** -0.5),
        'b_mod': nrm(ks[7], (DEPTH, N_MOD * D), 0.02),
        'norm_pre': 1.0 + nrm(ks[8], (DEPTH, 3, D), 0.02),
        'norm_post': 1.0 + nrm(ks[9], (DEPTH, 3, D), 0.02),
        'ffn_w1': nrm(ks[10], (DEPTH, 2, D, D_FF), D ** -0.5),
        'ffn_w3': nrm(ks[11], (DEPTH, 2, D, D_FF), D ** -0.5),
        'ffn_w2': nrm(ks[12], (DEPTH, 2, D_FF, D), D_FF ** -0.5),
        'rwkv_mu': jax.random.uniform(ks[13], (NR, 6, D), jnp.float32),
        'rwkv_w_rkv': nrm(ks[14], (NR, 3, D, D), D ** -0.5),
        'rwkv_w0': jnp.linspace(-6.0, -1.0, D, dtype=jnp.float32)[None, None, :] + nrm(ks[15], (NR, 2, D), 0.1),
        'rwkv_w1': nrm(ks[16], (NR, 2, D, DECAY_LORA), D ** -0.5),
        'rwkv_w2': nrm(ks[17], (NR, 2, DECAY_LORA, D), 0.1 * DECAY_LORA ** -0.5),
        'rwkv_a0': nrm(ks[18], (NR, 2, D), 0.1),
        'rwkv_a1': nrm(ks[19], (NR, 2, D, A_LORA), D ** -0.5),
        'rwkv_a2': nrm(ks[20], (NR, 2, A_LORA, D), A_LORA ** -0.5),
        'rwkv_g1': nrm(ks[21], (NR, D, GATE_LORA), D ** -0.5),
        'rwkv_g2': nrm(ks[22], (NR, GATE_LORA, D), GATE_LORA ** -0.5),
        'rwkv_k_k': 0.85 + nrm(ks[23], (NR, D), 0.02),
        'rwkv_k_a': 1.0 + nrm(ks[24], (NR, D), 0.02),
        'rwkv_r_k': nrm(ks[25], (NR, N_HEADS, HEAD_DIM), 0.1),
        'rwkv_gn_w': 1.0 + nrm(ks[26], (NR, D), 0.02),
        'rwkv_gn_b': nrm(ks[27], (NR, D), 0.02),
        'rwkv_w_o': nrm(ks[28], (NR, D, D), D ** -0.5),
        'pool_w': nrm(ks[29], (NP, N_POOL_GROUPS, POOL_GC, POOL_GC), POOL_GC ** -0.5),
        'pool_scale': 1.0 + nrm(ks[30], (NP, D), 0.1),
    }


def reference(x_prompt, x_sample, c, state_ctx_fwd, state_ctx_bwd, c_ctx, w_mod, b_mod, norm_pre, norm_post,
              ffn_w1, ffn_w3, ffn_w2, rwkv_mu, rwkv_w_rkv, rwkv_w0, rwkv_w1, rwkv_w2, rwkv_a0, rwkv_a1,
              rwkv_a2, rwkv_g1, rwkv_g2, rwkv_k_k, rwkv_k_a, rwkv_r_k, rwkv_gn_w, rwkv_gn_b, rwkv_w_o,
              pool_w, pool_scale):
    W = dict(w_mod=w_mod, b_mod=b_mod, norm_pre=norm_pre, norm_post=norm_post,
             ffn_w1=ffn_w1, ffn_w3=ffn_w3, ffn_w2=ffn_w2,
             rwkv_mu=rwkv_mu, rwkv_w_rkv=rwkv_w_rkv, rwkv_w0=rwkv_w0, rwkv_w1=rwkv_w1, rwkv_w2=rwkv_w2,
             rwkv_a0=rwkv_a0, rwkv_a1=rwkv_a1, rwkv_a2=rwkv_a2, rwkv_g1=rwkv_g1, rwkv_g2=rwkv_g2,
             rwkv_k_k=rwkv_k_k, rwkv_k_a=rwkv_k_a, rwkv_r_k=rwkv_r_k, rwkv_gn_w=rwkv_gn_w,
             rwkv_gn_b=rwkv_gn_b, rwkv_w_o=rwkv_w_o, pool_w=pool_w, pool_scale=pool_scale)
    y_prompt, new_state_ctx_fwd, new_state_ctx_bwd = trunk(x_prompt, c_ctx[None, :], None, None, False, W)
    y_sample, _, _ = trunk(x_sample, c, state_ctx_fwd, state_ctx_bwd, True, W)
    return (y_prompt, y_sample, new_state_ctx_fwd, new_state_ctx_bwd)
```

```python
import functools

import numpy as np
import jax
import jax.numpy as jnp
from jax import lax
from jax.experimental import pallas as pl
from jax.experimental.pallas import tpu as pltpu

D = 1024
HEAD = 64
NHEAD = D // HEAD
D_FF = 2816
DEPTH = 2
N_MOD = 9
POOL_WINDOWS = (2, 4, 8, 16)
POOL_GC = D // 4
GRID_W = 64
RMS_EPS = 1e-6
GN_EPS = 64e-5

BATCH, SEQ = 32, 256
DEC_BATCH, DEC_SEQ = 8, 2048
N_CTX = BATCH * SEQ
N_SMP = DEC_BATCH * DEC_SEQ
NTOK = N_CTX + N_SMP
COND_ROWS = 16

VMEM_LIMIT = 56 * 1024 * 1024

F32 = jnp.float32
BF16 = jnp.bfloat16


def _dot(a, b):
    return jnp.dot(a.astype(BF16), b.astype(BF16), preferred_element_type=F32)


def _rms(x, g):
    return (x * lax.rsqrt(jnp.mean(x * x, axis=-1, keepdims=True) + RMS_EPS)) * g


def _cond_row(start_row):
    return jnp.where(start_row < N_CTX, 0, 1 + (start_row - N_CTX) // DEC_SEQ)


def _split_sum(x, ones):
    hi = x.astype(BF16)
    lo = (x - hi.astype(F32)).astype(BF16)
    return (jnp.dot(hi, ones, preferred_element_type=F32)
            + jnp.dot(lo, ones, preferred_element_type=F32))


def _mod_kernel(c_ref, w_ref, b_ref, o_ref):
    c = c_ref[...]
    o_ref[0] = _dot(jax.nn.silu(c), w_ref[0]) + b_ref[0]


def _modulation(cond, w_mod, b_mod):
    nb = N_MOD * D // D
    return pl.pallas_call(
        _mod_kernel,
        out_shape=jax.ShapeDtypeStruct((DEPTH, COND_ROWS, N_MOD * D), F32),
        grid=(DEPTH, nb),
        in_specs=[
            pl.BlockSpec((COND_ROWS, D), lambda l, n: (0, 0)),
            pl.BlockSpec((1, D, D), lambda l, n: (l, 0, n)),
            pl.BlockSpec((1, 1, D), lambda l, n: (l, 0, n)),
        ],
        out_specs=pl.BlockSpec((1, COND_ROWS, D), lambda l, n: (l, 0, n)),
        compiler_params=pltpu.CompilerParams(
            dimension_semantics=("arbitrary", "arbitrary"), vmem_limit_bytes=VMEM_LIMIT),
        name="modulation",
    )(cond, w_mod, b_mod.reshape(DEPTH, 1, N_MOD * D))


FFN_TM = 512
FFN_TF = 1408


def _ffn_kernel(*refs, k0, pending_k):
    if pending_k is None:
        x_ref, mod_ref, gpre_ref, gpost_ref, w1_ref, w3_ref, w2_ref, o_ref, h_sc, acc_sc, x1_sc = refs
    else:
        (x_ref, mod_ref, gpre_ref, gpost_ref, w1_ref, w3_ref, w2_ref, d_ref, gd_ref,
         o_ref, h_sc, acc_sc, x1_sc) = refs
    f = pl.program_id(1)

    @pl.when(f == 0)
    def _():
        x = x_ref[...]
        if pending_k is not None:
            x = x + mod_ref[0, pending_k:pending_k + 1, :] * _rms(d_ref[...], gd_ref[...])
        x1_sc[...] = x
        h = _rms(x, gpre_ref[...]) * (1.0 + mod_ref[0, k0 + 1:k0 + 2, :]) + mod_ref[0, k0:k0 + 1, :]
        h_sc[...] = h.astype(BF16)
        acc_sc[...] = jnp.zeros_like(acc_sc)

    h = h_sc[...]
    u = jnp.dot(h, w1_ref[...], preferred_element_type=F32)
    g = jnp.dot(h, w3_ref[...], preferred_element_type=F32)
    a = (jax.nn.silu(u) * g).astype(BF16)
    acc_sc[...] += jnp.dot(a, w2_ref[...], preferred_element_type=F32)

    @pl.when(f == pl.num_programs(1) - 1)
    def _():
        o_ref[...] = x1_sc[...] + 0.5 * mod_ref[0, k0 + 2:k0 + 3, :] * _rms(acc_sc[...], gpost_ref[...])


def _ffn(x, mod, gpre, gpost, w1, w3, w2, k0, pending=None):
    tm, tf = FFN_TM, FFN_TF
    row = lambda i, f: (i, 0)
    in_specs = [
        pl.BlockSpec((tm, D), row),
        pl.BlockSpec((1, N_MOD, D), lambda i, f: (_cond_row(i * tm), 0, 0)),
        pl.BlockSpec((1, D), lambda i, f: (0, 0)),
        pl.BlockSpec((1, D), lambda i, f: (0, 0)),
        pl.BlockSpec((D, tf), lambda i, f: (0, f)),
        pl.BlockSpec((D, tf), lambda i, f: (0, f)),
        pl.BlockSpec((tf, D), lambda i, f: (f, 0)),
    ]
    args = [x, mod, gpre.reshape(1, D), gpost.reshape(1, D), w1, w3, w2]
    pending_k = None
    if pending is not None:
        delta, gd, pending_k = pending
        in_specs += [pl.BlockSpec((tm, D), row), pl.BlockSpec((1, D), lambda i, f: (0, 0))]
        args += [delta, gd.reshape(1, D)]
    return pl.pallas_call(
        functools.partial(_ffn_kernel, k0=k0, pending_k=pending_k),
        out_shape=jax.ShapeDtypeStruct((NTOK, D), F32),
        grid=(NTOK // tm, D_FF // tf),
        in_specs=in_specs,
        out_specs=pl.BlockSpec((tm, D), row),
        scratch_shapes=[pltpu.VMEM((tm, D), BF16), pltpu.VMEM((tm, D), F32), pltpu.VMEM((tm, D), F32)],
        compiler_params=pltpu.CompilerParams(
            dimension_semantics=("arbitrary", "arbitrary"), vmem_limit_bytes=VMEM_LIMIT),
        name="ffn",
    )(*args)


RW_TM = 256
HALO = 8


def _rwkv_proj_kernel(x_ref, xp_ref, xn_ref, mod_ref, gpre_ref, mu_ref, wrkv_ref, w1_ref, w2_ref, w0_ref,
                      a1_ref, a2_ref, a0_ref, g1_ref, g2_ref, kk_ref_, ka_ref, ones_ref,
                      r_out, v_out, kk_out, g_out, dec_out, kd_out, b_out):
    i = pl.program_id(0)
    shift = mod_ref[0, 3:4, :]
    scale = 1.0 + mod_ref[0, 4:5, :]
    gpre = gpre_ref[...]
    premod = lambda x: _rms(x, gpre) * scale + shift

    k = i - N_CTX // RW_TM
    per_seq = DEC_SEQ // RW_TM
    is_ctx = i < N_CTX // RW_TM
    first = jnp.logical_or(is_ctx, k % per_seq == 0)
    last = jnp.logical_or(is_ctx, k % per_seq == per_seq - 1)

    h = premod(x_ref[...])
    hp = jnp.where(first, 0.0, premod(xp_ref[...])[HALO - 1:HALO, :])
    hn = jnp.where(last, 0.0, premod(xn_ref[...])[0:1, :])
    rows = lax.broadcasted_iota(jnp.int32, (RW_TM, 1), 0)
    h_prev = jnp.where(rows == 0, hp, pltpu.roll(h, 1, 0))
    h_next = jnp.where(rows == RW_TM - 1, hn, pltpu.roll(h, RW_TM - 1, 0))
    xx = 0.5 * (h_prev + h_next) - h
    mix = lambda n: h + xx * mu_ref[n:n + 1, :]

    r = _dot(mix(0), wrkv_ref[0])
    kx = _dot(mix(2), wrkv_ref[1])
    v = _dot(mix(3), wrkv_ref[2])
    g = _dot(jax.nn.sigmoid(_dot(mix(5), g1_ref[...])), g2_ref[...])

    z = w0_ref[...] + _dot(jnp.tanh(_dot(mix(1), w1_ref[...])), w2_ref[...])
    wlog = -jax.nn.softplus(-z) - 0.5
    dec = jnp.exp(-jnp.exp(wlog))
    a = jax.nn.sigmoid(a0_ref[...] + _dot(_dot(mix(4), a1_ref[...]), a2_ref[...]))

    kkraw = kx * kk_ref_[...]
    nrm = jnp.sqrt(_split_sum(kkraw * kkraw, ones_ref[...]))
    kk = kkraw / jnp.maximum(nrm, 1e-12)

    r_out[...] = r
    v_out[...] = v
    kk_out[...] = kk
    g_out[...] = g
    dec_out[...] = dec
    ka = ka_ref[...]
    for d in range(2):
        a_d = a[:, d * D:(d + 1) * D]
        kd_out[:, d * D:(d + 1) * D] = kx * (1.0 + (a_d - 1.0) * ka)
        b_out[:, d * D:(d + 1) * D] = kk * a_d


def _blockdiag2(m):
    z = jnp.zeros_like(m[0])
    return jnp.concatenate([jnp.concatenate([m[0], z], axis=1), jnp.concatenate([z, m[1]], axis=1)], axis=0)


def _head_ones():
    idx = np.arange(D) // HEAD
    return jnp.asarray((idx[:, None] == idx[None, :]).astype(np.float32), dtype=BF16)


def _rwkv_proj(x, mod, gpre, p):
    tm = RW_TM
    nblk = NTOK // HALO
    row = lambda i: (i, 0)
    full2 = lambda i: (0, 0)
    full3 = lambda i: (0, 0, 0)
    w1c = jnp.concatenate([p['w1'][0], p['w1'][1]], axis=1).astype(BF16)
    a1c = jnp.concatenate([p['a1'][0], p['a1'][1]], axis=1).astype(BF16)
    w2bd = _blockdiag2(p['w2']).astype(BF16)
    a2bd = _blockdiag2(p['a2']).astype(BF16)
    lo = DEC_BATCH
    del lo
    outs = pl.pallas_call(
        _rwkv_proj_kernel,
        out_shape=[jax.ShapeDtypeStruct((NTOK, D), F32)] * 4 + [jax.ShapeDtypeStruct((NTOK, 2 * D), F32)] * 3,
        grid=(NTOK // tm,),
        in_specs=[
            pl.BlockSpec((tm, D), row),
            pl.BlockSpec((HALO, D), lambda i: (jnp.maximum(i * (tm // HALO) - 1, 0), 0)),
            pl.BlockSpec((HALO, D), lambda i: (jnp.minimum((i + 1) * (tm // HALO), nblk - 1), 0)),
            pl.BlockSpec((1, N_MOD, D), lambda i: (_cond_row(i * tm), 0, 0)),
            pl.BlockSpec((1, D), full2),
            pl.BlockSpec((6, D), full2),
            pl.BlockSpec((3, D, D), full3),
            pl.BlockSpec((D, 128), full2),
            pl.BlockSpec((128, 2 * D), full2),
            pl.BlockSpec((1, 2 * D), full2),
            pl.BlockSpec((D, 128), full2),
            pl.BlockSpec((128, 2 * D), full2),
            pl.BlockSpec((1, 2 * D), full2),
            pl.BlockSpec((D, 128), full2),
            pl.BlockSpec((128, D), full2),
            pl.BlockSpec((1, D), full2),
            pl.BlockSpec((1, D), full2),
            pl.BlockSpec((D, D), full2),
        ],
        out_specs=[pl.BlockSpec((tm, D), row)] * 4 + [pl.BlockSpec((tm, 2 * D), row)] * 3,
        compiler_params=pltpu.CompilerParams(
            dimension_semantics=("arbitrary",), vmem_limit_bytes=VMEM_LIMIT),
        name="rwkv_proj",
    )(x, x, x, mod, gpre.reshape(1, D), p['mu'], p['w_rkv'].astype(BF16), w1c, w2bd,
      p['w0'].reshape(1, 2 * D), a1c, a2bd, p['a0'].reshape(1, 2 * D),
      p['g1'].astype(BF16), p['g2'].astype(BF16), p['k_k'].reshape(1, D), p['k_a'].reshape(1, D),
      _head_ones())
    return outs


SCAN_TB = 4
SCAN_IQ = 4


def _scan_kernel(*refs, nj, jl, zero_init):
    if zero_init:
        dec_ref, b_ref, k_ref, r_ref, kk_ref, v_ref, y_ref, s_ref = refs
    else:
        dec_ref, b_ref, k_ref, r_ref, kk_ref, v_ref, s0_ref, y_ref, s_ref = refs
    t_blk = pl.program_id(1)
    ib = y_ref.shape[1]

    @pl.when(t_blk == 0)
    def _():
        if zero_init:
            s_ref[...] = jnp.zeros_like(s_ref)
        else:
            s_ref[...] = s0_ref[...]

    def key_sum(x):
        step = 8 // jl
        while step < 8:
            x = x + pltpu.roll(x, step, 0)
            step *= 2
        return x

    def time_step(t, carry):
        def i_tile(it, carry2):
            i0 = it * SCAN_IQ
            sa = [jnp.zeros((8, 128), F32) for _ in range(SCAN_IQ)]
            for j in range(nj):
                kkj = kk_ref[t, j]
                for q in range(SCAN_IQ):
                    sa[q] = sa[q] + s_ref[j, i0 + q] * kkj
            sa = [-key_sum(s) for s in sa]
            vv = [v_ref[t, i0 + q] for q in range(SCAN_IQ)]
            y = [jnp.zeros((8, 128), F32) for _ in range(SCAN_IQ)]
            for j in range(nj):
                wj = dec_ref[t, j]
                bj = b_ref[t, j]
                kj = k_ref[t, j]
                rj = r_ref[t, j]
                for q in range(SCAN_IQ):
                    s = s_ref[j, i0 + q] * wj + sa[q] * bj + vv[q] * kj
                    s_ref[j, i0 + q] = s
                    y[q] = y[q] + s * rj
            for q in range(SCAN_IQ):
                y_ref[t, i0 + q] = key_sum(y[q])
            return carry2
        return lax.fori_loop(0, ib // SCAN_IQ, i_tile, carry)

    lax.fori_loop(0, SCAN_TB, time_step, 0)


def _scan(dec, b, k, r, kk, v, s0, n_iblk):
    T, nj = dec.shape[0], dec.shape[1]
    jl = HEAD // nj
    ib = HEAD // n_iblk
    jspec = pl.BlockSpec((SCAN_TB, nj, 8, 128), lambda c, t: (t, 0, 0, 0))
    ispec = pl.BlockSpec((SCAN_TB, ib, 8, 128), lambda c, t: (t, c, 0, 0))
    sspec = pl.BlockSpec((nj, ib, 8, 128), lambda c, t: (0, c, 0, 0))
    zero_init = s0 is None
    args = [dec, b, k, r, kk, v] + ([] if zero_init else [s0])
    return pl.pallas_call(
        functools.partial(_scan_kernel, nj=nj, jl=jl, zero_init=zero_init),
        out_shape=[jax.ShapeDtypeStruct((T, HEAD, 8, 128), F32), jax.ShapeDtypeStruct((nj, HEAD, 8, 128), F32)],
        grid=(n_iblk, T // SCAN_TB),
        in_specs=[jspec] * 5 + [ispec] + ([] if zero_init else [sspec]),
        out_specs=[ispec, sspec],
        compiler_params=pltpu.CompilerParams(
            dimension_semantics=("arbitrary", "arbitrary"), vmem_limit_bytes=VMEM_LIMIT),
        name="wkv_scan",
    )(*args)


def _wkv(r, v, kk, dec, kd, b, state_f, state_b):
    def heads(x, n, t):
        return x.reshape(n, t, -1, NHEAD, HEAD)

    def both_dirs(x, n, t, per_dir):
        x = heads(x, n, t)
        fwd = x[:, :, 0]
        bwd = jnp.flip(x[:, :, 1 if per_dir else 0], axis=1)
        return jnp.stack([fwd, bwd])

    def ctx_layout(x, per_dir):
        x = both_dirs(x[:N_CTX], BATCH, SEQ, per_dir)
        return x.transpose(2, 4, 0, 1, 3).reshape(SEQ, HEAD, 8, 128)

    def smp_key_layout(x, per_dir):
        x = both_dirs(x[N_CTX:], DEC_BATCH, DEC_SEQ, per_dir)
        x = x.transpose(2, 4, 0, 1, 3).reshape(DEC_SEQ, 16, 4, 2, 128)
        return x.reshape(DEC_SEQ, 16, 8, 128)

    def smp_val_layout(x):
        x = both_dirs(x[N_CTX:], DEC_BATCH, DEC_SEQ, False)
        x = x.transpose(2, 4, 0, 1, 3).reshape(DEC_SEQ, HEAD, 1, 2, 128)
        return jnp.broadcast_to(x, (DEC_SEQ, HEAD, 4, 2, 128)).reshape(DEC_SEQ, HEAD, 8, 128)

    y_c, s_c = _scan(ctx_layout(dec, True), ctx_layout(b, True), ctx_layout(kd, True),
                     ctx_layout(r, False), ctx_layout(kk, False), ctx_layout(v, False), None, 2)
    s0 = jnp.stack([state_f, state_b])
    s0 = s0.transpose(4, 3, 0, 1, 2).reshape(16, 4, HEAD, 2, 128).transpose(0, 2, 1, 3, 4)
    s0 = s0.reshape(16, HEAD, 8, 128)
    y_s, _ = _scan(smp_key_layout(dec, True), smp_key_layout(b, True), smp_key_layout(kd, True),
                   smp_key_layout(r, False), smp_key_layout(kk, False), smp_val_layout(v), s0, 1)

    y_c = y_c.reshape(SEQ, HEAD, 2, BATCH, NHEAD).transpose(2, 3, 0, 4, 1)
    y_s = y_s.reshape(DEC_SEQ, HEAD, 4, 2, DEC_BATCH, NHEAD)[:, :, 0].transpose(2, 3, 0, 4, 1)
    y_f = jnp.concatenate([y_c[0].reshape(N_CTX, D), y_s[0].reshape(N_SMP, D)])
    y_b = jnp.concatenate([jnp.flip(y_c[1], axis=1).reshape(N_CTX, D),
                           jnp.flip(y_s[1], axis=1).reshape(N_SMP, D)])
    s_c = s_c.reshape(HEAD, HEAD, 2, BATCH, NHEAD).transpose(2, 3, 4, 1, 0)
    return y_f, y_b, s_c[0][:, None], s_c[1][:, None]


RO_TM = 512


def _rwkv_out_kernel(x_ref, yf_ref, yb_ref, r_ref, v_ref, kd_ref, g_ref, mod_ref, gnw_ref, gnb_ref,
                     rk_ref, wo_ref, gpost_ref, ones_ref, o_ref):
    ones = ones_ref[...]
    ys = yf_ref[...] + yb_ref[...]
    mean = _split_sum(ys, ones) * (1.0 / HEAD)
    c = ys - mean
    var = _split_sum(c * c, ones) * (1.0 / HEAD)
    yn = c * lax.rsqrt(var + GN_EPS) * gnw_ref[...] + gnb_ref[...]
    kd = kd_ref[...]
    bonus = _split_sum(r_ref[...] * (kd[:, :D] + kd[:, D:]) * rk_ref[...], ones) * v_ref[...]
    out = _dot((yn + bonus) * g_ref[...], wo_ref[...])
    o_ref[...] = x_ref[...] + mod_ref[0, 5:6, :] * _rms(out, gpost_ref[...])


def _rwkv_out(x, y_f, y_b, r, v, kd, g, mod, p, gpost):
    tm = RO_TM
    row = lambda i: (i, 0)
    full2 = lambda i: (0, 0)
    vec = pl.BlockSpec((1, D), full2)
    return pl.pallas_call(
        _rwkv_out_kernel,
        out_shape=jax.ShapeDtypeStruct((NTOK, D), F32),
        grid=(NTOK // tm,),
        in_specs=[pl.BlockSpec((tm, D), row)] * 5 + [pl.BlockSpec((tm, 2 * D), row), pl.BlockSpec((tm, D), row),
                  pl.BlockSpec((1, N_MOD, D), lambda i: (_cond_row(i * tm), 0, 0)),
                  vec, vec, vec, pl.BlockSpec((D, D), full2), vec, pl.BlockSpec((D, D), full2)],
        out_specs=pl.BlockSpec((tm, D), row),
        compiler_params=pltpu.CompilerParams(
            dimension_semantics=("arbitrary",), vmem_limit_bytes=VMEM_LIMIT),
        name="rwkv_out",
    )(x, y_f, y_b, r, v, kd, g, mod, p['gn_w'].reshape(1, D), p['gn_b'].reshape(1, D),
      p['r_k'].reshape(1, D), p['w_o'].astype(BF16), gpost.reshape(1, D), _head_ones())


def _box_bounds(length, win):
    t = np.arange(length)
    lo = np.clip(t - win // 2, 0, length)
    hi = np.clip(t + win - win // 2, 0, length)
    return lo, hi


def _box_matrix(length, win):
    lo, hi = _box_bounds(length, win)
    s = np.arange(length)
    a = ((s[None, :] >= lo[:, None]) & (s[None, :] < hi[:, None])).astype(np.float32)
    return a, (hi - lo).astype(np.float32)


def _pool_constants(grid):
    mats, inv = [], []
    for win in POOL_WINDOWS:
        if grid:
            ar, cr = _box_matrix(DEC_SEQ // GRID_W, win)
            ac, cc = _box_matrix(GRID_W, win)
            mats.append(np.kron(ar, ac))
            inv.append(1.0 / np.kron(cr, cc))
        else:
            a, c = _box_matrix(SEQ, win)
            mats.append(a)
            inv.append(1.0 / c)
    mats = jnp.asarray(np.stack(mats), dtype=BF16)
    inv = jnp.asarray(np.stack(inv)[:, :, None], dtype=F32)
    return mats, inv


def _pool_kernel(xf_ref, xg_ref, mod_ref, gpre_ref, a_ref, inv_ref, pw_ref, ps_ref, o_ref):
    xf = xf_ref[...]
    rs = lax.rsqrt(jnp.mean(xf * xf, axis=-1, keepdims=True) + RMS_EPS)
    part = (xg_ref[...] * rs) * gpre_ref[...] * (1.0 + mod_ref[0, 4:5, :]) + mod_ref[0, 3:4, :]
    m = _split_sum_left(a_ref[0], part) * inv_ref[0]
    o_ref[...] = _dot(m - part, pw_ref[0]) * ps_ref[...]


def _split_sum_left(ones, x):
    hi = x.astype(BF16)
    lo = (x - hi.astype(F32)).astype(BF16)
    return (jnp.dot(ones, hi, preferred_element_type=F32)
            + jnp.dot(ones, lo, preferred_element_type=F32))


def _pool(x, mod, gpre, pool_w, pool_scale, t, seq0, nseq, grid):
    mats, inv = _pool_constants(grid)
    gc = POOL_GC
    return pl.pallas_call(
        _pool_kernel,
        out_shape=jax.ShapeDtypeStruct((nseq * t, D), F32),
        grid=(4, nseq),
        in_specs=[
            pl.BlockSpec((t, D), lambda g, s: (seq0 + s, 0)),
            pl.BlockSpec((t, gc), lambda g, s: (seq0 + s, g)),
            pl.BlockSpec((1, N_MOD, gc), lambda g, s: (_cond_row((seq0 + s) * t), 0, g)),
            pl.BlockSpec((1, gc), lambda g, s: (0, g)),
            pl.BlockSpec((1, t, t), lambda g, s: (g, 0, 0)),
            pl.BlockSpec((1, t, 1), lambda g, s: (g, 0, 0)),
            pl.BlockSpec((1, gc, gc), lambda g, s: (g, 0, 0)),
            pl.BlockSpec((1, gc), lambda g, s: (0, g)),
        ],
        out_specs=pl.BlockSpec((t, gc), lambda g, s: (s, g)),
        compiler_params=pltpu.CompilerParams(
            dimension_semantics=("arbitrary", "arbitrary"), vmem_limit_bytes=VMEM_LIMIT),
        name="pool_grid" if grid else "pool_seq",
    )(x, x, mod, gpre.reshape(1, D), mats, inv, pool_w.astype(BF16), pool_scale.reshape(1, D))


def kernel(x_prompt, x_sample, c, state_ctx_fwd, state_ctx_bwd, c_ctx, w_mod, b_mod, norm_pre, norm_post,
           ffn_w1, ffn_w3, ffn_w2, rwkv_mu, rwkv_w_rkv, rwkv_w0, rwkv_w1, rwkv_w2, rwkv_a0, rwkv_a1,
           rwkv_a2, rwkv_g1, rwkv_g2, rwkv_k_k, rwkv_k_a, rwkv_r_k, rwkv_gn_w, rwkv_gn_b, rwkv_w_o,
           pool_w, pool_scale):
    x = jnp.concatenate([x_prompt.reshape(N_CTX, D), x_sample.reshape(N_SMP, D)])
    cond = jnp.concatenate([c_ctx[None], c, jnp.zeros((COND_ROWS - 1 - DEC_BATCH, D), F32)])
    mod = _modulation(cond, w_mod, b_mod).reshape(DEPTH, COND_ROWS, N_MOD, D)
    w1, w3, w2 = ffn_w1.astype(BF16), ffn_w3.astype(BF16), ffn_w2.astype(BF16)

    x = _ffn(x, mod[0], norm_pre[0, 0], norm_post[0, 0], w1[0, 0], w3[0, 0], w2[0, 0], 0)
    p = dict(mu=rwkv_mu[0], w_rkv=rwkv_w_rkv[0], w0=rwkv_w0[0], w1=rwkv_w1[0], w2=rwkv_w2[0],
             a0=rwkv_a0[0], a1=rwkv_a1[0], a2=rwkv_a2[0], g1=rwkv_g1[0], g2=rwkv_g2[0],
             k_k=rwkv_k_k[0], k_a=rwkv_k_a[0], r_k=rwkv_r_k[0], gn_w=rwkv_gn_w[0], gn_b=rwkv_gn_b[0],
             w_o=rwkv_w_o[0])
    r, v, kk, g, dec, kd, b = _rwkv_proj(x, mod[0], norm_pre[0, 1], p)
    y_f, y_b, new_f, new_b = _wkv(r, v, kk, dec, kd, b, state_ctx_fwd[:, 0], state_ctx_bwd[:, 0])
    x = _rwkv_out(x, y_f, y_b, r, v, kd, g, mod[0], p, norm_post[0, 1])
    x = _ffn(x, mod[0], norm_pre[0, 2], norm_post[0, 2], w1[0, 1], w3[0, 1], w2[0, 1], 6)

    x = _ffn(x, mod[1], norm_pre[1, 0], norm_post[1, 0], w1[1, 0], w3[1, 0], w2[1, 0], 0)
    pc = _pool(x, mod[1], norm_pre[1, 1], pool_w[0], pool_scale[0], SEQ, 0, BATCH, False)
    ps = _pool(x, mod[1], norm_pre[1, 1], pool_w[0], pool_scale[0], DEC_SEQ, N_CTX // DEC_SEQ, DEC_BATCH, True)
    delta = jnp.concatenate([pc, ps])
    x = _ffn(x, mod[1], norm_pre[1, 2], norm_post[1, 2], w1[1, 1], w3[1, 1], w2[1, 1], 6,
             pending=(delta, norm_post[1, 1], 5))

    return (x[:N_CTX].reshape(BATCH, SEQ, D), x[N_CTX:].reshape(DEC_BATCH, DEC_SEQ, D), new_f, new_b)
```

```python
import functools

import numpy as np
import jax
import jax.numpy as jnp
from jax import lax
from jax.experimental import pallas as pl
from jax.experimental.pallas import tpu as pltpu

D = 1024
HEAD = 64
NHEAD = D // HEAD
D_FF = 2816
DEPTH = 2
N_MOD = 9
POOL_WINDOWS = (2, 4, 8, 16)
POOL_GC = D // 4
GRID_W = 64
RMS_EPS = 1e-6
GN_EPS = 64e-5

BATCH, SEQ = 32, 256
DEC_BATCH, DEC_SEQ = 8, 2048
N_CTX = BATCH * SEQ
N_SMP = DEC_BATCH * DEC_SEQ
NTOK = N_CTX + N_SMP
COND_ROWS = 16

VMEM_LIMIT = 56 * 1024 * 1024

F32 = jnp.float32
BF16 = jnp.bfloat16


def _dot(a, b):
    return jnp.dot(a.astype(BF16), b.astype(BF16), preferred_element_type=F32)


def _rms(x, g):
    return (x * lax.rsqrt(jnp.mean(x * x, axis=-1, keepdims=True) + RMS_EPS)) * g


def _cond_row(start_row):
    return jnp.where(start_row < N_CTX, 0, 1 + (start_row - N_CTX) // DEC_SEQ)


def _split(x):
    hi = x.astype(BF16)
    return hi, (x - hi.astype(F32)).astype(BF16)


def _split_sum(x, ones):
    hi, lo = _split(x)
    return (jnp.dot(hi, ones, preferred_element_type=F32)
            + jnp.dot(lo, ones, preferred_element_type=F32))


def _split_sum_left(ones, x):
    hi, lo = _split(x)
    return (jnp.dot(ones, hi, preferred_element_type=F32)
            + jnp.dot(ones, lo, preferred_element_type=F32))


def _mod_kernel(c_ref, w_ref, b_ref, o_ref):
    c = c_ref[...]
    o_ref[0] = _dot(jax.nn.silu(c), w_ref[0]) + b_ref[0]


def _modulation(cond, w_mod, b_mod):
    return pl.pallas_call(
        _mod_kernel,
        out_shape=jax.ShapeDtypeStruct((DEPTH, COND_ROWS, N_MOD * D), F32),
        grid=(DEPTH, N_MOD),
        in_specs=[
            pl.BlockSpec((COND_ROWS, D), lambda l, n: (0, 0)),
            pl.BlockSpec((1, D, D), lambda l, n: (l, 0, n)),
            pl.BlockSpec((1, 1, D), lambda l, n: (l, 0, n)),
        ],
        out_specs=pl.BlockSpec((1, COND_ROWS, D), lambda l, n: (l, 0, n)),
        compiler_params=pltpu.CompilerParams(
            dimension_semantics=("arbitrary", "arbitrary"), vmem_limit_bytes=VMEM_LIMIT),
        name="modulation",
    )(cond, w_mod, b_mod.reshape(DEPTH, 1, N_MOD * D))


FFN_TM = 512
FFN_TF = 1408


def _ffn_kernel(*refs, k0, pending_k):
    if pending_k is None:
        x_ref, mod_ref, gpre_ref, gpost_ref, w1_ref, w3_ref, w2_ref, o_ref, h_sc, acc_sc, x1_sc = refs
    else:
        (x_ref, mod_ref, gpre_ref, gpost_ref, w1_ref, w3_ref, w2_ref, d_ref, gd_ref,
         o_ref, h_sc, acc_sc, x1_sc) = refs
    f = pl.program_id(1)

    @pl.when(f == 0)
    def _():
        x = x_ref[...]
        if pending_k is not None:
            x = x + mod_ref[0, pending_k:pending_k + 1, :] * _rms(d_ref[...], gd_ref[...])
        x1_sc[...] = x
        h = _rms(x, gpre_ref[...]) * (1.0 + mod_ref[0, k0 + 1:k0 + 2, :]) + mod_ref[0, k0:k0 + 1, :]
        h_sc[...] = h.astype(BF16)
        acc_sc[...] = jnp.zeros_like(acc_sc)

    h = h_sc[...]
    u = jnp.dot(h, w1_ref[...], preferred_element_type=F32)
    g = jnp.dot(h, w3_ref[...], preferred_element_type=F32)
    a = (jax.nn.silu(u) * g).astype(BF16)
    acc_sc[...] += jnp.dot(a, w2_ref[...], preferred_element_type=F32)

    @pl.when(f == pl.num_programs(1) - 1)
    def _():
        o_ref[...] = x1_sc[...] + 0.5 * mod_ref[0, k0 + 2:k0 + 3, :] * _rms(acc_sc[...], gpost_ref[...])


def _ffn(x, mod, gpre, gpost, w1, w3, w2, k0, pending=None):
    tm, tf = FFN_TM, FFN_TF
    row = lambda i, f: (i, 0)
    in_specs = [
        pl.BlockSpec((tm, D), row),
        pl.BlockSpec((1, N_MOD, D), lambda i, f: (_cond_row(i * tm), 0, 0)),
        pl.BlockSpec((1, D), lambda i, f: (0, 0)),
        pl.BlockSpec((1, D), lambda i, f: (0, 0)),
        pl.BlockSpec((D, tf), lambda i, f: (0, f)),
        pl.BlockSpec((D, tf), lambda i, f: (0, f)),
        pl.BlockSpec((tf, D), lambda i, f: (f, 0)),
    ]
    args = [x, mod, gpre.reshape(1, D), gpost.reshape(1, D), w1, w3, w2]
    pending_k = None
    if pending is not None:
        delta, gd, pending_k = pending
        in_specs += [pl.BlockSpec((tm, D), row), pl.BlockSpec((1, D), lambda i, f: (0, 0))]
        args += [delta, gd.reshape(1, D)]
    return pl.pallas_call(
        functools.partial(_ffn_kernel, k0=k0, pending_k=pending_k),
        out_shape=jax.ShapeDtypeStruct((NTOK, D), F32),
        grid=(NTOK // tm, D_FF // tf),
        in_specs=in_specs,
        out_specs=pl.BlockSpec((tm, D), row),
        scratch_shapes=[pltpu.VMEM((tm, D), BF16), pltpu.VMEM((tm, D), F32), pltpu.VMEM((tm, D), F32)],
        compiler_params=pltpu.CompilerParams(
            dimension_semantics=("arbitrary", "arbitrary"), vmem_limit_bytes=VMEM_LIMIT),
        name="ffn",
    )(*args)


RW_TM = 256
HALO = 8


def _rwkv_proj_kernel(x_ref, xp_ref, xn_ref, mod_ref, gpre_ref, mu_ref, wrkv_ref, w1_ref, w2_ref, w0_ref,
                      a1_ref, a2_ref, a0_ref, g1_ref, g2_ref, k_k_ref, k_a_ref, ones_ref,
                      r_out, v_out, kk_out, g_out, lw_out, kd_out, b_out):
    i = pl.program_id(0)
    shift = mod_ref[0, 3:4, :]
    scale = 1.0 + mod_ref[0, 4:5, :]
    gpre = gpre_ref[...]
    premod = lambda x: _rms(x, gpre) * scale + shift

    k = i - N_CTX // RW_TM
    per_seq = DEC_SEQ // RW_TM
    is_ctx = i < N_CTX // RW_TM
    first = jnp.logical_or(is_ctx, k % per_seq == 0)
    last = jnp.logical_or(is_ctx, k % per_seq == per_seq - 1)

    h = premod(x_ref[...])
    hp = jnp.where(first, 0.0, premod(xp_ref[...])[HALO - 1:HALO, :])
    hn = jnp.where(last, 0.0, premod(xn_ref[...])[0:1, :])
    rows = lax.broadcasted_iota(jnp.int32, (RW_TM, 1), 0)
    h_prev = jnp.where(rows == 0, hp, pltpu.roll(h, 1, 0))
    h_next = jnp.where(rows == RW_TM - 1, hn, pltpu.roll(h, RW_TM - 1, 0))
    xx = 0.5 * (h_prev + h_next) - h
    mix = lambda n: h + xx * mu_ref[n:n + 1, :]

    r = _dot(mix(0), wrkv_ref[0])
    kx = _dot(mix(2), wrkv_ref[1])
    v = _dot(mix(3), wrkv_ref[2])
    g = _dot(jax.nn.sigmoid(_dot(mix(5), g1_ref[...])), g2_ref[...])

    z = w0_ref[...] + _dot(jnp.tanh(_dot(mix(1), w1_ref[...])), w2_ref[...])
    wlog = -jax.nn.softplus(-z) - 0.5
    a = jax.nn.sigmoid(a0_ref[...] + _dot(_dot(mix(4), a1_ref[...]), a2_ref[...]))

    kkraw = kx * k_k_ref[...]
    nrm = jnp.sqrt(_split_sum(kkraw * kkraw, ones_ref[...]))
    kk = kkraw / jnp.maximum(nrm, 1e-12)

    r_out[...] = r
    v_out[...] = v
    kk_out[...] = kk
    g_out[...] = g
    lw_out[...] = -jnp.exp(wlog)
    ka = k_a_ref[...]
    for d in range(2):
        a_d = a[:, d * D:(d + 1) * D]
        kd_out[:, d * D:(d + 1) * D] = kx * (1.0 + (a_d - 1.0) * ka)
        b_out[:, d * D:(d + 1) * D] = kk * a_d


def _blockdiag2(m):
    z = jnp.zeros_like(m[0])
    return jnp.concatenate([jnp.concatenate([m[0], z], axis=1), jnp.concatenate([z, m[1]], axis=1)], axis=0)


def _head_ones():
    idx = np.arange(D) // HEAD
    return jnp.asarray((idx[:, None] == idx[None, :]).astype(np.float32), dtype=BF16)


def _rwkv_proj(x, mod, gpre, p):
    tm = RW_TM
    nblk = NTOK // HALO
    row = lambda i: (i, 0)
    full2 = lambda i: (0, 0)
    full3 = lambda i: (0, 0, 0)
    w1c = jnp.concatenate([p['w1'][0], p['w1'][1]], axis=1).astype(BF16)
    a1c = jnp.concatenate([p['a1'][0], p['a1'][1]], axis=1).astype(BF16)
    w2bd = _blockdiag2(p['w2']).astype(BF16)
    a2bd = _blockdiag2(p['a2']).astype(BF16)
    return pl.pallas_call(
        _rwkv_proj_kernel,
        out_shape=[jax.ShapeDtypeStruct((NTOK, D), F32)] * 4 + [jax.ShapeDtypeStruct((NTOK, 2 * D), F32)] * 3,
        grid=(NTOK // tm,),
        in_specs=[
            pl.BlockSpec((tm, D), row),
            pl.BlockSpec((HALO, D), lambda i: (jnp.maximum(i * (tm // HALO) - 1, 0), 0)),
            pl.BlockSpec((HALO, D), lambda i: (jnp.minimum((i + 1) * (tm // HALO), nblk - 1), 0)),
            pl.BlockSpec((1, N_MOD, D), lambda i: (_cond_row(i * tm), 0, 0)),
            pl.BlockSpec((1, D), full2),
            pl.BlockSpec((6, D), full2),
            pl.BlockSpec((3, D, D), full3),
            pl.BlockSpec((D, 128), full2),
            pl.BlockSpec((128, 2 * D), full2),
            pl.BlockSpec((1, 2 * D), full2),
            pl.BlockSpec((D, 128), full2),
            pl.BlockSpec((128, 2 * D), full2),
            pl.BlockSpec((1, 2 * D), full2),
            pl.BlockSpec((D, 128), full2),
            pl.BlockSpec((128, D), full2),
            pl.BlockSpec((1, D), full2),
            pl.BlockSpec((1, D), full2),
            pl.BlockSpec((D, D), full2),
        ],
        out_specs=[pl.BlockSpec((tm, D), row)] * 4 + [pl.BlockSpec((tm, 2 * D), row)] * 3,
        compiler_params=pltpu.CompilerParams(
            dimension_semantics=("arbitrary",), vmem_limit_bytes=VMEM_LIMIT),
        name="rwkv_proj",
    )(x, x, x, mod, gpre.reshape(1, D), p['mu'], p['w_rkv'].astype(BF16), w1c, w2bd,
      p['w0'].reshape(1, 2 * D), a1c, a2bd, p['a0'].reshape(1, 2 * D),
      p['g1'].astype(BF16), p['g2'].astype(BF16), p['k_k'].reshape(1, D), p['k_a'].reshape(1, D),
      _head_ones())


CH = 32
GH = 4
GW = GH * HEAD
NG = NHEAD // GH
SR = GH * CH

_NT = (((1,), (1,)), ((), ()))
_TN = (((0,), (0,)), ((), ()))


def _scan_masks():
    rho = np.arange(SR)
    same = (rho[:, None] // CH) == (rho[None, :] // CH)
    lower = rho[None, :] < rho[:, None]
    strict = np.stack([same & lower, same & lower.T])
    incl = strict | np.eye(SR, dtype=bool)[None]
    t = np.arange(CH)
    tri = np.stack([t[None, :] <= t[:, None], t[None, :] >= t[:, None]])
    return (jnp.asarray(strict.astype(np.float32)), jnp.asarray(incl.astype(np.float32)),
            jnp.asarray(tri.astype(np.float32), dtype=BF16))


def _wkv_kernel(r_ref, v_ref, kk_ref, lw_ref, kd_ref, b_ref, strict_ref, incl_ref, tri_ref, s0_ref,
                y_ref, s_ref, *, n_ctx_chunks, ctx_chunks_per_seq, smp_chunks_per_seq):
    d = pl.program_id(0)
    q = pl.program_id(1)
    is_ctx = q < n_ctx_chunks
    c = jnp.where(is_ctx, q % ctx_chunks_per_seq, (q - n_ctx_chunks) % smp_chunks_per_seq)

    @pl.when(c == 0)
    def _():
        s_ref[...] = jnp.where(is_ctx, 0.0, s0_ref[...])

    strict = strict_ref[0] > 0.5
    incl = incl_ref[0] > 0.5
    lw = lw_ref[...]
    cum = _split_sum_left(tri_ref[0], lw)
    tot = jnp.where(d == 0, cum[CH - 1:CH, :], cum[0:1, :])
    e_neg = jnp.exp(-cum)
    e_rem = jnp.exp(tot - cum)
    e_tot = jnp.exp(tot)
    qt = kk_ref[...] * jnp.exp(cum - lw)
    rt = r_ref[...] * jnp.exp(cum)
    kd = kd_ref[...]
    bb = b_ref[...]
    kt, bt, kh, bh = kd * e_neg, bb * e_neg, kd * e_rem, bb * e_rem
    v = v_ref[...]

    lane_head = lax.broadcasted_iota(jnp.int32, (CH, GW), 1) // HEAD
    head_lanes = [lane_head == h for h in range(GH)]

    def stack(x):
        return jnp.concatenate([jnp.where(m, x, 0.0) for m in head_lanes], axis=0).astype(BF16)

    eye = (lax.broadcasted_iota(jnp.int32, (SR, SR), 0)
           == lax.broadcasted_iota(jnp.int32, (SR, SR), 1)).astype(F32)
    dot = functools.partial(jnp.dot, preferred_element_type=F32)
    dotg = functools.partial(lax.dot_general, preferred_element_type=F32)

    groups = range(NG)
    sls = [slice(g * GW, (g + 1) * GW) for g in groups]
    ops = [[stack(x[:, sl]) for x in (qt, rt, kt, bt, v, kh, bh)] for sl in sls]
    qs, rs, ks, bs, vs, khs, bhs = zip(*ops)
    gm = [dotg(jnp.concatenate([qs[g], rs[g]], axis=0), jnp.concatenate([bs[g], ks[g]], axis=0), _NT)
          for g in groups]
    s0 = [s_ref[0, 0, g] for g in groups]
    s0b = [s.astype(BF16) for s in s0]
    l = [jnp.where(strict, gm[g][:SR, :SR], 0.0) for g in groups]
    m = [jnp.where(strict, gm[g][:SR, SR:], 0.0).astype(BF16) for g in groups]
    pb = [jnp.where(incl, gm[g][SR:, :SR], 0.0).astype(BF16) for g in groups]
    p = [jnp.where(incl, gm[g][SR:, SR:], 0.0).astype(BF16) for g in groups]
    n = [eye - l[g] for g in groups]
    pw = [l[g].astype(BF16) for g in groups]
    mv = [dot(m[g], vs[g]).astype(BF16) for g in groups]
    for step in range(CH.bit_length() - 2):
        pw = [dot(pw[g], pw[g]).astype(BF16) for g in groups]
        if step == 0:
            acc = [dotg(rs[g], s0b[g], _NT) + dot(p[g], vs[g]) for g in groups]
        if step == 1:
            kv = [dotg(vs[g], khs[g], _TN) for g in groups]
        n = [n[g] + dot(n[g].astype(BF16), pw[g]) for g in groups]
    x = [dot(n[g].astype(BF16), jnp.concatenate([qs[g], mv[g]], axis=1)) for g in groups]
    z = [dotg(x[g][:, :GW].astype(BF16), s0b[g], _NT) for g in groups]
    u = [(-(z[g] + x[g][:, GW:])).astype(BF16) for g in groups]
    ys = [acc[g] + dot(pb[g], u[g]) for g in groups]
    for g in groups:
        s_ref[0, 0, g] = s0[g] * e_tot[:, sls[g]] + kv[g] + dotg(u[g], bhs[g], _TN)
    for g in groups:
        y = ys[g][0:CH]
        for h in range(1, GH):
            y = y + ys[g][h * CH:(h + 1) * CH]
        y_ref[:, sls[g]] = y


def _wkv(r, v, kk, lw, kd, b, state_f, state_b, n_ctx_seq=BATCH, t_ctx=SEQ, n_smp_seq=DEC_BATCH, t_smp=DEC_SEQ):
    ntok = n_ctx_seq * t_ctx + n_smp_seq * t_smp
    cps_c, cps_s = t_ctx // CH, t_smp // CH
    ncc = n_ctx_seq * cps_c
    nq = ncc + n_smp_seq * cps_s

    def seq_of(q):
        return jnp.where(q < ncc, q // cps_c, n_ctx_seq + (q - ncc) // cps_s)

    def row_block(d, q):
        ctx = q < ncc
        cps = jnp.where(ctx, cps_c, cps_s)
        c = jnp.where(ctx, q % cps_c, (q - ncc) % cps_s)
        return q - c + jnp.where(d == 0, c, cps - 1 - c)

    tok = pl.BlockSpec((CH, D), lambda d, q: (row_block(d, q), 0))
    tokd = pl.BlockSpec((CH, D), lambda d, q: (row_block(d, q), d))
    msk = pl.BlockSpec((1, SR, SR), lambda d, q: (d, 0, 0))
    tri = pl.BlockSpec((1, CH, CH), lambda d, q: (d, 0, 0))
    s0spec = pl.BlockSpec((1, 1, NG, GW, GW),
                          lambda d, q: (d, jnp.maximum(seq_of(q) - n_ctx_seq, 0), 0, 0, 0))
    sspec = pl.BlockSpec((1, 1, NG, GW, GW), lambda d, q: (d, seq_of(q), 0, 0, 0))

    s0 = jnp.stack([state_f, state_b]).reshape(2, n_smp_seq, NG, GH, HEAD, HEAD)
    eye = jnp.eye(GH, dtype=F32)
    s0 = (s0[:, :, :, :, :, None, :] * eye[None, None, None, :, None, :, None]).reshape(2, n_smp_seq, NG, GW, GW)

    strict, incl, trim = _scan_masks()
    y, s_fin = pl.pallas_call(
        functools.partial(_wkv_kernel, n_ctx_chunks=ncc, ctx_chunks_per_seq=cps_c, smp_chunks_per_seq=cps_s),
        out_shape=[jax.ShapeDtypeStruct((ntok, 2 * D), F32),
                   jax.ShapeDtypeStruct((2, n_ctx_seq + n_smp_seq, NG, GW, GW), F32)],
        grid=(2, nq),
        in_specs=[tok, tok, tok, tokd, tokd, tokd, msk, msk, tri, s0spec],
        out_specs=[tokd, sspec],
        compiler_params=pltpu.CompilerParams(
            dimension_semantics=("arbitrary", "arbitrary"), vmem_limit_bytes=VMEM_LIMIT),
        name="wkv_scan",
    )(r, v, kk, lw, kd, b, strict, incl, trim, s0)
    s_fin = s_fin[:, :n_ctx_seq].reshape(2, n_ctx_seq, NG, GH, HEAD, GH, HEAD)
    s_fin = jnp.stack([s_fin[:, :, :, h, :, h, :] for h in range(GH)], axis=3)
    s_fin = s_fin.reshape(2, n_ctx_seq, 1, NHEAD, HEAD, HEAD)
    return y, s_fin[0], s_fin[1]


RO_TM = 512


def _rwkv_out_kernel(x_ref, y_ref, r_ref, v_ref, kd_ref, g_ref, mod_ref, gnw_ref, gnb_ref,
                     rk_ref, wo_ref, gpost_ref, ones_ref, o_ref):
    ones = ones_ref[...]
    y = y_ref[...]
    ys = y[:, :D] + y[:, D:]
    mean = _split_sum(ys, ones) * (1.0 / HEAD)
    c = ys - mean
    var = _split_sum(c * c, ones) * (1.0 / HEAD)
    yn = c * lax.rsqrt(var + GN_EPS) * gnw_ref[...] + gnb_ref[...]
    kd = kd_ref[...]
    bonus = _split_sum(r_ref[...] * (kd[:, :D] + kd[:, D:]) * rk_ref[...], ones) * v_ref[...]
    out = _dot((yn + bonus) * g_ref[...], wo_ref[...])
    o_ref[...] = x_ref[...] + mod_ref[0, 5:6, :] * _rms(out, gpost_ref[...])


def _rwkv_out(x, y, r, v, kd, g, mod, p, gpost):
    tm = RO_TM
    row = lambda i: (i, 0)
    full2 = lambda i: (0, 0)
    vec = pl.BlockSpec((1, D), full2)
    tok = pl.BlockSpec((tm, D), row)
    tok2 = pl.BlockSpec((tm, 2 * D), row)
    return pl.pallas_call(
        _rwkv_out_kernel,
        out_shape=jax.ShapeDtypeStruct((NTOK, D), F32),
        grid=(NTOK // tm,),
        in_specs=[tok, tok2, tok, tok, tok2, tok,
                  pl.BlockSpec((1, N_MOD, D), lambda i: (_cond_row(i * tm), 0, 0)),
                  vec, vec, vec, pl.BlockSpec((D, D), full2), vec, pl.BlockSpec((D, D), full2)],
        out_specs=tok,
        compiler_params=pltpu.CompilerParams(
            dimension_semantics=("arbitrary",), vmem_limit_bytes=VMEM_LIMIT),
        name="rwkv_out",
    )(x, y, r, v, kd, g, mod, p['gn_w'].reshape(1, D), p['gn_b'].reshape(1, D),
      p['r_k'].reshape(1, D), p['w_o'].astype(BF16), gpost.reshape(1, D), _head_ones())


def _box_matrix(length, win):
    t = np.arange(length)
    lo = np.clip(t - win // 2, 0, length)
    hi = np.clip(t + win - win // 2, 0, length)
    a = ((t[None, :] >= lo[:, None]) & (t[None, :] < hi[:, None])).astype(np.float32)
    return a, (hi - lo).astype(np.float32)


def _pool_constants(grid):
    mats, inv = [], []
    for win in POOL_WINDOWS:
        if grid:
            ar, cr = _box_matrix(DEC_SEQ // GRID_W, win)
            ac, cc = _box_matrix(GRID_W, win)
            mats.append(np.kron(ar, ac))
            inv.append(1.0 / np.kron(cr, cc))
        else:
            a, c = _box_matrix(SEQ, win)
            mats.append(a)
            inv.append(1.0 / c)
    mats = jnp.asarray(np.stack(mats), dtype=BF16)
    inv = jnp.asarray(np.stack(inv)[:, :, None], dtype=F32)
    return mats, inv


def _pool_kernel(xf_ref, xg_ref, mod_ref, gpre_ref, a_ref, inv_ref, pw_ref, ps_ref, o_ref):
    xf = xf_ref[...]
    rs = lax.rsqrt(jnp.mean(xf * xf, axis=-1, keepdims=True) + RMS_EPS)
    part = (xg_ref[...] * rs) * gpre_ref[...] * (1.0 + mod_ref[0, 4:5, :]) + mod_ref[0, 3:4, :]
    m = _split_sum_left(a_ref[0], part) * inv_ref[0]
    o_ref[...] = _dot(m - part, pw_ref[0]) * ps_ref[...]


def _pool(x, mod, gpre, pool_w, pool_scale, t, seq0, nseq, grid):
    mats, inv = _pool_constants(grid)
    gc = POOL_GC
    return pl.pallas_call(
        _pool_kernel,
        out_shape=jax.ShapeDtypeStruct((nseq * t, D), F32),
        grid=(4, nseq),
        in_specs=[
            pl.BlockSpec((t, D), lambda g, s: (seq0 + s, 0)),
            pl.BlockSpec((t, gc), lambda g, s: (seq0 + s, g)),
            pl.BlockSpec((1, N_MOD, gc), lambda g, s: (_cond_row((seq0 + s) * t), 0, g)),
            pl.BlockSpec((1, gc), lambda g, s: (0, g)),
            pl.BlockSpec((1, t, t), lambda g, s: (g, 0, 0)),
            pl.BlockSpec((1, t, 1), lambda g, s: (g, 0, 0)),
            pl.BlockSpec((1, gc, gc), lambda g, s: (g, 0, 0)),
            pl.BlockSpec((1, gc), lambda g, s: (0, g)),
        ],
        out_specs=pl.BlockSpec((t, gc), lambda g, s: (s, g)),
        compiler_params=pltpu.CompilerParams(
            dimension_semantics=("arbitrary", "arbitrary"), vmem_limit_bytes=VMEM_LIMIT),
        name="pool_grid" if grid else "pool_seq",
    )(x, x, mod, gpre.reshape(1, D), mats, inv, pool_w.astype(BF16), pool_scale.reshape(1, D))


def kernel(x_prompt, x_sample, c, state_ctx_fwd, state_ctx_bwd, c_ctx, w_mod, b_mod, norm_pre, norm_post,
           ffn_w1, ffn_w3, ffn_w2, rwkv_mu, rwkv_w_rkv, rwkv_w0, rwkv_w1, rwkv_w2, rwkv_a0, rwkv_a1,
           rwkv_a2, rwkv_g1, rwkv_g2, rwkv_k_k, rwkv_k_a, rwkv_r_k, rwkv_gn_w, rwkv_gn_b, rwkv_w_o,
           pool_w, pool_scale):
    x = jnp.concatenate([x_prompt.reshape(N_CTX, D), x_sample.reshape(N_SMP, D)])
    cond = jnp.concatenate([c_ctx[None], c, jnp.zeros((COND_ROWS - 1 - DEC_BATCH, D), F32)])
    mod = _modulation(cond, w_mod, b_mod).reshape(DEPTH, COND_ROWS, N_MOD, D)
    w1, w3, w2 = ffn_w1.astype(BF16), ffn_w3.astype(BF16), ffn_w2.astype(BF16)

    x = _ffn(x, mod[0], norm_pre[0, 0], norm_post[0, 0], w1[0, 0], w3[0, 0], w2[0, 0], 0)
    p = dict(mu=rwkv_mu[0], w_rkv=rwkv_w_rkv[0], w0=rwkv_w0[0], w1=rwkv_w1[0], w2=rwkv_w2[0],
             a0=rwkv_a0[0], a1=rwkv_a1[0], a2=rwkv_a2[0], g1=rwkv_g1[0], g2=rwkv_g2[0],
             k_k=rwkv_k_k[0], k_a=rwkv_k_a[0], r_k=rwkv_r_k[0], gn_w=rwkv_gn_w[0], gn_b=rwkv_gn_b[0],
             w_o=rwkv_w_o[0])
    r, v, kk, g, lw, kd, b = _rwkv_proj(x, mod[0], norm_pre[0, 1], p)
    y, new_f, new_b = _wkv(r, v, kk, lw, kd, b, state_ctx_fwd[:, 0], state_ctx_bwd[:, 0])
    x = _rwkv_out(x, y, r, v, kd, g, mod[0], p, norm_post[0, 1])
    x = _ffn(x, mod[0], norm_pre[0, 2], norm_post[0, 2], w1[0, 1], w3[0, 1], w2[0, 1], 6)

    x = _ffn(x, mod[1], norm_pre[1, 0], norm_post[1, 0], w1[1, 0], w3[1, 0], w2[1, 0], 0)
    pc = _pool(x, mod[1], norm_pre[1, 1], pool_w[0], pool_scale[0], SEQ, 0, BATCH, False)
    ps = _pool(x, mod[1], norm_pre[1, 1], pool_w[0], pool_scale[0], DEC_SEQ, N_CTX // DEC_SEQ, DEC_BATCH, True)
    delta = jnp.concatenate([pc, ps])
    x = _ffn(x, mod[1], norm_pre[1, 2], norm_post[1, 2], w1[1, 1], w3[1, 1], w2[1, 1], 6,
             pending=(delta, norm_post[1, 1], 5))

    return (x[:N_CTX].reshape(BATCH, SEQ, D), x[N_CTX:].reshape(DEC_BATCH, DEC_SEQ, D), new_f, new_b)
```

```python
import functools

import numpy as np
import jax
import jax.numpy as jnp
from jax import lax
from jax.experimental import pallas as pl
from jax.experimental.pallas import tpu as pltpu

D = 1024
HEAD = 64
NHEAD = D // HEAD
D_FF = 2816
DEPTH = 2
N_MOD = 9
POOL_WINDOWS = (2, 4, 8, 16)
POOL_GC = D // 4
GRID_W = 64
RMS_EPS = 1e-6
GN_EPS = 64e-5

BATCH, SEQ = 32, 256
DEC_BATCH, DEC_SEQ = 8, 2048
N_CTX = BATCH * SEQ
N_SMP = DEC_BATCH * DEC_SEQ
NTOK = N_CTX + N_SMP
COND_ROWS = 16

VMEM_LIMIT = 56 * 1024 * 1024

F32 = jnp.float32
BF16 = jnp.bfloat16


def _dot(a, b):
    return jnp.dot(a.astype(BF16), b.astype(BF16), preferred_element_type=F32)


def _rms(x, g):
    return (x * lax.rsqrt(jnp.mean(x * x, axis=-1, keepdims=True) + RMS_EPS)) * g


def _cond_row(start_row):
    return jnp.where(start_row < N_CTX, 0, 1 + (start_row - N_CTX) // DEC_SEQ)


def _split(x):
    hi = x.astype(BF16)
    return hi, (x - hi.astype(F32)).astype(BF16)


def _split_sum(x, ones):
    hi, lo = _split(x)
    return (jnp.dot(hi, ones, preferred_element_type=F32)
            + jnp.dot(lo, ones, preferred_element_type=F32))


def _split_sum_left(ones, x):
    hi, lo = _split(x)
    return (jnp.dot(ones, hi, preferred_element_type=F32)
            + jnp.dot(ones, lo, preferred_element_type=F32))


def _mod_kernel(c_ref, w_ref, b_ref, o_ref):
    c = c_ref[...]
    o_ref[0] = _dot(jax.nn.silu(c), w_ref[0]) + b_ref[0]


def _modulation(cond, w_mod, b_mod):
    return pl.pallas_call(
        _mod_kernel,
        out_shape=jax.ShapeDtypeStruct((DEPTH, COND_ROWS, N_MOD * D), F32),
        grid=(DEPTH, N_MOD),
        in_specs=[
            pl.BlockSpec((COND_ROWS, D), lambda l, n: (0, 0)),
            pl.BlockSpec((1, D, D), lambda l, n: (l, 0, n)),
            pl.BlockSpec((1, 1, D), lambda l, n: (l, 0, n)),
        ],
        out_specs=pl.BlockSpec((1, COND_ROWS, D), lambda l, n: (l, 0, n)),
        compiler_params=pltpu.CompilerParams(
            dimension_semantics=("arbitrary", "arbitrary"), vmem_limit_bytes=VMEM_LIMIT),
        name="modulation",
    )(cond, w_mod, b_mod.reshape(DEPTH, 1, N_MOD * D))


FFN_TM = 512
FFN_TF = 1408


def _ffn_kernel(*refs, k0, pending_k):
    if pending_k is None:
        x_ref, mod_ref, gpre_ref, gpost_ref, w1_ref, w3_ref, w2_ref, o_ref, h_sc, acc_sc, x1_sc = refs
    else:
        (x_ref, mod_ref, gpre_ref, gpost_ref, w1_ref, w3_ref, w2_ref, d_ref, gd_ref,
         o_ref, h_sc, acc_sc, x1_sc) = refs
    f = pl.program_id(1)

    @pl.when(f == 0)
    def _():
        x = x_ref[...]
        if pending_k is not None:
            x = x + mod_ref[0, pending_k:pending_k + 1, :] * _rms(d_ref[...], gd_ref[...])
        x1_sc[...] = x
        h = _rms(x, gpre_ref[...]) * (1.0 + mod_ref[0, k0 + 1:k0 + 2, :]) + mod_ref[0, k0:k0 + 1, :]
        h_sc[...] = h.astype(BF16)
        acc_sc[...] = jnp.zeros_like(acc_sc)

    h = h_sc[...]
    u = jnp.dot(h, w1_ref[...], preferred_element_type=F32)
    g = jnp.dot(h, w3_ref[...], preferred_element_type=F32)
    a = (jax.nn.silu(u) * g).astype(BF16)
    acc_sc[...] += jnp.dot(a, w2_ref[...], preferred_element_type=F32)

    @pl.when(f == pl.num_programs(1) - 1)
    def _():
        o_ref[...] = x1_sc[...] + 0.5 * mod_ref[0, k0 + 2:k0 + 3, :] * _rms(acc_sc[...], gpost_ref[...])


def _ffn(x, mod, gpre, gpost, w1, w3, w2, k0, pending=None):
    tm, tf = FFN_TM, FFN_TF
    row = lambda i, f: (i, 0)
    in_specs = [
        pl.BlockSpec((tm, D), row),
        pl.BlockSpec((1, N_MOD, D), lambda i, f: (_cond_row(i * tm), 0, 0)),
        pl.BlockSpec((1, D), lambda i, f: (0, 0)),
        pl.BlockSpec((1, D), lambda i, f: (0, 0)),
        pl.BlockSpec((D, tf), lambda i, f: (0, f)),
        pl.BlockSpec((D, tf), lambda i, f: (0, f)),
        pl.BlockSpec((tf, D), lambda i, f: (f, 0)),
    ]
    args = [x, mod, gpre.reshape(1, D), gpost.reshape(1, D), w1, w3, w2]
    pending_k = None
    if pending is not None:
        delta, gd, pending_k = pending
        in_specs += [pl.BlockSpec((tm, D), row), pl.BlockSpec((1, D), lambda i, f: (0, 0))]
        args += [delta, gd.reshape(1, D)]
    return pl.pallas_call(
        functools.partial(_ffn_kernel, k0=k0, pending_k=pending_k),
        out_shape=jax.ShapeDtypeStruct((NTOK, D), F32),
        grid=(NTOK // tm, D_FF // tf),
        in_specs=in_specs,
        out_specs=pl.BlockSpec((tm, D), row),
        scratch_shapes=[pltpu.VMEM((tm, D), BF16), pltpu.VMEM((tm, D), F32), pltpu.VMEM((tm, D), F32)],
        compiler_params=pltpu.CompilerParams(
            dimension_semantics=("arbitrary", "arbitrary"), vmem_limit_bytes=VMEM_LIMIT),
        name="ffn",
    )(*args)


RW_TM = 256
HALO = 8


def _rwkv_proj_kernel(x_ref, xp_ref, xn_ref, mod_ref, gpre_ref, mu_ref, wrkv_ref, w1_ref, w2_ref, w0_ref,
                      a1_ref, a2_ref, a0_ref, g1_ref, g2_ref, k_k_ref, k_a_ref, ones_ref,
                      r_out, v_out, kk_out, g_out, lw_out, kd_out, b_out):
    i = pl.program_id(0)
    shift = mod_ref[0, 3:4, :]
    scale = 1.0 + mod_ref[0, 4:5, :]
    gpre = gpre_ref[...]
    premod = lambda x: _rms(x, gpre) * scale + shift

    k = i - N_CTX // RW_TM
    per_seq = DEC_SEQ // RW_TM
    is_ctx = i < N_CTX // RW_TM
    first = jnp.logical_or(is_ctx, k % per_seq == 0)
    last = jnp.logical_or(is_ctx, k % per_seq == per_seq - 1)

    h = premod(x_ref[...])
    hp = jnp.where(first, 0.0, premod(xp_ref[...])[HALO - 1:HALO, :])
    hn = jnp.where(last, 0.0, premod(xn_ref[...])[0:1, :])
    rows = lax.broadcasted_iota(jnp.int32, (RW_TM, 1), 0)
    h_prev = jnp.where(rows == 0, hp, pltpu.roll(h, 1, 0))
    h_next = jnp.where(rows == RW_TM - 1, hn, pltpu.roll(h, RW_TM - 1, 0))
    xx = 0.5 * (h_prev + h_next) - h
    mix = lambda n: h + xx * mu_ref[n:n + 1, :]

    r = _dot(mix(0), wrkv_ref[0])
    kx = _dot(mix(2), wrkv_ref[1])
    v = _dot(mix(3), wrkv_ref[2])
    g = _dot(jax.nn.sigmoid(_dot(mix(5), g1_ref[...])), g2_ref[...])

    z = w0_ref[...] + _dot(jnp.tanh(_dot(mix(1), w1_ref[...])), w2_ref[...])
    wlog = -jax.nn.softplus(-z) - 0.5
    a = jax.nn.sigmoid(a0_ref[...] + _dot(_dot(mix(4), a1_ref[...]), a2_ref[...]))

    kkraw = kx * k_k_ref[...]
    nrm = jnp.sqrt(_split_sum(kkraw * kkraw, ones_ref[...]))
    kk = kkraw / jnp.maximum(nrm, 1e-12)

    r_out[...] = r
    v_out[...] = v
    kk_out[...] = kk
    g_out[...] = g
    lw_out[...] = -jnp.exp(wlog)
    ka = k_a_ref[...]
    for d in range(2):
        a_d = a[:, d * D:(d + 1) * D]
        kd_out[:, d * D:(d + 1) * D] = kx * (1.0 + (a_d - 1.0) * ka)
        b_out[:, d * D:(d + 1) * D] = kk * a_d


def _blockdiag2(m):
    z = jnp.zeros_like(m[0])
    return jnp.concatenate([jnp.concatenate([m[0], z], axis=1), jnp.concatenate([z, m[1]], axis=1)], axis=0)


def _head_ones():
    idx = np.arange(D) // HEAD
    return jnp.asarray((idx[:, None] == idx[None, :]).astype(np.float32), dtype=BF16)


def _rwkv_proj(x, mod, gpre, p):
    tm = RW_TM
    nblk = NTOK // HALO
    row = lambda i: (i, 0)
    full2 = lambda i: (0, 0)
    full3 = lambda i: (0, 0, 0)
    w1c = jnp.concatenate([p['w1'][0], p['w1'][1]], axis=1).astype(BF16)
    a1c = jnp.concatenate([p['a1'][0], p['a1'][1]], axis=1).astype(BF16)
    w2bd = _blockdiag2(p['w2']).astype(BF16)
    a2bd = _blockdiag2(p['a2']).astype(BF16)
    return pl.pallas_call(
        _rwkv_proj_kernel,
        out_shape=[jax.ShapeDtypeStruct((NTOK, D), F32)] * 4 + [jax.ShapeDtypeStruct((NTOK, 2 * D), F32)] * 3,
        grid=(NTOK // tm,),
        in_specs=[
            pl.BlockSpec((tm, D), row),
            pl.BlockSpec((HALO, D), lambda i: (jnp.maximum(i * (tm // HALO) - 1, 0), 0)),
            pl.BlockSpec((HALO, D), lambda i: (jnp.minimum((i + 1) * (tm // HALO), nblk - 1), 0)),
            pl.BlockSpec((1, N_MOD, D), lambda i: (_cond_row(i * tm), 0, 0)),
            pl.BlockSpec((1, D), full2),
            pl.BlockSpec((6, D), full2),
            pl.BlockSpec((3, D, D), full3),
            pl.BlockSpec((D, 128), full2),
            pl.BlockSpec((128, 2 * D), full2),
            pl.BlockSpec((1, 2 * D), full2),
            pl.BlockSpec((D, 128), full2),
            pl.BlockSpec((128, 2 * D), full2),
            pl.BlockSpec((1, 2 * D), full2),
            pl.BlockSpec((D, 128), full2),
            pl.BlockSpec((128, D), full2),
            pl.BlockSpec((1, D), full2),
            pl.BlockSpec((1, D), full2),
            pl.BlockSpec((D, D), full2),
        ],
        out_specs=[pl.BlockSpec((tm, D), row)] * 4 + [pl.BlockSpec((tm, 2 * D), row)] * 3,
        compiler_params=pltpu.CompilerParams(
            dimension_semantics=("arbitrary",), vmem_limit_bytes=VMEM_LIMIT),
        name="rwkv_proj",
    )(x, x, x, mod, gpre.reshape(1, D), p['mu'], p['w_rkv'].astype(BF16), w1c, w2bd,
      p['w0'].reshape(1, 2 * D), a1c, a2bd, p['a0'].reshape(1, 2 * D),
      p['g1'].astype(BF16), p['g2'].astype(BF16), p['k_k'].reshape(1, D), p['k_a'].reshape(1, D),
      _head_ones())


CH = 32
SUB = 2
GH = 4
GW = GH * HEAD
NG = NHEAD // GH
SR = GH * CH

_NT = (((1,), (1,)), ((), ()))
_TN = (((0,), (0,)), ((), ()))


def _scan_masks():
    rho = np.arange(SR)
    same = (rho[:, None] // CH) == (rho[None, :] // CH)
    lower = rho[None, :] < rho[:, None]
    strict = np.stack([same & lower, same & lower.T])
    incl = strict | np.eye(SR, dtype=bool)[None]
    t = np.arange(CH)
    tri = np.stack([t[None, :] <= t[:, None], t[None, :] >= t[:, None]])
    return (jnp.asarray(strict.astype(np.float32)), jnp.asarray(incl.astype(np.float32)),
            jnp.asarray(tri.astype(np.float32), dtype=BF16))


def _wkv_kernel(rf_ref, vf_ref, kkf_ref, lwf_ref, kdf_ref, bf_ref, rb_ref, vb_ref, kkb_ref, lwb_ref, kdb_ref,
                bb_ref, strict_ref, incl_ref, tri_ref, s0_ref, yf_ref, yb_ref, s_ref,
                *, n_ctx_chunks, ctx_chunks_per_seq, smp_chunks_per_seq):
    q = pl.program_id(0)
    is_ctx = q < n_ctx_chunks
    c = jnp.where(is_ctx, q % ctx_chunks_per_seq, (q - n_ctx_chunks) % smp_chunks_per_seq)

    @pl.when(c == 0)
    def _():
        s_ref[...] = jnp.where(is_ctx, 0.0, s0_ref[...])

    def decayed(d, a, r_ref, v_ref, kk_ref, lw_ref, kd_ref, b_ref):
        pos = a if d == 0 else SUB - 1 - a
        rows = slice(pos * CH, (pos + 1) * CH)
        lw = lw_ref[rows, :]
        cum = _split_sum_left(tri_ref[d], lw)
        tot = cum[CH - 1:CH, :] if d == 0 else cum[0:1, :]
        e_neg = jnp.exp(-cum)
        e_rem = jnp.exp(tot - cum)
        kd = kd_ref[rows, :]
        bb = b_ref[rows, :]
        return (kk_ref[rows, :] * jnp.exp(cum - lw), r_ref[rows, :] * jnp.exp(cum), kd * e_neg, bb * e_neg,
                v_ref[rows, :], kd * e_rem, bb * e_rem), jnp.exp(tot), rows

    refs_d = ((rf_ref, vf_ref, kkf_ref, lwf_ref, kdf_ref, bf_ref),
              (rb_ref, vb_ref, kkb_ref, lwb_ref, kdb_ref, bb_ref))
    tiles = {(a, d): decayed(d, a, *refs_d[d]) for a in range(SUB) for d in range(2)}
    y_refs = (yf_ref, yb_ref)
    strict_d = [strict_ref[d] > 0.5 for d in range(2)]
    incl_d = [incl_ref[d] > 0.5 for d in range(2)]

    lane_head = lax.broadcasted_iota(jnp.int32, (CH, GW), 1) // HEAD
    head_lanes = [lane_head == h for h in range(GH)]

    def stack(x):
        return jnp.concatenate([jnp.where(m, x, 0.0) for m in head_lanes], axis=0).astype(BF16)

    eye = (lax.broadcasted_iota(jnp.int32, (SR, SR), 0)
           == lax.broadcasted_iota(jnp.int32, (SR, SR), 1)).astype(F32)
    dot = functools.partial(jnp.dot, preferred_element_type=F32)
    dotg = functools.partial(lax.dot_general, preferred_element_type=F32)

    chains = [(a, d, g) for a in range(SUB) for g in range(NG) for d in range(2)]
    ids = range(len(chains))
    sls = [slice(g * GW, (g + 1) * GW) for _, _, g in chains]
    ops = [[stack(x[:, sls[k]]) for x in tiles[(a, d)][0]] for k, (a, d, _) in enumerate(chains)]
    qs, rs, ks, bs, vs, khs, bhs = zip(*ops)
    gm = [dotg(jnp.concatenate([qs[k], rs[k]], axis=0), jnp.concatenate([bs[k], ks[k]], axis=0), _NT)
          for k in ids]
    strict = [strict_d[d] for _, d, _ in chains]
    incl = [incl_d[d] for _, d, _ in chains]
    l = [jnp.where(strict[k], gm[k][:SR, :SR], 0.0) for k in ids]
    m = [jnp.where(strict[k], gm[k][:SR, SR:], 0.0).astype(BF16) for k in ids]
    ppb = [jnp.concatenate([jnp.where(incl[k], gm[k][SR:, SR:], 0.0), jnp.where(incl[k], gm[k][SR:, :SR], 0.0)],
                           axis=1).astype(BF16) for k in ids]
    khbh = [jnp.concatenate([khs[k], bhs[k]], axis=0) for k in ids]
    n = [eye - l[k] for k in ids]
    pw = [l[k].astype(BF16) for k in ids]
    mv = [dot(m[k], vs[k]).astype(BF16) for k in ids]
    for _ in range(CH.bit_length() - 2):
        pw = [dot(pw[k], pw[k]).astype(BF16) for k in ids]
        n = [n[k] + dot(n[k].astype(BF16), pw[k]) for k in ids]
    x = [dot(n[k].astype(BF16), jnp.concatenate([qs[k], mv[k]], axis=1)) for k in ids]
    x1r = [jnp.concatenate([x[k][:, :GW].astype(BF16), rs[k]], axis=0) for k in ids]

    state = {(d, g): s_ref[d, 0, g] for d in range(2) for g in range(NG)}
    for a in range(SUB):
        ka = [k for k in ids if chains[k][0] == a]
        s0 = {k: state[chains[k][1:]] for k in ka}
        zz = {k: dotg(x1r[k], s0[k].astype(BF16), _NT) for k in ka}
        vu = {k: jnp.concatenate([vs[k], (-(zz[k][:SR] + x[k][:, GW:])).astype(BF16)], axis=0) for k in ka}
        ys = {k: zz[k][SR:] + dot(ppb[k], vu[k]) for k in ka}
        for k in ka:
            _, d, g = chains[k]
            state[(d, g)] = s0[k] * tiles[(a, d)][1][:, sls[k]] + dotg(vu[k], khbh[k], _TN)
        for k in ka:
            _, d, g = chains[k]
            y = ys[k][0:CH]
            for h in range(1, GH):
                y = y + ys[k][h * CH:(h + 1) * CH]
            y_refs[d][tiles[(a, d)][2], sls[k]] = y
    for (d, g), s in state.items():
        s_ref[d, 0, g] = s


def _wkv(r, v, kk, lw, kd, b, state_f, state_b, n_ctx_seq=BATCH, t_ctx=SEQ, n_smp_seq=DEC_BATCH, t_smp=DEC_SEQ):
    ntok = n_ctx_seq * t_ctx + n_smp_seq * t_smp
    blk = SUB * CH
    cps_c, cps_s = t_ctx // blk, t_smp // blk
    ncc = n_ctx_seq * cps_c
    nq = ncc + n_smp_seq * cps_s

    def seq_of(q):
        return jnp.where(q < ncc, q // cps_c, n_ctx_seq + (q - ncc) // cps_s)

    def row_block(d, q):
        ctx = q < ncc
        cps = jnp.where(ctx, cps_c, cps_s)
        c = jnp.where(ctx, q % cps_c, (q - ncc) % cps_s)
        return q - c + (c if d == 0 else cps - 1 - c)

    tok = [pl.BlockSpec((blk, D), functools.partial(lambda q, d: (row_block(d, q), 0), d=d)) for d in range(2)]
    tokd = [pl.BlockSpec((blk, D), functools.partial(lambda q, d: (row_block(d, q), d), d=d)) for d in range(2)]
    msk = pl.BlockSpec((2, SR, SR), lambda q: (0, 0, 0))
    tri = pl.BlockSpec((2, CH, CH), lambda q: (0, 0, 0))
    s0spec = pl.BlockSpec((2, 1, NG, GW, GW), lambda q: (0, jnp.maximum(seq_of(q) - n_ctx_seq, 0), 0, 0, 0))
    sspec = pl.BlockSpec((2, 1, NG, GW, GW), lambda q: (0, seq_of(q), 0, 0, 0))

    s0 = jnp.stack([state_f, state_b]).reshape(2, n_smp_seq, NG, GH, HEAD, HEAD)
    eye = jnp.eye(GH, dtype=F32)
    s0 = (s0[:, :, :, :, :, None, :] * eye[None, None, None, :, None, :, None]).reshape(2, n_smp_seq, NG, GW, GW)

    strict, incl, trim = _scan_masks()
    y_f, y_b, s_fin = pl.pallas_call(
        functools.partial(_wkv_kernel, n_ctx_chunks=ncc, ctx_chunks_per_seq=cps_c, smp_chunks_per_seq=cps_s),
        out_shape=[jax.ShapeDtypeStruct((ntok, D), F32), jax.ShapeDtypeStruct((ntok, D), F32),
                   jax.ShapeDtypeStruct((2, n_ctx_seq + n_smp_seq, NG, GW, GW), F32)],
        grid=(nq,),
        in_specs=[tok[0], tok[0], tok[0], tokd[0], tokd[0], tokd[0],
                  tok[1], tok[1], tok[1], tokd[1], tokd[1], tokd[1], msk, msk, tri, s0spec],
        out_specs=[tok[0], tok[1], sspec],
        compiler_params=pltpu.CompilerParams(
            dimension_semantics=("arbitrary",), vmem_limit_bytes=VMEM_LIMIT),
        name="wkv_scan",
    )(r, v, kk, lw, kd, b, r, v, kk, lw, kd, b, strict, incl, trim, s0)
    s_fin = s_fin[:, :n_ctx_seq].reshape(2, n_ctx_seq, NG, GH, HEAD, GH, HEAD)
    s_fin = jnp.stack([s_fin[:, :, :, h, :, h, :] for h in range(GH)], axis=3)
    s_fin = s_fin.reshape(2, n_ctx_seq, 1, NHEAD, HEAD, HEAD)
    return y_f, y_b, s_fin[0], s_fin[1]


RO_TM = 512


def _rwkv_out_kernel(x_ref, yf_ref, yb_ref, r_ref, v_ref, kd_ref, g_ref, mod_ref, gnw_ref, gnb_ref,
                     rk_ref, wo_ref, gpost_ref, ones_ref, o_ref):
    ones = ones_ref[...]
    ys = yf_ref[...] + yb_ref[...]
    mean = _split_sum(ys, ones) * (1.0 / HEAD)
    c = ys - mean
    var = _split_sum(c * c, ones) * (1.0 / HEAD)
    yn = c * lax.rsqrt(var + GN_EPS) * gnw_ref[...] + gnb_ref[...]
    kd = kd_ref[...]
    bonus = _split_sum(r_ref[...] * (kd[:, :D] + kd[:, D:]) * rk_ref[...], ones) * v_ref[...]
    out = _dot((yn + bonus) * g_ref[...], wo_ref[...])
    o_ref[...] = x_ref[...] + mod_ref[0, 5:6, :] * _rms(out, gpost_ref[...])


def _rwkv_out(x, y_f, y_b, r, v, kd, g, mod, p, gpost):
    tm = RO_TM
    row = lambda i: (i, 0)
    full2 = lambda i: (0, 0)
    vec = pl.BlockSpec((1, D), full2)
    tok = pl.BlockSpec((tm, D), row)
    tok2 = pl.BlockSpec((tm, 2 * D), row)
    return pl.pallas_call(
        _rwkv_out_kernel,
        out_shape=jax.ShapeDtypeStruct((NTOK, D), F32),
        grid=(NTOK // tm,),
        in_specs=[tok, tok, tok, tok, tok, tok2, tok,
                  pl.BlockSpec((1, N_MOD, D), lambda i: (_cond_row(i * tm), 0, 0)),
                  vec, vec, vec, pl.BlockSpec((D, D), full2), vec, pl.BlockSpec((D, D), full2)],
        out_specs=tok,
        compiler_params=pltpu.CompilerParams(
            dimension_semantics=("arbitrary",), vmem_limit_bytes=VMEM_LIMIT),
        name="rwkv_out",
    )(x, y_f, y_b, r, v, kd, g, mod, p['gn_w'].reshape(1, D), p['gn_b'].reshape(1, D),
      p['r_k'].reshape(1, D), p['w_o'].astype(BF16), gpost.reshape(1, D), _head_ones())


def _box_matrix(length, win):
    t = np.arange(length)
    lo = np.clip(t - win // 2, 0, length)
    hi = np.clip(t + win - win // 2, 0, length)
    a = ((t[None, :] >= lo[:, None]) & (t[None, :] < hi[:, None])).astype(np.float32)
    return a, (hi - lo).astype(np.float32)


def _pool_constants(grid):
    mats, inv = [], []
    for win in POOL_WINDOWS:
        if grid:
            ar, cr = _box_matrix(DEC_SEQ // GRID_W, win)
            ac, cc = _box_matrix(GRID_W, win)
            mats.append(np.kron(ar, ac))
            inv.append(1.0 / np.kron(cr, cc))
        else:
            a, c = _box_matrix(SEQ, win)
            mats.append(a)
            inv.append(1.0 / c)
    mats = jnp.asarray(np.stack(mats), dtype=BF16)
    inv = jnp.asarray(np.stack(inv)[:, :, None], dtype=F32)
    return mats, inv


def _pool_kernel(xf_ref, xg_ref, mod_ref, gpre_ref, a_ref, inv_ref, pw_ref, ps_ref, o_ref):
    xf = xf_ref[...]
    rs = lax.rsqrt(jnp.mean(xf * xf, axis=-1, keepdims=True) + RMS_EPS)
    part = (xg_ref[...] * rs) * gpre_ref[...] * (1.0 + mod_ref[0, 4:5, :]) + mod_ref[0, 3:4, :]
    m = _split_sum_left(a_ref[0], part) * inv_ref[0]
    o_ref[...] = _dot(m - part, pw_ref[0]) * ps_ref[...]


def _pool(x, mod, gpre, pool_w, pool_scale, t, seq0, nseq, grid):
    mats, inv = _pool_constants(grid)
    gc = POOL_GC
    return pl.pallas_call(
        _pool_kernel,
        out_shape=jax.ShapeDtypeStruct((nseq * t, D), F32),
        grid=(4, nseq),
        in_specs=[
            pl.BlockSpec((t, D), lambda g, s: (seq0 + s, 0)),
            pl.BlockSpec((t, gc), lambda g, s: (seq0 + s, g)),
            pl.BlockSpec((1, N_MOD, gc), lambda g, s: (_cond_row((seq0 + s) * t), 0, g)),
            pl.BlockSpec((1, gc), lambda g, s: (0, g)),
            pl.BlockSpec((1, t, t), lambda g, s: (g, 0, 0)),
            pl.BlockSpec((1, t, 1), lambda g, s: (g, 0, 0)),
            pl.BlockSpec((1, gc, gc), lambda g, s: (g, 0, 0)),
            pl.BlockSpec((1, gc), lambda g, s: (0, g)),
        ],
        out_specs=pl.BlockSpec((t, gc), lambda g, s: (s, g)),
        compiler_params=pltpu.CompilerParams(
            dimension_semantics=("arbitrary", "arbitrary"), vmem_limit_bytes=VMEM_LIMIT),
        name="pool_grid" if grid else "pool_seq",
    )(x, x, mod, gpre.reshape(1, D), mats, inv, pool_w.astype(BF16), pool_scale.reshape(1, D))


def kernel(x_prompt, x_sample, c, state_ctx_fwd, state_ctx_bwd, c_ctx, w_mod, b_mod, norm_pre, norm_post,
           ffn_w1, ffn_w3, ffn_w2, rwkv_mu, rwkv_w_rkv, rwkv_w0, rwkv_w1, rwkv_w2, rwkv_a0, rwkv_a1,
           rwkv_a2, rwkv_g1, rwkv_g2, rwkv_k_k, rwkv_k_a, rwkv_r_k, rwkv_gn_w, rwkv_gn_b, rwkv_w_o,
           pool_w, pool_scale):
    x = jnp.concatenate([x_prompt.reshape(N_CTX, D), x_sample.reshape(N_SMP, D)])
    cond = jnp.concatenate([c_ctx[None], c, jnp.zeros((COND_ROWS - 1 - DEC_BATCH, D), F32)])
    mod = _modulation(cond, w_mod, b_mod).reshape(DEPTH, COND_ROWS, N_MOD, D)
    w1, w3, w2 = ffn_w1.astype(BF16), ffn_w3.astype(BF16), ffn_w2.astype(BF16)

    x = _ffn(x, mod[0], norm_pre[0, 0], norm_post[0, 0], w1[0, 0], w3[0, 0], w2[0, 0], 0)
    p = dict(mu=rwkv_mu[0], w_rkv=rwkv_w_rkv[0], w0=rwkv_w0[0], w1=rwkv_w1[0], w2=rwkv_w2[0],
             a0=rwkv_a0[0], a1=rwkv_a1[0], a2=rwkv_a2[0], g1=rwkv_g1[0], g2=rwkv_g2[0],
             k_k=rwkv_k_k[0], k_a=rwkv_k_a[0], r_k=rwkv_r_k[0], gn_w=rwkv_gn_w[0], gn_b=rwkv_gn_b[0],
             w_o=rwkv_w_o[0])
    r, v, kk, g, lw, kd, b = _rwkv_proj(x, mod[0], norm_pre[0, 1], p)
    y_f, y_b, new_f, new_b = _wkv(r, v, kk, lw, kd, b, state_ctx_fwd[:, 0], state_ctx_bwd[:, 0])
    x = _rwkv_out(x, y_f, y_b, r, v, kd, g, mod[0], p, norm_post[0, 1])
    x = _ffn(x, mod[0], norm_pre[0, 2], norm_post[0, 2], w1[0, 1], w3[0, 1], w2[0, 1], 6)

    x = _ffn(x, mod[1], norm_pre[1, 0], norm_post[1, 0], w1[1, 0], w3[1, 0], w2[1, 0], 0)
    pc = _pool(x, mod[1], norm_pre[1, 1], pool_w[0], pool_scale[0], SEQ, 0, BATCH, False)
    ps = _pool(x, mod[1], norm_pre[1, 1], pool_w[0], pool_scale[0], DEC_SEQ, N_CTX // DEC_SEQ, DEC_BATCH, True)
    delta = jnp.concatenate([pc, ps])
    x = _ffn(x, mod[1], norm_pre[1, 2], norm_post[1, 2], w1[1, 1], w3[1, 1], w2[1, 1], 6,
             pending=(delta, norm_post[1, 1], 5))

    return (x[:N_CTX].reshape(BATCH, SEQ, D), x[N_CTX:].reshape(DEC_BATCH, DEC_SEQ, D), new_f, new_b)
```

```python
import functools

import numpy as np
import jax
import jax.numpy as jnp
from jax import lax
from jax.experimental import pallas as pl
from jax.experimental.pallas import tpu as pltpu

D = 1024
HEAD = 64
NHEAD = D // HEAD
D_FF = 2816
DEPTH = 2
N_MOD = 9
POOL_WINDOWS = (2, 4, 8, 16)
POOL_GC = D // 4
GRID_W = 64
RMS_EPS = 1e-6
GN_EPS = 64e-5

BATCH, SEQ = 32, 256
DEC_BATCH, DEC_SEQ = 8, 2048
N_CTX = BATCH * SEQ
N_SMP = DEC_BATCH * DEC_SEQ
NTOK = N_CTX + N_SMP
COND_ROWS = 16

VMEM_LIMIT = 56 * 1024 * 1024

F32 = jnp.float32
BF16 = jnp.bfloat16


def _dot(a, b):
    return jnp.dot(a.astype(BF16), b.astype(BF16), preferred_element_type=F32)


def _rms(x, g):
    return (x * lax.rsqrt(jnp.mean(x * x, axis=-1, keepdims=True) + RMS_EPS)) * g


def _cond_row(start_row):
    return jnp.where(start_row < N_CTX, 0, 1 + (start_row - N_CTX) // DEC_SEQ)


def _split(x):
    hi = x.astype(BF16)
    return hi, (x - hi.astype(F32)).astype(BF16)


def _split_sum(x, ones):
    hi, lo = _split(x)
    return (jnp.dot(hi, ones, preferred_element_type=F32)
            + jnp.dot(lo, ones, preferred_element_type=F32))


HSUM_W = 256


def _head_sum(x, ones):
    return jnp.concatenate([_split_sum(x[:, i:i + HSUM_W], ones) for i in range(0, D, HSUM_W)], axis=1)


def _split_sum_left(ones, x):
    hi, lo = _split(x)
    return (jnp.dot(ones, hi, preferred_element_type=F32)
            + jnp.dot(ones, lo, preferred_element_type=F32))


def _mod_kernel(c_ref, w_ref, b_ref, o_ref):
    c = c_ref[...]
    o_ref[0] = _dot(jax.nn.silu(c), w_ref[0]) + b_ref[0]


def _modulation(cond, w_mod, b_mod):
    return pl.pallas_call(
        _mod_kernel,
        out_shape=jax.ShapeDtypeStruct((DEPTH, COND_ROWS, N_MOD * D), F32),
        grid=(DEPTH, N_MOD),
        in_specs=[
            pl.BlockSpec((COND_ROWS, D), lambda l, n: (0, 0)),
            pl.BlockSpec((1, D, D), lambda l, n: (l, 0, n)),
            pl.BlockSpec((1, 1, D), lambda l, n: (l, 0, n)),
        ],
        out_specs=pl.BlockSpec((1, COND_ROWS, D), lambda l, n: (l, 0, n)),
        compiler_params=pltpu.CompilerParams(
            dimension_semantics=("arbitrary", "arbitrary"), vmem_limit_bytes=VMEM_LIMIT),
        name="modulation",
    )(cond, w_mod, b_mod.reshape(DEPTH, 1, N_MOD * D))


FFN_TM = 512
FFN_TF = 1408


def _tok_operand(x, tm):
    if not isinstance(x, tuple):
        return [x], [pl.BlockSpec((tm, D), lambda i, f: (i, 0))]
    na = N_CTX // tm
    return list(x), [pl.BlockSpec((tm, D), lambda i, f: (jnp.minimum(i, na - 1), 0)),
                     pl.BlockSpec((tm, D), lambda i, f: (jnp.maximum(i - na, 0), 0))]


def _ffn_kernel(*refs, k0, pending_k, n_x, n_d, n_o):
    refs = list(refs)
    take = lambda n: [refs.pop(0) for _ in range(n)]
    x_refs = take(n_x)
    mod_ref, gpre_ref, gpost_ref, w1_ref, w3_ref, w2_ref = take(6)
    d_refs = take(n_d)
    gd_ref = take(1)[0] if n_d else None
    o_refs = take(n_o)
    h_sc, acc_sc, x1_sc = refs
    f = pl.program_id(1)
    is_ctx = pl.program_id(0) < N_CTX // FFN_TM

    def read(rs):
        return rs[0][...] if len(rs) == 1 else jnp.where(is_ctx, rs[0][...], rs[1][...])

    @pl.when(f == 0)
    def _():
        x = read(x_refs)
        if pending_k is not None:
            x = x + mod_ref[0, pending_k:pending_k + 1, :] * _rms(read(d_refs), gd_ref[...])
        x1_sc[...] = x
        h = _rms(x, gpre_ref[...]) * (1.0 + mod_ref[0, k0 + 1:k0 + 2, :]) + mod_ref[0, k0:k0 + 1, :]
        h_sc[...] = h.astype(BF16)
        acc_sc[...] = jnp.zeros_like(acc_sc)

    h = h_sc[...]
    u = jnp.dot(h, w1_ref[...], preferred_element_type=F32)
    g = jnp.dot(h, w3_ref[...], preferred_element_type=F32)
    a = (jax.nn.silu(u) * g).astype(BF16)
    acc_sc[...] += jnp.dot(a, w2_ref[...], preferred_element_type=F32)

    last = f == pl.num_programs(1) - 1
    result = lambda: x1_sc[...] + 0.5 * mod_ref[0, k0 + 2:k0 + 3, :] * _rms(acc_sc[...], gpost_ref[...])
    if n_o == 1:
        @pl.when(last)
        def _():
            o_refs[0][...] = result()
    else:
        @pl.when(jnp.logical_and(last, is_ctx))
        def _():
            o_refs[0][...] = result()

        @pl.when(jnp.logical_and(last, jnp.logical_not(is_ctx)))
        def _():
            o_refs[1][...] = result()


def _ffn(x, mod, gpre, gpost, w1, w3, w2, k0, pending=None, split_out=False):
    tm, tf = FFN_TM, FFN_TF
    vec = pl.BlockSpec((1, D), lambda i, f: (0, 0))
    x_args, x_specs = _tok_operand(x, tm)
    args = x_args + [mod, gpre.reshape(1, D), gpost.reshape(1, D), w1, w3, w2]
    in_specs = x_specs + [
        pl.BlockSpec((1, N_MOD, D), lambda i, f: (_cond_row(i * tm), 0, 0)), vec, vec,
        pl.BlockSpec((D, tf), lambda i, f: (0, f)),
        pl.BlockSpec((D, tf), lambda i, f: (0, f)),
        pl.BlockSpec((tf, D), lambda i, f: (f, 0)),
    ]
    pending_k, n_d = None, 0
    if pending is not None:
        delta, gd, pending_k = pending
        d_args, d_specs = _tok_operand(delta, tm)
        n_d = len(d_args)
        args += d_args + [gd.reshape(1, D)]
        in_specs += d_specs + [vec]
    if split_out:
        out_shape = [jax.ShapeDtypeStruct((N_CTX, D), F32), jax.ShapeDtypeStruct((N_SMP, D), F32)]
        out_specs = _tok_operand((None, None), tm)[1]
    else:
        out_shape = jax.ShapeDtypeStruct((NTOK, D), F32)
        out_specs = _tok_operand(None, tm)[1][0]
    return pl.pallas_call(
        functools.partial(_ffn_kernel, k0=k0, pending_k=pending_k, n_x=len(x_args), n_d=n_d,
                          n_o=2 if split_out else 1),
        out_shape=out_shape,
        grid=(NTOK // tm, D_FF // tf),
        in_specs=in_specs,
        out_specs=out_specs,
        scratch_shapes=[pltpu.VMEM((tm, D), BF16), pltpu.VMEM((tm, D), F32), pltpu.VMEM((tm, D), F32)],
        compiler_params=pltpu.CompilerParams(
            dimension_semantics=("arbitrary", "arbitrary"), vmem_limit_bytes=VMEM_LIMIT),
        name="ffn",
    )(*args)


RW_TM = 256
HALO = 8


def _rwkv_proj_kernel(x_ref, xp_ref, xn_ref, mod_ref, gpre_ref, mu_ref, wrkv_ref, w1_ref, w2_ref, w0_ref,
                      a1_ref, a2_ref, a0_ref, g1_ref, g2_ref, k_k_ref, k_a_ref, ones_ref,
                      r_out, v_out, kk_out, g_out, lw_out, kd_out, b_out):
    i = pl.program_id(0)
    shift = mod_ref[0, 3:4, :]
    scale = 1.0 + mod_ref[0, 4:5, :]
    gpre = gpre_ref[...]
    premod = lambda x: _rms(x, gpre) * scale + shift

    k = i - N_CTX // RW_TM
    per_seq = DEC_SEQ // RW_TM
    is_ctx = i < N_CTX // RW_TM
    first = jnp.logical_or(is_ctx, k % per_seq == 0)
    last = jnp.logical_or(is_ctx, k % per_seq == per_seq - 1)

    h = premod(x_ref[...])
    hp = jnp.where(first, 0.0, premod(xp_ref[...])[HALO - 1:HALO, :])
    hn = jnp.where(last, 0.0, premod(xn_ref[...])[0:1, :])
    rows = lax.broadcasted_iota(jnp.int32, (RW_TM, 1), 0)
    h_prev = jnp.where(rows == 0, hp, pltpu.roll(h, 1, 0))
    h_next = jnp.where(rows == RW_TM - 1, hn, pltpu.roll(h, RW_TM - 1, 0))
    xx = 0.5 * (h_prev + h_next) - h
    mix = lambda n: h + xx * mu_ref[n:n + 1, :]

    r = _dot(mix(0), wrkv_ref[0])
    kx = _dot(mix(2), wrkv_ref[1])
    v = _dot(mix(3), wrkv_ref[2])
    g = _dot(jax.nn.sigmoid(_dot(mix(5), g1_ref[...])), g2_ref[...])

    z = w0_ref[...] + _dot(jnp.tanh(_dot(mix(1), w1_ref[...])), w2_ref[...])
    wlog = -jax.nn.softplus(-z) - 0.5
    a = jax.nn.sigmoid(a0_ref[...] + _dot(_dot(mix(4), a1_ref[...]), a2_ref[...]))

    kkraw = kx * k_k_ref[...]
    nrm = jnp.sqrt(_head_sum(kkraw * kkraw, ones_ref[...]))
    kk = kkraw / jnp.maximum(nrm, 1e-12)

    r_out[...] = r
    v_out[...] = v
    kk_out[...] = kk
    g_out[...] = g
    lw_out[...] = -jnp.exp(wlog)
    ka = k_a_ref[...]
    for d in range(2):
        a_d = a[:, d * D:(d + 1) * D]
        kd_out[:, d * D:(d + 1) * D] = kx * (1.0 + (a_d - 1.0) * ka)
        b_out[:, d * D:(d + 1) * D] = kk * a_d


def _blockdiag2(m):
    z = jnp.zeros_like(m[0])
    return jnp.concatenate([jnp.concatenate([m[0], z], axis=1), jnp.concatenate([z, m[1]], axis=1)], axis=0)


def _head_ones():
    idx = np.arange(HSUM_W) // HEAD
    return jnp.asarray((idx[:, None] == idx[None, :]).astype(np.float32), dtype=BF16)


def _rwkv_proj(x, mod, gpre, p):
    tm = RW_TM
    nblk = NTOK // HALO
    row = lambda i: (i, 0)
    full2 = lambda i: (0, 0)
    full3 = lambda i: (0, 0, 0)
    w1c = jnp.concatenate([p['w1'][0], p['w1'][1]], axis=1).astype(BF16)
    a1c = jnp.concatenate([p['a1'][0], p['a1'][1]], axis=1).astype(BF16)
    w2bd = _blockdiag2(p['w2']).astype(BF16)
    a2bd = _blockdiag2(p['a2']).astype(BF16)
    return pl.pallas_call(
        _rwkv_proj_kernel,
        out_shape=[jax.ShapeDtypeStruct((NTOK, D), F32)] * 4 + [jax.ShapeDtypeStruct((NTOK, 2 * D), F32)] * 3,
        grid=(NTOK // tm,),
        in_specs=[
            pl.BlockSpec((tm, D), row),
            pl.BlockSpec((HALO, D), lambda i: (jnp.maximum(i * (tm // HALO) - 1, 0), 0)),
            pl.BlockSpec((HALO, D), lambda i: (jnp.minimum((i + 1) * (tm // HALO), nblk - 1), 0)),
            pl.BlockSpec((1, N_MOD, D), lambda i: (_cond_row(i * tm), 0, 0)),
            pl.BlockSpec((1, D), full2),
            pl.BlockSpec((6, D), full2),
            pl.BlockSpec((3, D, D), full3),
            pl.BlockSpec((D, 128), full2),
            pl.BlockSpec((128, 2 * D), full2),
            pl.BlockSpec((1, 2 * D), full2),
            pl.BlockSpec((D, 128), full2),
            pl.BlockSpec((128, 2 * D), full2),
            pl.BlockSpec((1, 2 * D), full2),
            pl.BlockSpec((D, 128), full2),
            pl.BlockSpec((128, D), full2),
            pl.BlockSpec((1, D), full2),
            pl.BlockSpec((1, D), full2),
            pl.BlockSpec((HSUM_W, HSUM_W), full2),
        ],
        out_specs=[pl.BlockSpec((tm, D), row)] * 4 + [pl.BlockSpec((tm, 2 * D), row)] * 3,
        compiler_params=pltpu.CompilerParams(
            dimension_semantics=("arbitrary",), vmem_limit_bytes=VMEM_LIMIT),
        name="rwkv_proj",
    )(x, x, x, mod, gpre.reshape(1, D), p['mu'], p['w_rkv'].astype(BF16), w1c, w2bd,
      p['w0'].reshape(1, 2 * D), a1c, a2bd, p['a0'].reshape(1, 2 * D),
      p['g1'].astype(BF16), p['g2'].astype(BF16), p['k_k'].reshape(1, D), p['k_a'].reshape(1, D),
      _head_ones())


CH = 32
SUB = 4
GH = 4
GW = GH * HEAD
NG = NHEAD // GH
SR = GH * CH

_NT = (((1,), (1,)), ((), ()))
_TN = (((0,), (0,)), ((), ()))


def _scan_masks():
    rho = np.arange(SR)
    same = (rho[:, None] // CH) == (rho[None, :] // CH)
    lower = rho[None, :] < rho[:, None]
    strict = np.stack([same & lower, same & lower.T])
    incl = strict | np.eye(SR, dtype=bool)[None]
    t = np.arange(CH)
    tri = np.stack([t[None, :] <= t[:, None], t[None, :] >= t[:, None]])
    return (jnp.asarray(strict.astype(np.float32)), jnp.asarray(incl.astype(np.float32)),
            jnp.asarray(tri.astype(np.float32), dtype=BF16))


def _wkv_kernel(rf_ref, vf_ref, kkf_ref, lwf_ref, kdf_ref, bf_ref, rb_ref, vb_ref, kkb_ref, lwb_ref, kdb_ref,
                bb_ref, strict_ref, incl_ref, tri_ref, s0_ref, yf_ref, yb_ref, sfin_ref, s_ref,
                *, n_ctx_chunks, ctx_chunks_per_seq, smp_chunks_per_seq):
    q = pl.program_id(0)
    is_ctx = q < n_ctx_chunks
    c = jnp.where(is_ctx, q % ctx_chunks_per_seq, (q - n_ctx_chunks) % smp_chunks_per_seq)

    @pl.when(c == 0)
    def _():
        s_ref[...] = jnp.where(is_ctx, 0.0, s0_ref[:, 0])

    def decayed(d, a, r_ref, v_ref, kk_ref, lw_ref, kd_ref, b_ref):
        pos = a if d == 0 else SUB - 1 - a
        rows = slice(pos * CH, (pos + 1) * CH)
        lw = lw_ref[rows, :]
        cum = _split_sum_left(tri_ref[d], lw)
        tot = cum[CH - 1:CH, :] if d == 0 else cum[0:1, :]
        e_neg = jnp.exp(-cum)
        e_rem = jnp.exp(tot - cum)
        kd = kd_ref[rows, :]
        bb = b_ref[rows, :]
        return (kk_ref[rows, :] * jnp.exp(cum - lw), r_ref[rows, :] * jnp.exp(cum), kd * e_neg, bb * e_neg,
                v_ref[rows, :], kd * e_rem, bb * e_rem), jnp.exp(tot), rows

    refs_d = ((rf_ref, vf_ref, kkf_ref, lwf_ref, kdf_ref, bf_ref),
              (rb_ref, vb_ref, kkb_ref, lwb_ref, kdb_ref, bb_ref))
    tiles = {(a, d): decayed(d, a, *refs_d[d]) for a in range(SUB) for d in range(2)}
    y_refs = (yf_ref, yb_ref)
    strict_d = [strict_ref[d] > 0.5 for d in range(2)]
    incl_d = [incl_ref[d] > 0.5 for d in range(2)]

    lane_head = lax.broadcasted_iota(jnp.int32, (CH, GW), 1) // HEAD
    head_lanes = [lane_head == h for h in range(GH)]

    def stack(x):
        return jnp.concatenate([jnp.where(m, x, 0.0) for m in head_lanes], axis=0).astype(BF16)

    eye = (lax.broadcasted_iota(jnp.int32, (SR, SR), 0)
           == lax.broadcasted_iota(jnp.int32, (SR, SR), 1)).astype(F32)
    dot = functools.partial(jnp.dot, preferred_element_type=F32)
    dotg = functools.partial(lax.dot_general, preferred_element_type=F32)

    chains = [(a, d, g) for a in range(SUB) for g in range(NG) for d in range(2)]
    ids = range(len(chains))
    sls = [slice(g * GW, (g + 1) * GW) for _, _, g in chains]
    ops = [[stack(x[:, sls[k]]) for x in tiles[(a, d)][0]] for k, (a, d, _) in enumerate(chains)]
    qs, rs, ks, bs, vs, khs, bhs = zip(*ops)
    gm = [dotg(jnp.concatenate([qs[k], rs[k]], axis=0), jnp.concatenate([bs[k], ks[k]], axis=0), _NT)
          for k in ids]
    strict = [strict_d[d] for _, d, _ in chains]
    incl = [incl_d[d] for _, d, _ in chains]
    l = [jnp.where(strict[k], gm[k][:SR, :SR], 0.0) for k in ids]
    m = [jnp.where(strict[k], gm[k][:SR, SR:], 0.0).astype(BF16) for k in ids]
    ppb = [jnp.concatenate([jnp.where(incl[k], gm[k][SR:, SR:], 0.0), jnp.where(incl[k], gm[k][SR:, :SR], 0.0)],
                           axis=1).astype(BF16) for k in ids]
    khbh = [jnp.concatenate([khs[k], bhs[k]], axis=0) for k in ids]
    n = [eye - l[k] for k in ids]
    pw = [l[k].astype(BF16) for k in ids]
    mv = [dot(m[k], vs[k]).astype(BF16) for k in ids]
    for _ in range(CH.bit_length() - 2):
        pw = [dot(pw[k], pw[k]).astype(BF16) for k in ids]
        n = [n[k] + dot(n[k].astype(BF16), pw[k]) for k in ids]
    x = [dot(n[k].astype(BF16), jnp.concatenate([qs[k], mv[k]], axis=1)) for k in ids]
    x1r = [jnp.concatenate([x[k][:, :GW].astype(BF16), rs[k]], axis=0) for k in ids]

    state = {(d, g): s_ref[d, g] for d in range(2) for g in range(NG)}
    for a in range(SUB):
        ka = [k for k in ids if chains[k][0] == a]
        s0 = {k: state[chains[k][1:]] for k in ka}
        zz = {k: dotg(x1r[k], s0[k].astype(BF16), _NT) for k in ka}
        vu = {k: jnp.concatenate([vs[k], (-(zz[k][:SR] + x[k][:, GW:])).astype(BF16)], axis=0) for k in ka}
        ys = {k: zz[k][SR:] + dot(ppb[k], vu[k]) for k in ka}
        for k in ka:
            _, d, g = chains[k]
            state[(d, g)] = s0[k] * tiles[(a, d)][1][:, sls[k]] + dotg(vu[k], khbh[k], _TN)
        for k in ka:
            _, d, g = chains[k]
            y = ys[k][0:CH]
            for h in range(1, GH):
                y = y + ys[k][h * CH:(h + 1) * CH]
            y_refs[d][tiles[(a, d)][2], sls[k]] = y
    for (d, g), s in state.items():
        s_ref[d, g] = s

    @pl.when(jnp.logical_and(is_ctx, c == ctx_chunks_per_seq - 1))
    def _():
        for (d, g), s in state.items():
            for h in range(GH):
                sfin_ref[d, 0, g * GH + h] = s[h * HEAD:(h + 1) * HEAD, h * HEAD:(h + 1) * HEAD]


def _wkv(r, v, kk, lw, kd, b, state_f, state_b, n_ctx_seq=BATCH, t_ctx=SEQ, n_smp_seq=DEC_BATCH, t_smp=DEC_SEQ):
    ntok = n_ctx_seq * t_ctx + n_smp_seq * t_smp
    blk = SUB * CH
    cps_c, cps_s = t_ctx // blk, t_smp // blk
    ncc = n_ctx_seq * cps_c
    nq = ncc + n_smp_seq * cps_s

    def seq_of(q):
        return jnp.where(q < ncc, q // cps_c, n_ctx_seq + (q - ncc) // cps_s)

    def row_block(d, q):
        ctx = q < ncc
        cps = jnp.where(ctx, cps_c, cps_s)
        c = jnp.where(ctx, q % cps_c, (q - ncc) % cps_s)
        return q - c + (c if d == 0 else cps - 1 - c)

    tok = [pl.BlockSpec((blk, D), functools.partial(lambda q, d: (row_block(d, q), 0), d=d)) for d in range(2)]
    tokd = [pl.BlockSpec((blk, D), functools.partial(lambda q, d: (row_block(d, q), d), d=d)) for d in range(2)]
    msk = pl.BlockSpec((2, SR, SR), lambda q: (0, 0, 0))
    tri = pl.BlockSpec((2, CH, CH), lambda q: (0, 0, 0))
    s0spec = pl.BlockSpec((2, 1, NG, GW, GW), lambda q: (0, jnp.maximum(seq_of(q) - n_ctx_seq, 0), 0, 0, 0))
    sspec = pl.BlockSpec((2, 1, NHEAD, HEAD, HEAD),
                         lambda q: (0, jnp.minimum(seq_of(q), n_ctx_seq - 1), 0, 0, 0))

    s0 = jnp.stack([state_f, state_b]).reshape(2, n_smp_seq, NG, GH, HEAD, HEAD)
    eye = jnp.eye(GH, dtype=F32)
    s0 = (s0[:, :, :, :, :, None, :] * eye[None, None, None, :, None, :, None]).reshape(2, n_smp_seq, NG, GW, GW)

    strict, incl, trim = _scan_masks()
    y_f, y_b, s_fin = pl.pallas_call(
        functools.partial(_wkv_kernel, n_ctx_chunks=ncc, ctx_chunks_per_seq=cps_c, smp_chunks_per_seq=cps_s),
        out_shape=[jax.ShapeDtypeStruct((ntok, D), F32), jax.ShapeDtypeStruct((ntok, D), F32),
                   jax.ShapeDtypeStruct((2, n_ctx_seq, NHEAD, HEAD, HEAD), F32)],
        grid=(nq,),
        in_specs=[tok[0], tok[0], tok[0], tokd[0], tokd[0], tokd[0],
                  tok[1], tok[1], tok[1], tokd[1], tokd[1], tokd[1], msk, msk, tri, s0spec],
        out_specs=[tok[0], tok[1], sspec],
        scratch_shapes=[pltpu.VMEM((2, NG, GW, GW), F32)],
        compiler_params=pltpu.CompilerParams(
            dimension_semantics=("arbitrary",), vmem_limit_bytes=VMEM_LIMIT),
        name="wkv_scan",
    )(r, v, kk, lw, kd, b, r, v, kk, lw, kd, b, strict, incl, trim, s0)
    return y_f, y_b, s_fin[0][:, None], s_fin[1][:, None]


RO_TM = 512


def _rwkv_out_kernel(x_ref, yf_ref, yb_ref, r_ref, v_ref, kd_ref, g_ref, mod_ref, gnw_ref, gnb_ref,
                     rk_ref, wo_ref, gpost_ref, ones_ref, o_ref):
    ones = ones_ref[...]
    ys = yf_ref[...] + yb_ref[...]
    mean = _head_sum(ys, ones) * (1.0 / HEAD)
    c = ys - mean
    var = _head_sum(c * c, ones) * (1.0 / HEAD)
    yn = c * lax.rsqrt(var + GN_EPS) * gnw_ref[...] + gnb_ref[...]
    kd = kd_ref[...]
    bonus = _head_sum(r_ref[...] * (kd[:, :D] + kd[:, D:]) * rk_ref[...], ones) * v_ref[...]
    out = _dot((yn + bonus) * g_ref[...], wo_ref[...])
    o_ref[...] = x_ref[...] + mod_ref[0, 5:6, :] * _rms(out, gpost_ref[...])


def _rwkv_out(x, y_f, y_b, r, v, kd, g, mod, p, gpost):
    tm = RO_TM
    row = lambda i: (i, 0)
    full2 = lambda i: (0, 0)
    vec = pl.BlockSpec((1, D), full2)
    tok = pl.BlockSpec((tm, D), row)
    tok2 = pl.BlockSpec((tm, 2 * D), row)
    return pl.pallas_call(
        _rwkv_out_kernel,
        out_shape=jax.ShapeDtypeStruct((NTOK, D), F32),
        grid=(NTOK // tm,),
        in_specs=[tok, tok, tok, tok, tok, tok2, tok,
                  pl.BlockSpec((1, N_MOD, D), lambda i: (_cond_row(i * tm), 0, 0)),
                  vec, vec, vec, pl.BlockSpec((D, D), full2), vec, pl.BlockSpec((HSUM_W, HSUM_W), full2)],
        out_specs=tok,
        compiler_params=pltpu.CompilerParams(
            dimension_semantics=("arbitrary",), vmem_limit_bytes=VMEM_LIMIT),
        name="rwkv_out",
    )(x, y_f, y_b, r, v, kd, g, mod, p['gn_w'].reshape(1, D), p['gn_b'].reshape(1, D),
      p['r_k'].reshape(1, D), p['w_o'].astype(BF16), gpost.reshape(1, D), _head_ones())


def _box_matrix(length, win):
    t = np.arange(length)
    lo = np.clip(t - win // 2, 0, length)
    hi = np.clip(t + win - win // 2, 0, length)
    a = ((t[None, :] >= lo[:, None]) & (t[None, :] < hi[:, None])).astype(np.float32)
    return a, (hi - lo).astype(np.float32)


def _pool_constants(grid):
    mats, inv = [], []
    for win in POOL_WINDOWS:
        if grid:
            ar, cr = _box_matrix(DEC_SEQ // GRID_W, win)
            ac, cc = _box_matrix(GRID_W, win)
            mats.append(np.kron(ar, ac))
            inv.append(1.0 / np.kron(cr, cc))
        else:
            a, c = _box_matrix(SEQ, win)
            mats.append(a)
            inv.append(1.0 / c)
    mats = jnp.asarray(np.stack(mats), dtype=BF16)
    inv = jnp.asarray(np.stack(inv)[:, :, None], dtype=F32)
    return mats, inv


def _pool_kernel(xf_ref, xg_ref, mod_ref, gpre_ref, a_ref, inv_ref, pw_ref, ps_ref, o_ref):
    xf = xf_ref[...]
    rs = lax.rsqrt(jnp.mean(xf * xf, axis=-1, keepdims=True) + RMS_EPS)
    part = (xg_ref[...] * rs) * gpre_ref[...] * (1.0 + mod_ref[0, 4:5, :]) + mod_ref[0, 3:4, :]
    m = _split_sum_left(a_ref[0], part) * inv_ref[0]
    o_ref[...] = _dot(m - part, pw_ref[0]) * ps_ref[...]


def _pool(x, mod, gpre, pool_w, pool_scale, t, seq0, nseq, grid):
    mats, inv = _pool_constants(grid)
    gc = POOL_GC
    return pl.pallas_call(
        _pool_kernel,
        out_shape=jax.ShapeDtypeStruct((nseq * t, D), F32),
        grid=(4, nseq),
        in_specs=[
            pl.BlockSpec((t, D), lambda g, s: (seq0 + s, 0)),
            pl.BlockSpec((t, gc), lambda g, s: (seq0 + s, g)),
            pl.BlockSpec((1, N_MOD, gc), lambda g, s: (_cond_row((seq0 + s) * t), 0, g)),
            pl.BlockSpec((1, gc), lambda g, s: (0, g)),
            pl.BlockSpec((1, t, t), lambda g, s: (g, 0, 0)),
            pl.BlockSpec((1, t, 1), lambda g, s: (g, 0, 0)),
            pl.BlockSpec((1, gc, gc), lambda g, s: (g, 0, 0)),
            pl.BlockSpec((1, gc), lambda g, s: (0, g)),
        ],
        out_specs=pl.BlockSpec((t, gc), lambda g, s: (s, g)),
        compiler_params=pltpu.CompilerParams(
            dimension_semantics=("arbitrary", "arbitrary"), vmem_limit_bytes=VMEM_LIMIT),
        name="pool_grid" if grid else "pool_seq",
    )(x, x, mod, gpre.reshape(1, D), mats, inv, pool_w.astype(BF16), pool_scale.reshape(1, D))


def kernel(x_prompt, x_sample, c, state_ctx_fwd, state_ctx_bwd, c_ctx, w_mod, b_mod, norm_pre, norm_post,
           ffn_w1, ffn_w3, ffn_w2, rwkv_mu, rwkv_w_rkv, rwkv_w0, rwkv_w1, rwkv_w2, rwkv_a0, rwkv_a1,
           rwkv_a2, rwkv_g1, rwkv_g2, rwkv_k_k, rwkv_k_a, rwkv_r_k, rwkv_gn_w, rwkv_gn_b, rwkv_w_o,
           pool_w, pool_scale):
    x = (x_prompt.reshape(N_CTX, D), x_sample.reshape(N_SMP, D))
    cond =jnp.concatenate([c_ctx[None], c, jnp.zeros((COND_ROWS - 1 - DEC_BATCH, D), F32)])
    mod = _modulation(cond, w_mod, b_mod).reshape(DEPTH, COND_ROWS, N_MOD, D)
    w1, w3, w2 = ffn_w1.astype(BF16), ffn_w3.astype(BF16), ffn_w2.astype(BF16)

    x = _ffn(x, mod[0], norm_pre[0, 0], norm_post[0, 0], w1[0, 0], w3[0, 0], w2[0, 0], 0)
    p = dict(mu=rwkv_mu[0], w_rkv=rwkv_w_rkv[0], w0=rwkv_w0[0], w1=rwkv_w1[0], w2=rwkv_w2[0],
             a0=rwkv_a0[0], a1=rwkv_a1[0], a2=rwkv_a2[0], g1=rwkv_g1[0], g2=rwkv_g2[0],
             k_k=rwkv_k_k[0], k_a=rwkv_k_a[0], r_k=rwkv_r_k[0], gn_w=rwkv_gn_w[0], gn_b=rwkv_gn_b[0],
             w_o=rwkv_w_o[0])
    r, v, kk, g, lw, kd, b = _rwkv_proj(x, mod[0], norm_pre[0, 1], p)
    y_f, y_b, new_f, new_b = _wkv(r, v, kk, lw, kd, b, state_ctx_fwd[:, 0], state_ctx_bwd[:, 0])
    x = _rwkv_out(x, y_f, y_b, r, v, kd, g, mod[0], p, norm_post[0, 1])
    x = _ffn(x, mod[0], norm_pre[0, 2], norm_post[0, 2], w1[0, 1], w3[0, 1], w2[0, 1], 6)

    x = _ffn(x, mod[1], norm_pre[1, 0], norm_post[1, 0], w1[1, 0], w3[1, 0], w2[1, 0], 0)
    pc = _pool(x, mod[1], norm_pre[1, 1], pool_w[0], pool_scale[0], SEQ, 0, BATCH, False)
    ps = _pool(x, mod[1], norm_pre[1, 1], pool_w[0], pool_scale[0], DEC_SEQ, N_CTX // DEC_SEQ, DEC_BATCH, True)
    y_ctx, y_smp = _ffn(x, mod[1], norm_pre[1, 2], norm_post[1, 2], w1[1, 1], w3[1, 1], w2[1, 1], 6,
                        pending=((pc, ps), norm_post[1, 1], 5), split_out=True)

    return (y_ctx.reshape(BATCH, SEQ, D), y_smp.reshape(DEC_BATCH, DEC_SEQ, D), new_f, new_b)
```

```python
import functools

import numpy as np
import jax
import jax.numpy as jnp
from jax import lax
from jax.experimental import pallas as pl
from jax.experimental.pallas import tpu as pltpu

D = 1024
HEAD = 64
NHEAD = D // HEAD
D_FF = 2816
DEPTH = 2
N_MOD = 9
POOL_WINDOWS = (2, 4, 8, 16)
POOL_GC = D // 4
GRID_W = 64
RMS_EPS = 1e-6
GN_EPS = 64e-5

BATCH, SEQ = 32, 256
DEC_BATCH, DEC_SEQ = 8, 2048
N_CTX = BATCH * SEQ
N_SMP = DEC_BATCH * DEC_SEQ
NTOK = N_CTX + N_SMP
COND_ROWS = 16

VMEM_LIMIT = 56 * 1024 * 1024

F32 = jnp.float32
BF16 = jnp.bfloat16


def _dot(a, b):
    return jnp.dot(a.astype(BF16), b.astype(BF16), preferred_element_type=F32)


def _rms(x, g):
    return (x * lax.rsqrt(jnp.mean(x * x, axis=-1, keepdims=True) + RMS_EPS)) * g


def _cond_row(start_row):
    return jnp.where(start_row < N_CTX, 0, 1 + (start_row - N_CTX) // DEC_SEQ)


def _split(x):
    hi = x.astype(BF16)
    return hi, (x - hi.astype(F32)).astype(BF16)


def _split_sum(x, ones):
    hi, lo = _split(x)
    return (jnp.dot(hi, ones, preferred_element_type=F32)
            + jnp.dot(lo, ones, preferred_element_type=F32))


HSUM_W = 256


def _head_sum(x, ones):
    return jnp.concatenate([_split_sum(x[:, i:i + HSUM_W], ones) for i in range(0, D, HSUM_W)], axis=1)


def _split_sum_left(ones, x):
    hi, lo = _split(x)
    return (jnp.dot(ones, hi, preferred_element_type=F32)
            + jnp.dot(ones, lo, preferred_element_type=F32))


def _mod_kernel(c_ref, w_ref, b_ref, o_ref):
    c = c_ref[...]
    o_ref[0] = _dot(jax.nn.silu(c), w_ref[0]) + b_ref[0]


def _modulation(cond, w_mod, b_mod):
    return pl.pallas_call(
        _mod_kernel,
        out_shape=jax.ShapeDtypeStruct((DEPTH, COND_ROWS, N_MOD * D), F32),
        grid=(DEPTH, N_MOD),
        in_specs=[
            pl.BlockSpec((COND_ROWS, D), lambda l, n: (0, 0)),
            pl.BlockSpec((1, D, D), lambda l, n: (l, 0, n)),
            pl.BlockSpec((1, 1, D), lambda l, n: (l, 0, n)),
        ],
        out_specs=pl.BlockSpec((1, COND_ROWS, D), lambda l, n: (l, 0, n)),
        compiler_params=pltpu.CompilerParams(
            dimension_semantics=("arbitrary", "arbitrary"), vmem_limit_bytes=VMEM_LIMIT),
        name="modulation",
    )(cond, w_mod, b_mod.reshape(DEPTH, 1, N_MOD * D))


FFN_TM = 512
FFN_SPLIT = 2


def _tok_operand(x, tm):
    if not isinstance(x, tuple):
        return [x], [pl.BlockSpec((tm, D), lambda i: (i, 0))]
    na = N_CTX // tm
    return list(x), [pl.BlockSpec((tm, D), lambda i: (jnp.minimum(i, na - 1), 0)),
                     pl.BlockSpec((tm, D), lambda i: (jnp.maximum(i - na, 0), 0))]


def _ffn_kernel(*refs, k0, pending_k, n_x, n_d, n_o):
    refs = list(refs)
    take = lambda n: [refs.pop(0) for _ in range(n)]
    x_refs = take(n_x)
    mod_ref, gpre_ref, gpost_ref, w1_ref, w3_ref, w2_ref = take(6)
    d_refs = take(n_d)
    gd_ref = take(1)[0] if n_d else None
    o_refs = take(n_o)
    is_ctx = pl.program_id(0) < N_CTX // FFN_TM

    def read(rs, rows):
        return rs[0][rows, :] if len(rs) == 1 else jnp.where(is_ctx, rs[0][rows, :], rs[1][rows, :])

    rt = FFN_TM // FFN_SPLIT
    for s in range(FFN_SPLIT):
        rows = slice(s * rt, (s + 1) * rt)
        x = read(x_refs, rows)
        if pending_k is not None:
            x = x + mod_ref[0, pending_k:pending_k + 1, :] * _rms(read(d_refs, rows), gd_ref[...])
        h = _rms(x, gpre_ref[...]) * (1.0 + mod_ref[0, k0 + 1:k0 + 2, :]) + mod_ref[0, k0:k0 + 1, :]
        h = h.astype(BF16)
        u = jnp.dot(h, w1_ref[...], preferred_element_type=F32)
        g = jnp.dot(h, w3_ref[...], preferred_element_type=F32)
        a = (jax.nn.silu(u) * g).astype(BF16)
        f = jnp.dot(a, w2_ref[...], preferred_element_type=F32)
        y = x + 0.5 * mod_ref[0, k0 + 2:k0 + 3, :] * _rms(f, gpost_ref[...])
        if n_o == 1:
            o_refs[0][rows, :] = y
        else:
            @pl.when(is_ctx)
            def _():
                o_refs[0][rows, :] = y

            @pl.when(jnp.logical_not(is_ctx))
            def _():
                o_refs[1][rows, :] = y


def _ffn(x, mod, gpre, gpost, w1, w3, w2, k0, pending=None, split_out=False):
    tm = FFN_TM
    vec = pl.BlockSpec((1, D), lambda i: (0, 0))
    resident = lambda shape: pl.BlockSpec(shape, lambda i: (0, 0), pipeline_mode=pl.Buffered(1))
    x_args, x_specs = _tok_operand(x, tm)
    args = x_args + [mod, gpre.reshape(1, D), gpost.reshape(1, D), w1, w3, w2]
    in_specs = x_specs + [
        pl.BlockSpec((1, N_MOD, D), lambda i: (_cond_row(i * tm), 0, 0)), vec, vec,
        resident((D, D_FF)), resident((D, D_FF)), resident((D_FF, D)),
    ]
    pending_k, n_d = None, 0
    if pending is not None:
        delta, gd, pending_k = pending
        d_args, d_specs = _tok_operand(delta, tm)
        n_d = len(d_args)
        args += d_args + [gd.reshape(1, D)]
        in_specs += d_specs + [vec]
    if split_out:
        out_shape = [jax.ShapeDtypeStruct((N_CTX, D), F32), jax.ShapeDtypeStruct((N_SMP, D), F32)]
        out_specs = _tok_operand((None, None), tm)[1]
    else:
        out_shape = jax.ShapeDtypeStruct((NTOK, D), F32)
        out_specs = _tok_operand(None, tm)[1][0]
    return pl.pallas_call(
        functools.partial(_ffn_kernel, k0=k0, pending_k=pending_k, n_x=len(x_args), n_d=n_d,
                          n_o=2 if split_out else 1),
        out_shape=out_shape,
        grid=(NTOK // tm,),
        in_specs=in_specs,
        out_specs=out_specs,
        compiler_params=pltpu.CompilerParams(
            dimension_semantics=("arbitrary",), vmem_limit_bytes=VMEM_LIMIT),
        name="ffn",
    )(*args)


RW_TM = 256
HALO = 8


def _rwkv_proj_kernel(x_ref, xp_ref, xn_ref, mod_ref, gpre_ref, mu_ref, wrkv_ref, w1_ref, w2_ref, w0_ref,
                      a1_ref, a2_ref, a0_ref, g1_ref, g2_ref, k_k_ref, k_a_ref, ones_ref,
                      r_out, v_out, kk_out, g_out, lw_out, kd_out, b_out):
    i = pl.program_id(0)
    shift = mod_ref[0, 3:4, :]
    scale = 1.0 + mod_ref[0, 4:5, :]
    gpre = gpre_ref[...]
    premod = lambda x: _rms(x, gpre) * scale + shift

    k = i - N_CTX // RW_TM
    per_seq = DEC_SEQ // RW_TM
    is_ctx = i < N_CTX // RW_TM
    first = jnp.logical_or(is_ctx, k % per_seq == 0)
    last = jnp.logical_or(is_ctx, k % per_seq == per_seq - 1)

    h = premod(x_ref[...])
    hp = jnp.where(first, 0.0, premod(xp_ref[...])[HALO - 1:HALO, :])
    hn = jnp.where(last, 0.0, premod(xn_ref[...])[0:1, :])
    rows = lax.broadcasted_iota(jnp.int32, (RW_TM, 1), 0)
    h_prev = jnp.where(rows == 0, hp, pltpu.roll(h, 1, 0))
    h_next = jnp.where(rows == RW_TM - 1, hn, pltpu.roll(h, RW_TM - 1, 0))
    xx = 0.5 * (h_prev + h_next) - h
    mix = lambda n: h + xx * mu_ref[n:n + 1, :]

    r = _dot(mix(0), wrkv_ref[0])
    kx = _dot(mix(2), wrkv_ref[1])
    v = _dot(mix(3), wrkv_ref[2])
    g = _dot(jax.nn.sigmoid(_dot(mix(5), g1_ref[...])), g2_ref[...])

    z = w0_ref[...] + _dot(jnp.tanh(_dot(mix(1), w1_ref[...])), w2_ref[...])
    a = jax.nn.sigmoid(a0_ref[...] + _dot(_dot(mix(4), a1_ref[...]), a2_ref[...]))

    kkraw = kx * k_k_ref[...]
    nrm = jnp.sqrt(_head_sum(kkraw * kkraw, ones_ref[...]))
    kk = kkraw / jnp.maximum(nrm, 1e-12)

    r_out[...] = r
    v_out[...] = v
    kk_out[...] = kk
    g_out[...] = g
    lw_out[...] = -float(np.exp(-0.5)) * jax.nn.sigmoid(z)
    ka = k_a_ref[...]
    for d in range(2):
        a_d = a[:, d * D:(d + 1) * D]
        kd_out[:, d * D:(d + 1) * D] = kx * (1.0 + (a_d - 1.0) * ka)
        b_out[:, d * D:(d + 1) * D] = kk * a_d


def _blockdiag2(m):
    z = jnp.zeros_like(m[0])
    return jnp.concatenate([jnp.concatenate([m[0], z], axis=1), jnp.concatenate([z, m[1]], axis=1)], axis=0)


def _head_ones():
    idx = np.arange(HSUM_W) // HEAD
    return jnp.asarray((idx[:, None] == idx[None, :]).astype(np.float32), dtype=BF16)


def _rwkv_proj(x, mod, gpre, p):
    tm = RW_TM
    nblk = NTOK // HALO
    row = lambda i: (i, 0)
    full2 = lambda i: (0, 0)
    full3 = lambda i: (0, 0, 0)
    w1c = jnp.concatenate([p['w1'][0], p['w1'][1]], axis=1).astype(BF16)
    a1c = jnp.concatenate([p['a1'][0], p['a1'][1]], axis=1).astype(BF16)
    w2bd = _blockdiag2(p['w2']).astype(BF16)
    a2bd = _blockdiag2(p['a2']).astype(BF16)
    return pl.pallas_call(
        _rwkv_proj_kernel,
        out_shape=[jax.ShapeDtypeStruct((NTOK, D), F32)] * 4 + [jax.ShapeDtypeStruct((NTOK, 2 * D), F32)] * 3,
        grid=(NTOK // tm,),
        in_specs=[
            pl.BlockSpec((tm, D), row),
            pl.BlockSpec((HALO, D), lambda i: (jnp.maximum(i * (tm // HALO) - 1, 0), 0)),
            pl.BlockSpec((HALO, D), lambda i: (jnp.minimum((i + 1) * (tm // HALO), nblk - 1), 0)),
            pl.BlockSpec((1, N_MOD, D), lambda i: (_cond_row(i * tm), 0, 0)),
            pl.BlockSpec((1, D), full2),
            pl.BlockSpec((6, D), full2),
            pl.BlockSpec((3, D, D), full3),
            pl.BlockSpec((D, 128), full2),
            pl.BlockSpec((128, 2 * D), full2),
            pl.BlockSpec((1, 2 * D), full2),
            pl.BlockSpec((D, 128), full2),
            pl.BlockSpec((128, 2 * D), full2),
            pl.BlockSpec((1, 2 * D), full2),
            pl.BlockSpec((D, 128), full2),
            pl.BlockSpec((128, D), full2),
            pl.BlockSpec((1, D), full2),
            pl.BlockSpec((1, D), full2),
            pl.BlockSpec((HSUM_W, HSUM_W), full2),
        ],
        out_specs=[pl.BlockSpec((tm, D), row)] * 4 + [pl.BlockSpec((tm, 2 * D), row)] * 3,
        compiler_params=pltpu.CompilerParams(
            dimension_semantics=("arbitrary",), vmem_limit_bytes=VMEM_LIMIT),
        name="rwkv_proj",
    )(x, x, x, mod, gpre.reshape(1, D), p['mu'], p['w_rkv'].astype(BF16), w1c, w2bd,
      p['w0'].reshape(1, 2 * D), a1c, a2bd, p['a0'].reshape(1, 2 * D),
      p['g1'].astype(BF16), p['g2'].astype(BF16), p['k_k'].reshape(1, D), p['k_a'].reshape(1, D),
      _head_ones())


CH = 32
SUB = 4
GH = 4
GW = GH * HEAD
NG = NHEAD // GH
SR = GH * CH

_NT = (((1,), (1,)), ((), ()))
_TN = (((0,), (0,)), ((), ()))


def _scan_masks():
    rho = np.arange(SR)
    same = (rho[:, None] // CH) == (rho[None, :] // CH)
    lower = rho[None, :] < rho[:, None]
    strict = np.stack([same & lower, same & lower.T])
    incl = strict | np.eye(SR, dtype=bool)[None]
    t = np.arange(CH)
    tri = np.stack([t[None, :] <= t[:, None], t[None, :] >= t[:, None]])
    return (jnp.asarray(strict.astype(np.float32)), jnp.asarray(incl.astype(np.float32)),
            jnp.asarray(tri.astype(np.float32), dtype=BF16))


def _wkv_kernel(rf_ref, vf_ref, kkf_ref, lwf_ref, kdf_ref, bf_ref, rb_ref, vb_ref, kkb_ref, lwb_ref, kdb_ref,
                bb_ref, strict_ref, incl_ref, tri_ref, s0_ref, yf_ref, yb_ref, sfin_ref, s_ref,
                *, n_ctx_chunks, ctx_chunks_per_seq, smp_chunks_per_seq):
    q = pl.program_id(0)
    is_ctx = q < n_ctx_chunks
    c = jnp.where(is_ctx, q % ctx_chunks_per_seq, (q - n_ctx_chunks) % smp_chunks_per_seq)

    @pl.when(c == 0)
    def _():
        s_ref[...] = jnp.where(is_ctx, 0.0, s0_ref[:, 0])

    def decayed(d, a, r_ref, v_ref, kk_ref, lw_ref, kd_ref, b_ref):
        pos = a if d == 0 else SUB - 1 - a
        rows = slice(pos * CH, (pos + 1) * CH)
        lw = lw_ref[rows, :]
        cum = _split_sum_left(tri_ref[d], lw)
        tot = cum[CH - 1:CH, :] if d == 0 else cum[0:1, :]
        e_neg = jnp.exp(-cum)
        e_rem = jnp.exp(tot - cum)
        kd = kd_ref[rows, :]
        bb = b_ref[rows, :]
        return (kk_ref[rows, :] * jnp.exp(cum - lw), r_ref[rows, :] * jnp.exp(cum), kd * e_neg, bb * e_neg,
                v_ref[rows, :], kd * e_rem, bb * e_rem), jnp.exp(tot), rows

    refs_d = ((rf_ref, vf_ref, kkf_ref, lwf_ref, kdf_ref, bf_ref),
              (rb_ref, vb_ref, kkb_ref, lwb_ref, kdb_ref, bb_ref))
    tiles = {(a, d): decayed(d, a, *refs_d[d]) for a in range(SUB) for d in range(2)}
    y_refs = (yf_ref, yb_ref)
    strict_d = [strict_ref[d] > 0.5 for d in range(2)]
    incl_d = [incl_ref[d] > 0.5 for d in range(2)]

    lane_head = lax.broadcasted_iota(jnp.int32, (CH, GW), 1) // HEAD
    head_lanes = [lane_head == h for h in range(GH)]

    def stack(x):
        return jnp.concatenate([jnp.where(m, x, 0.0) for m in head_lanes], axis=0).astype(BF16)

    eye = (lax.broadcasted_iota(jnp.int32, (SR, SR), 0)
           == lax.broadcasted_iota(jnp.int32, (SR, SR), 1)).astype(F32)
    dot = functools.partial(jnp.dot, preferred_element_type=F32)
    dotg = functools.partial(lax.dot_general, preferred_element_type=F32)

    chains = [(a, d, g) for a in range(SUB) for g in range(NG) for d in range(2)]
    ids = range(len(chains))
    sls = [slice(g * GW, (g + 1) * GW) for _, _, g in chains]
    ops = [[stack(x[:, sls[k]]) for x in tiles[(a, d)][0]] for k, (a, d, _) in enumerate(chains)]
    qs, rs, ks, bs, vs, khs, bhs = zip(*ops)
    gm = [dotg(jnp.concatenate([qs[k], rs[k]], axis=0), jnp.concatenate([bs[k], ks[k]], axis=0), _NT)
          for k in ids]
    strict = [strict_d[d] for _, d, _ in chains]
    incl = [incl_d[d] for _, d, _ in chains]
    l = [jnp.where(strict[k], gm[k][:SR, :SR], 0.0) for k in ids]
    m = [jnp.where(strict[k], gm[k][:SR, SR:], 0.0).astype(BF16) for k in ids]
    ppb = [jnp.concatenate([jnp.where(incl[k], gm[k][SR:, SR:], 0.0), jnp.where(incl[k], gm[k][SR:, :SR], 0.0)],
                           axis=1).astype(BF16) for k in ids]
    khbh = [jnp.concatenate([khs[k], bhs[k]], axis=0) for k in ids]
    n = [eye - l[k] for k in ids]
    pw = [l[k].astype(BF16) for k in ids]
    mv = [dot(m[k], vs[k]).astype(BF16) for k in ids]
    for _ in range(CH.bit_length() - 2):
        pw = [dot(pw[k], pw[k]).astype(BF16) for k in ids]
        n = [n[k] + dot(n[k].astype(BF16), pw[k]) for k in ids]
    x = [dot(n[k].astype(BF16), jnp.concatenate([qs[k], mv[k]], axis=1)) for k in ids]
    x1r = [jnp.concatenate([x[k][:, :GW].astype(BF16), rs[k]], axis=0) for k in ids]

    state = {(d, g): s_ref[d, g] for d in range(2) for g in range(NG)}
    for a in range(SUB):
        ka = [k for k in ids if chains[k][0] == a]
        s0 = {k: state[chains[k][1:]] for k in ka}
        zz = {k: dotg(x1r[k], s0[k].astype(BF16), _NT) for k in ka}
        vu = {k: jnp.concatenate([vs[k], (-(zz[k][:SR] + x[k][:, GW:])).astype(BF16)], axis=0) for k in ka}
        ys = {k: zz[k][SR:] + dot(ppb[k], vu[k]) for k in ka}
        for k in ka:
            _, d, g = chains[k]
            state[(d, g)] = s0[k] * tiles[(a, d)][1][:, sls[k]] + dotg(vu[k], khbh[k], _TN)
        for k in ka:
            _, d, g = chains[k]
            y = ys[k][0:CH]
            for h in range(1, GH):
                y = y + ys[k][h * CH:(h + 1) * CH]
            y_refs[d][tiles[(a, d)][2], sls[k]] = y
    for (d, g), s in state.items():
        s_ref[d, g] = s

    @pl.when(jnp.logical_and(is_ctx, c == ctx_chunks_per_seq - 1))
    def _():
        for (d, g), s in state.items():
            for h in range(GH):
                sfin_ref[d, 0, g * GH + h] = s[h * HEAD:(h + 1) * HEAD, h * HEAD:(h + 1) * HEAD]


def _wkv(r, v, kk, lw, kd, b, state_f, state_b, n_ctx_seq=BATCH, t_ctx=SEQ, n_smp_seq=DEC_BATCH, t_smp=DEC_SEQ):
    ntok = n_ctx_seq * t_ctx + n_smp_seq * t_smp
    blk = SUB * CH
    cps_c, cps_s = t_ctx // blk, t_smp // blk
    ncc = n_ctx_seq * cps_c
    nq = ncc + n_smp_seq * cps_s

    def seq_of(q):
        return jnp.where(q < ncc, q // cps_c, n_ctx_seq + (q - ncc) // cps_s)

    def row_block(d, q):
        ctx = q < ncc
        cps = jnp.where(ctx, cps_c, cps_s)
        c = jnp.where(ctx, q % cps_c, (q - ncc) % cps_s)
        return q - c + (c if d == 0 else cps - 1 - c)

    tok = [pl.BlockSpec((blk, D), functools.partial(lambda q, d: (row_block(d, q), 0), d=d)) for d in range(2)]
    tokd = [pl.BlockSpec((blk, D), functools.partial(lambda q, d: (row_block(d, q), d), d=d)) for d in range(2)]
    msk = pl.BlockSpec((2, SR, SR), lambda q: (0, 0, 0))
    tri = pl.BlockSpec((2, CH, CH), lambda q: (0, 0, 0))
    s0spec = pl.BlockSpec((2, 1, NG, GW, GW), lambda q: (0, jnp.maximum(seq_of(q) - n_ctx_seq, 0), 0, 0, 0))
    sspec = pl.BlockSpec((2, 1, NHEAD, HEAD, HEAD),
                         lambda q: (0, jnp.minimum(seq_of(q), n_ctx_seq - 1), 0, 0, 0))

    s0 = jnp.stack([state_f, state_b]).reshape(2, n_smp_seq, NG, GH, HEAD, HEAD)
    eye = jnp.eye(GH, dtype=F32)
    s0 = (s0[:, :, :, :, :, None, :] * eye[None, None, None, :, None, :, None]).reshape(2, n_smp_seq, NG, GW, GW)

    strict, incl, trim = _scan_masks()
    y_f, y_b, s_fin = pl.pallas_call(
        functools.partial(_wkv_kernel, n_ctx_chunks=ncc, ctx_chunks_per_seq=cps_c, smp_chunks_per_seq=cps_s),
        out_shape=[jax.ShapeDtypeStruct((ntok, D), F32), jax.ShapeDtypeStruct((ntok, D), F32),
                   jax.ShapeDtypeStruct((2, n_ctx_seq, NHEAD, HEAD, HEAD), F32)],
        grid=(nq,),
        in_specs=[tok[0], tok[0], tok[0], tokd[0], tokd[0], tokd[0],
                  tok[1], tok[1], tok[1], tokd[1], tokd[1], tokd[1], msk, msk, tri, s0spec],
        out_specs=[tok[0], tok[1], sspec],
        scratch_shapes=[pltpu.VMEM((2, NG, GW, GW), F32)],
        compiler_params=pltpu.CompilerParams(
            dimension_semantics=("arbitrary",), vmem_limit_bytes=VMEM_LIMIT),
        name="wkv_scan",
    )(r, v, kk, lw, kd, b, r, v, kk, lw, kd, b, strict, incl, trim, s0)
    return y_f, y_b, s_fin[0][:, None], s_fin[1][:, None]


RO_TM = 512


def _rwkv_out_kernel(x_ref, yf_ref, yb_ref, r_ref, v_ref, kd_ref, g_ref, mod_ref, gnw_ref, gnb_ref,
                     rk_ref, wo_ref, gpost_ref, ones_ref, o_ref):
    ones = ones_ref[...]
    ys = yf_ref[...] + yb_ref[...]
    mean = _head_sum(ys, ones) * (1.0 / HEAD)
    c = ys - mean
    var = _head_sum(c * c, ones) * (1.0 / HEAD)
    yn = c * lax.rsqrt(var + GN_EPS) * gnw_ref[...] + gnb_ref[...]
    kd = kd_ref[...]
    bonus = _head_sum(r_ref[...] * (kd[:, :D] + kd[:, D:]) * rk_ref[...], ones) * v_ref[...]
    out = _dot((yn + bonus) * g_ref[...], wo_ref[...])
    o_ref[...] = x_ref[...] + mod_ref[0, 5:6, :] * _rms(out, gpost_ref[...])


def _rwkv_out(x, y_f, y_b, r, v, kd, g, mod, p, gpost):
    tm = RO_TM
    row = lambda i: (i, 0)
    full2 = lambda i: (0, 0)
    vec = pl.BlockSpec((1, D), full2)
    tok = pl.BlockSpec((tm, D), row)
    tok2 = pl.BlockSpec((tm, 2 * D), row)
    return pl.pallas_call(
        _rwkv_out_kernel,
        out_shape=jax.ShapeDtypeStruct((NTOK, D), F32),
        grid=(NTOK // tm,),
        in_specs=[tok, tok, tok, tok, tok, tok2, tok,
                  pl.BlockSpec((1, N_MOD, D), lambda i: (_cond_row(i * tm), 0, 0)),
                  vec, vec, vec, pl.BlockSpec((D, D), full2), vec, pl.BlockSpec((HSUM_W, HSUM_W), full2)],
        out_specs=tok,
        compiler_params=pltpu.CompilerParams(
            dimension_semantics=("arbitrary",), vmem_limit_bytes=VMEM_LIMIT),
        name="rwkv_out",
    )(x, y_f, y_b, r, v, kd, g, mod, p['gn_w'].reshape(1, D), p['gn_b'].reshape(1, D),
      p['r_k'].reshape(1, D), p['w_o'].astype(BF16), gpost.reshape(1, D), _head_ones())


def _box_matrix(length, win):
    t = np.arange(length)
    lo = np.clip(t - win // 2, 0, length)
    hi = np.clip(t + win - win // 2, 0, length)
    a = ((t[None, :] >= lo[:, None]) & (t[None, :] < hi[:, None])).astype(np.float32)
    return a, (hi - lo).astype(np.float32)


def _pool_constants(grid):
    mats, inv = [], []
    for win in POOL_WINDOWS:
        if grid:
            ar, cr = _box_matrix(DEC_SEQ // GRID_W, win)
            ac, cc = _box_matrix(GRID_W, win)
            mats.append(np.kron(ar, ac))
            inv.append(1.0 / np.kron(cr, cc))
        else:
            a, c = _box_matrix(SEQ, win)
            mats.append(a)
            inv.append(1.0 / c)
    mats = jnp.asarray(np.stack(mats), dtype=BF16)
    inv = jnp.asarray(np.stack(inv)[:, :, None], dtype=F32)
    return mats, inv


def _pool_kernel(xf_ref, xg_ref, mod_ref, gpre_ref, a_ref, inv_ref, pw_ref, ps_ref, o_ref):
    xf = xf_ref[...]
    rs = lax.rsqrt(jnp.mean(xf * xf, axis=-1, keepdims=True) + RMS_EPS)
    part = (xg_ref[...] * rs) * gpre_ref[...] * (1.0 + mod_ref[0, 4:5, :]) + mod_ref[0, 3:4, :]
    m = _split_sum_left(a_ref[0], part) * inv_ref[0]
    o_ref[...] = _dot(m - part, pw_ref[0]) * ps_ref[...]


def _pool(x, mod, gpre, pool_w, pool_scale, t, seq0, nseq, grid):
    mats, inv = _pool_constants(grid)
    gc = POOL_GC
    return pl.pallas_call(
        _pool_kernel,
        out_shape=jax.ShapeDtypeStruct((nseq * t, D), F32),
        grid=(4, nseq),
        in_specs=[
            pl.BlockSpec((t, D), lambda g, s: (seq0 + s, 0)),
            pl.BlockSpec((t, gc), lambda g, s: (seq0 + s, g)),
            pl.BlockSpec((1, N_MOD, gc), lambda g, s: (_cond_row((seq0 + s) * t), 0, g)),
            pl.BlockSpec((1, gc), lambda g, s: (0, g)),
            pl.BlockSpec((1, t, t), lambda g, s: (g, 0, 0)),
            pl.BlockSpec((1, t, 1), lambda g, s: (g, 0, 0)),
            pl.BlockSpec((1, gc, gc), lambda g, s: (g, 0, 0)),
            pl.BlockSpec((1, gc), lambda g, s: (0, g)),
        ],
        out_specs=pl.BlockSpec((t, gc), lambda g, s: (s, g)),
        compiler_params=pltpu.CompilerParams(
            dimension_semantics=("arbitrary", "arbitrary"), vmem_limit_bytes=VMEM_LIMIT),
        name="pool_grid" if grid else "pool_seq",
    )(x, x, mod, gpre.reshape(1, D), mats, inv, pool_w.astype(BF16), pool_scale.reshape(1, D))


def kernel(x_prompt, x_sample, c, state_ctx_fwd, state_ctx_bwd, c_ctx, w_mod, b_mod, norm_pre, norm_post,
           ffn_w1, ffn_w3, ffn_w2, rwkv_mu, rwkv_w_rkv, rwkv_w0, rwkv_w1, rwkv_w2, rwkv_a0, rwkv_a1,
           rwkv_a2, rwkv_g1, rwkv_g2, rwkv_k_k, rwkv_k_a, rwkv_r_k, rwkv_gn_w, rwkv_gn_b, rwkv_w_o,
           pool_w, pool_scale):
    x = (x_prompt.reshape(N_CTX, D), x_sample.reshape(N_SMP, D))
    cond =jnp.concatenate([c_ctx[None], c, jnp.zeros((COND_ROWS - 1 - DEC_BATCH, D), F32)])
    mod = _modulation(cond, w_mod, b_mod).reshape(DEPTH, COND_ROWS, N_MOD, D)
    w1, w3, w2 = ffn_w1.astype(BF16), ffn_w3.astype(BF16), ffn_w2.astype(BF16)

    x = _ffn(x, mod[0], norm_pre[0, 0], norm_post[0, 0], w1[0, 0], w3[0, 0], w2[0, 0], 0)
    p = dict(mu=rwkv_mu[0], w_rkv=rwkv_w_rkv[0], w0=rwkv_w0[0], w1=rwkv_w1[0], w2=rwkv_w2[0],
             a0=rwkv_a0[0], a1=rwkv_a1[0], a2=rwkv_a2[0], g1=rwkv_g1[0], g2=rwkv_g2[0],
             k_k=rwkv_k_k[0], k_a=rwkv_k_a[0], r_k=rwkv_r_k[0], gn_w=rwkv_gn_w[0], gn_b=rwkv_gn_b[0],
             w_o=rwkv_w_o[0])
    r, v, kk, g, lw, kd, b = _rwkv_proj(x, mod[0], norm_pre[0, 1], p)
    y_f, y_b, new_f, new_b = _wkv(r, v, kk, lw, kd, b, state_ctx_fwd[:, 0], state_ctx_bwd[:, 0])
    x = _rwkv_out(x, y_f, y_b, r, v, kd, g, mod[0], p, norm_post[0, 1])
    x = _ffn(x, mod[0], norm_pre[0, 2], norm_post[0, 2], w1[0, 1], w3[0, 1], w2[0, 1], 6)

    x = _ffn(x, mod[1], norm_pre[1, 0], norm_post[1, 0], w1[1, 0], w3[1, 0], w2[1, 0], 0)
    pc = _pool(x, mod[1], norm_pre[1, 1], pool_w[0], pool_scale[0], SEQ, 0, BATCH, False)
    ps = _pool(x, mod[1], norm_pre[1, 1], pool_w[0], pool_scale[0], DEC_SEQ, N_CTX // DEC_SEQ, DEC_BATCH, True)
    y_ctx, y_smp = _ffn(x, mod[1], norm_pre[1, 2], norm_post[1, 2], w1[1, 1], w3[1, 1], w2[1, 1], 6,
                        pending=((pc, ps), norm_post[1, 1], 5), split_out=True)

    return (y_ctx.reshape(BATCH, SEQ, D), y_smp.reshape(DEC_BATCH, DEC_SEQ, D), new_f, new_b)
```

```python
import functools

import numpy as np
import jax
import jax.numpy as jnp
from jax import lax
from jax.experimental import pallas as pl
from jax.experimental.pallas import tpu as pltpu

D = 1024
HEAD = 64
NHEAD = D // HEAD
D_FF = 2816
DEPTH = 2
N_MOD = 9
POOL_WINDOWS = (2, 4, 8, 16)
POOL_GC = D // 4
GRID_W = 64
RMS_EPS = 1e-6
GN_EPS = 64e-5

BATCH, SEQ = 32, 256
DEC_BATCH, DEC_SEQ = 8, 2048
N_CTX = BATCH * SEQ
N_SMP = DEC_BATCH * DEC_SEQ
NTOK = N_CTX + N_SMP
COND_ROWS = 16

VMEM_LIMIT = 56 * 1024 * 1024

F32 = jnp.float32
BF16 = jnp.bfloat16


def _dot(a, b):
    return jnp.dot(a.astype(BF16), b.astype(BF16), preferred_element_type=F32)


def _rms(x, g):
    return (x * lax.rsqrt(jnp.mean(x * x, axis=-1, keepdims=True) + RMS_EPS)) * g


def _cond_row(start_row):
    return jnp.where(start_row < N_CTX, 0, 1 + (start_row - N_CTX) // DEC_SEQ)


def _split(x):
    hi = x.astype(BF16)
    return hi, (x - hi.astype(F32)).astype(BF16)


def _split_sum(x, ones):
    hi, lo = _split(x)
    return (jnp.dot(hi, ones, preferred_element_type=F32)
            + jnp.dot(lo, ones, preferred_element_type=F32))


HSUM_W = 256


def _head_sum(x, ones):
    return jnp.concatenate([_split_sum(x[:, i:i + HSUM_W], ones) for i in range(0, D, HSUM_W)], axis=1)


def _split_sum_left(ones, x):
    hi, lo = _split(x)
    return (jnp.dot(ones, hi, preferred_element_type=F32)
            + jnp.dot(ones, lo, preferred_element_type=F32))


def _mod_kernel(c_ref, w_ref, b_ref, o_ref):
    c = c_ref[...]
    o_ref[0] = _dot(jax.nn.silu(c), w_ref[0]) + b_ref[0]


def _modulation(cond, w_mod, b_mod):
    return pl.pallas_call(
        _mod_kernel,
        out_shape=jax.ShapeDtypeStruct((DEPTH, COND_ROWS, N_MOD * D), F32),
        grid=(DEPTH, N_MOD),
        in_specs=[
            pl.BlockSpec((COND_ROWS, D), lambda l, n: (0, 0)),
            pl.BlockSpec((1, D, D), lambda l, n: (l, 0, n)),
            pl.BlockSpec((1, 1, D), lambda l, n: (l, 0, n)),
        ],
        out_specs=pl.BlockSpec((1, COND_ROWS, D), lambda l, n: (l, 0, n)),
        compiler_params=pltpu.CompilerParams(
            dimension_semantics=("arbitrary", "arbitrary"), vmem_limit_bytes=VMEM_LIMIT),
        name="modulation",
    )(cond, w_mod, b_mod.reshape(DEPTH, 1, N_MOD * D))


FFN_TM = 512
FFN_SPLIT = 2


def _tok_operand(x, tm):
    if not isinstance(x, tuple):
        return [x], [pl.BlockSpec((tm, D), lambda i: (i, 0))]
    na = N_CTX // tm
    return list(x), [pl.BlockSpec((tm, D), lambda i: (jnp.minimum(i, na - 1), 0)),
                     pl.BlockSpec((tm, D), lambda i: (jnp.maximum(i - na, 0), 0))]


def _ffn_kernel(*refs, k0, pending_k, n_x, n_d, n_o):
    refs = list(refs)
    take = lambda n: [refs.pop(0) for _ in range(n)]
    x_refs = take(n_x)
    mod_ref, gpre_ref, gpost_ref, w1_ref, w3_ref, w2_ref = take(6)
    d_refs = take(n_d)
    gd_ref = take(1)[0] if n_d else None
    o_refs = take(n_o)
    is_ctx = pl.program_id(0) < N_CTX // FFN_TM

    def read(rs, rows):
        return rs[0][rows, :] if len(rs) == 1 else jnp.where(is_ctx, rs[0][rows, :], rs[1][rows, :])

    rt = FFN_TM // FFN_SPLIT
    for s in range(FFN_SPLIT):
        rows = slice(s * rt, (s + 1) * rt)
        x = read(x_refs, rows)
        if pending_k is not None:
            x = x + mod_ref[0, pending_k:pending_k + 1, :] * _rms(read(d_refs, rows), gd_ref[...])
        h = _rms(x, gpre_ref[...]) * (1.0 + mod_ref[0, k0 + 1:k0 + 2, :]) + mod_ref[0, k0:k0 + 1, :]
        h = h.astype(BF16)
        u = jnp.dot(h, w1_ref[...], preferred_element_type=F32)
        g = jnp.dot(h, w3_ref[...], preferred_element_type=F32)
        a = (jax.nn.silu(u) * g).astype(BF16)
        f = jnp.dot(a, w2_ref[...], preferred_element_type=F32)
        y = x + 0.5 * mod_ref[0, k0 + 2:k0 + 3, :] * _rms(f, gpost_ref[...])
        if n_o == 1:
            o_refs[0][rows, :] = y
        else:
            @pl.when(is_ctx)
            def _():
                o_refs[0][rows, :] = y

            @pl.when(jnp.logical_not(is_ctx))
            def _():
                o_refs[1][rows, :] = y


def _ffn(x, mod, gpre, gpost, weights, wsel, k0, pending=None, split_out=False):
    tm = FFN_TM
    vec = pl.BlockSpec((1, D), lambda i: (0, 0))
    resident = lambda rows, cols: pl.BlockSpec((None, None, rows, cols), lambda i: (*wsel, 0, 0),
                                               pipeline_mode=pl.Buffered(1))
    x_args, x_specs = _tok_operand(x, tm)
    args = x_args + [mod, gpre.reshape(1, D), gpost.reshape(1, D), *weights]
    in_specs = x_specs + [
        pl.BlockSpec((1, N_MOD, D), lambda i: (_cond_row(i * tm), 0, 0)), vec, vec,
        resident(D, D_FF), resident(D, D_FF), resident(D_FF, D),
    ]
    pending_k, n_d = None, 0
    if pending is not None:
        delta, gd, pending_k = pending
        d_args, d_specs = _tok_operand(delta, tm)
        n_d = len(d_args)
        args += d_args + [gd.reshape(1, D)]
        in_specs += d_specs + [vec]
    if split_out:
        out_shape = [jax.ShapeDtypeStruct((N_CTX, D), F32), jax.ShapeDtypeStruct((N_SMP, D), F32)]
        out_specs = _tok_operand((None, None), tm)[1]
    else:
        out_shape = jax.ShapeDtypeStruct((NTOK, D), F32)
        out_specs = _tok_operand(None, tm)[1][0]
    return pl.pallas_call(
        functools.partial(_ffn_kernel, k0=k0, pending_k=pending_k, n_x=len(x_args), n_d=n_d,
                          n_o=2 if split_out else 1),
        out_shape=out_shape,
        grid=(NTOK // tm,),
        in_specs=in_specs,
        out_specs=out_specs,
        compiler_params=pltpu.CompilerParams(
            dimension_semantics=("arbitrary",), vmem_limit_bytes=VMEM_LIMIT),
        name="ffn",
    )(*args)


RW_TM = 256
HALO = 8


def _rwkv_proj_kernel(x_ref, xp_ref, xn_ref, mod_ref, gpre_ref, mu_ref, wrkv_ref, w1_ref, w2_ref, w0_ref,
                      a1_ref, a2_ref, a0_ref, g1_ref, g2_ref, k_k_ref, k_a_ref, ones_ref,
                      r_out, v_out, kk_out, g_out, lw_out, kd_out, b_out):
    i = pl.program_id(0)
    shift = mod_ref[0, 3:4, :]
    scale = 1.0 + mod_ref[0, 4:5, :]
    gpre = gpre_ref[...]
    premod = lambda x: _rms(x, gpre) * scale + shift

    k = i - N_CTX // RW_TM
    per_seq = DEC_SEQ // RW_TM
    is_ctx = i < N_CTX // RW_TM
    first = jnp.logical_or(is_ctx, k % per_seq == 0)
    last = jnp.logical_or(is_ctx, k % per_seq == per_seq - 1)

    h = premod(x_ref[...])
    hp = jnp.where(first, 0.0, premod(xp_ref[...])[HALO - 1:HALO, :])
    hn = jnp.where(last, 0.0, premod(xn_ref[...])[0:1, :])
    rows = lax.broadcasted_iota(jnp.int32, (RW_TM, 1), 0)
    h_prev = jnp.where(rows == 0, hp, pltpu.roll(h, 1, 0))
    h_next = jnp.where(rows == RW_TM - 1, hn, pltpu.roll(h, RW_TM - 1, 0))
    xx = 0.5 * (h_prev + h_next) - h
    mix = lambda n: h + xx * mu_ref[n:n + 1, :]

    r = _dot(mix(0), wrkv_ref[0])
    kx = _dot(mix(2), wrkv_ref[1])
    v = _dot(mix(3), wrkv_ref[2])
    g = _dot(jax.nn.sigmoid(_dot(mix(5), g1_ref[...])), g2_ref[...])

    z = w0_ref[...] + _dot(jnp.tanh(_dot(mix(1), w1_ref[...])), w2_ref[...])
    a = jax.nn.sigmoid(a0_ref[...] + _dot(_dot(mix(4), a1_ref[...]), a2_ref[...]))

    kkraw = kx * k_k_ref[...]
    nrm = jnp.sqrt(_head_sum(kkraw * kkraw, ones_ref[...]))
    kk = kkraw / jnp.maximum(nrm, 1e-12)

    r_out[...] = r.astype(BF16)
    v_out[...] = v.astype(BF16)
    kk_out[...] = kk.astype(BF16)
    g_out[...] = g.astype(BF16)
    lw_out[...] = -float(np.exp(-0.5)) * jax.nn.sigmoid(z)
    ka = k_a_ref[...]
    for d in range(2):
        a_d = a[:, d * D:(d + 1) * D]
        kd_out[:, d * D:(d + 1) * D] = (kx * (1.0 + (a_d - 1.0) * ka)).astype(BF16)
        b_out[:, d * D:(d + 1) * D] = (kk * a_d).astype(BF16)


def _blockdiag2(m):
    z = jnp.zeros_like(m[0])
    return jnp.concatenate([jnp.concatenate([m[0], z], axis=1), jnp.concatenate([z, m[1]], axis=1)], axis=0)


def _head_ones():
    idx = np.arange(HSUM_W) // HEAD
    return jnp.asarray((idx[:, None] == idx[None, :]).astype(np.float32), dtype=BF16)


def _rwkv_proj(x, mod, gpre, p):
    tm = RW_TM
    nblk = NTOK // HALO
    row = lambda i: (i, 0)
    full2 = lambda i: (0, 0)
    full3 = lambda i: (0, 0, 0)
    w1c = jnp.concatenate([p['w1'][0], p['w1'][1]], axis=1).astype(BF16)
    a1c = jnp.concatenate([p['a1'][0], p['a1'][1]], axis=1).astype(BF16)
    w2bd = _blockdiag2(p['w2']).astype(BF16)
    a2bd = _blockdiag2(p['a2']).astype(BF16)
    return pl.pallas_call(
        _rwkv_proj_kernel,
        out_shape=([jax.ShapeDtypeStruct((NTOK, D), BF16)] * 4 + [jax.ShapeDtypeStruct((NTOK, 2 * D), F32)]
                   + [jax.ShapeDtypeStruct((NTOK, 2 * D), BF16)] * 2),
        grid=(NTOK // tm,),
        in_specs=[
            pl.BlockSpec((tm, D), row),
            pl.BlockSpec((HALO, D), lambda i: (jnp.maximum(i * (tm // HALO) - 1, 0), 0)),
            pl.BlockSpec((HALO, D), lambda i: (jnp.minimum((i + 1) * (tm // HALO), nblk - 1), 0)),
            pl.BlockSpec((1, N_MOD, D), lambda i: (_cond_row(i * tm), 0, 0)),
            pl.BlockSpec((1, D), full2),
            pl.BlockSpec((6, D), full2),
            pl.BlockSpec((3, D, D), full3),
            pl.BlockSpec((D, 128), full2),
            pl.BlockSpec((128, 2 * D), full2),
            pl.BlockSpec((1, 2 * D), full2),
            pl.BlockSpec((D, 128), full2),
            pl.BlockSpec((128, 2 * D), full2),
            pl.BlockSpec((1, 2 * D), full2),
            pl.BlockSpec((D, 128), full2),
            pl.BlockSpec((128, D), full2),
            pl.BlockSpec((1, D), full2),
            pl.BlockSpec((1, D), full2),
            pl.BlockSpec((HSUM_W, HSUM_W), full2),
        ],
        out_specs=[pl.BlockSpec((tm, D), row)] * 4 + [pl.BlockSpec((tm, 2 * D), row)] * 3,
        compiler_params=pltpu.CompilerParams(
            dimension_semantics=("arbitrary",), vmem_limit_bytes=VMEM_LIMIT),
        name="rwkv_proj",
    )(x, x, x, mod, gpre.reshape(1, D), p['mu'], p['w_rkv'].astype(BF16), w1c, w2bd,
      p['w0'].reshape(1, 2 * D), a1c, a2bd, p['a0'].reshape(1, 2 * D),
      p['g1'].astype(BF16), p['g2'].astype(BF16), p['k_k'].reshape(1, D), p['k_a'].reshape(1, D),
      _head_ones())


CH = 32
SUB = 4
GH = 4
GW = GH * HEAD
NG = NHEAD // GH
SR = GH * CH

_NT = (((1,), (1,)), ((), ()))
_TN = (((0,), (0,)), ((), ()))


def _scan_masks():
    rho = np.arange(SR)
    same = (rho[:, None] // CH) == (rho[None, :] // CH)
    lower = rho[None, :] < rho[:, None]
    strict = np.stack([same & lower, same & lower.T])
    incl = strict | np.eye(SR, dtype=bool)[None]
    t = np.arange(CH)
    tri = np.stack([t[None, :] <= t[:, None], t[None, :] >= t[:, None]])
    return (jnp.asarray(strict.astype(np.float32)), jnp.asarray(incl.astype(np.float32)),
            jnp.asarray(tri.astype(np.float32), dtype=BF16))


def _wkv_kernel(rf_ref, vf_ref, kkf_ref, lwf_ref, kdf_ref, bf_ref, rb_ref, vb_ref, kkb_ref, lwb_ref, kdb_ref,
                bb_ref, strict_ref, incl_ref, tri_ref, s0_ref, yf_ref, yb_ref, sfin_ref, s_ref,
                *, n_ctx_chunks, ctx_chunks_per_seq, smp_chunks_per_seq):
    q = pl.program_id(0)
    is_ctx = q < n_ctx_chunks
    c = jnp.where(is_ctx, q % ctx_chunks_per_seq, (q - n_ctx_chunks) % smp_chunks_per_seq)

    @pl.when(c == 0)
    def _():
        s_ref[...] = jnp.where(is_ctx, 0.0, s0_ref[:, 0])

    def decayed(d, a, r_ref, v_ref, kk_ref, lw_ref, kd_ref, b_ref):
        pos = a if d == 0 else SUB - 1 - a
        rows = slice(pos * CH, (pos + 1) * CH)
        lw = lw_ref[rows, :]
        cum = _split_sum_left(tri_ref[d], lw)
        tot = cum[CH - 1:CH, :] if d == 0 else cum[0:1, :]
        e_neg = jnp.exp(-cum)
        e_rem = jnp.exp(tot - cum)
        kd = kd_ref[rows, :]
        bb = b_ref[rows, :]
        return (kk_ref[rows, :] * jnp.exp(cum - lw), r_ref[rows, :] * jnp.exp(cum), kd * e_neg, bb * e_neg,
                v_ref[rows, :], kd * e_rem, bb * e_rem), jnp.exp(tot), rows

    refs_d = ((rf_ref, vf_ref, kkf_ref, lwf_ref, kdf_ref, bf_ref),
              (rb_ref, vb_ref, kkb_ref, lwb_ref, kdb_ref, bb_ref))
    tiles = {(a, d): decayed(d, a, *refs_d[d]) for a in range(SUB) for d in range(2)}
    y_refs = (yf_ref, yb_ref)
    strict_d = [strict_ref[d] > 0.5 for d in range(2)]
    incl_d = [incl_ref[d] > 0.5 for d in range(2)]

    lane_head = lax.broadcasted_iota(jnp.int32, (CH, GW), 1) // HEAD
    head_lanes = [lane_head == h for h in range(GH)]

    def stack(x):
        return jnp.concatenate([jnp.where(m, x, 0.0) for m in head_lanes], axis=0).astype(BF16)

    eye = (lax.broadcasted_iota(jnp.int32, (SR, SR), 0)
           == lax.broadcasted_iota(jnp.int32, (SR, SR), 1)).astype(F32)
    dot = functools.partial(jnp.dot, preferred_element_type=F32)
    dotg = functools.partial(lax.dot_general, preferred_element_type=F32)

    chains = [(a, d, g) for a in range(SUB) for g in range(NG) for d in range(2)]
    ids = range(len(chains))
    sls = [slice(g * GW, (g + 1) * GW) for _, _, g in chains]
    ops = [[stack(x[:, sls[k]]) for x in tiles[(a, d)][0]] for k, (a, d, _) in enumerate(chains)]
    qs, rs, ks, bs, vs, khs, bhs = zip(*ops)
    qr = [jnp.concatenate([qs[k], rs[k]], axis=0) for k in ids]
    gm = [dotg(qr[k], jnp.concatenate([bs[k], ks[k]], axis=0), _NT) for k in ids]
    strict = [strict_d[d] for _, d, _ in chains]
    incl = [incl_d[d] for _, d, _ in chains]
    l = [jnp.where(strict[k], gm[k][:SR, :SR], 0.0) for k in ids]
    m = [jnp.where(strict[k], gm[k][:SR, SR:], 0.0).astype(BF16) for k in ids]
    ppb = [jnp.concatenate([jnp.where(incl[k], gm[k][SR:, SR:], 0.0), jnp.where(incl[k], gm[k][SR:, :SR], 0.0)],
                           axis=1).astype(BF16) for k in ids]
    khbh = [jnp.concatenate([khs[k], bhs[k]], axis=0) for k in ids]
    n = [eye - l[k] for k in ids]
    pw = [l[k].astype(BF16) for k in ids]
    mv = [dot(m[k], vs[k]) for k in ids]
    for _ in range(CH.bit_length() - 2):
        pw = [dot(pw[k], pw[k]).astype(BF16) for k in ids]
        n = [n[k] + dot(n[k].astype(BF16), pw[k]) for k in ids]
    tinv = [n[k].astype(BF16) for k in ids]

    state = {(d, g): s_ref[d, g] for d in range(2) for g in range(NG)}
    for a in range(SUB):
        ka = [k for k in ids if chains[k][0] == a]
        s0 = {k: state[chains[k][1:]] for k in ka}
        zz = {k: dotg(qr[k], s0[k].astype(BF16), _NT) for k in ka}
        u = {k: dot(tinv[k], (zz[k][:SR] + mv[k]).astype(BF16)) for k in ka}
        vu = {k: jnp.concatenate([vs[k], (-u[k]).astype(BF16)], axis=0) for k in ka}
        ys = {k: zz[k][SR:] + dot(ppb[k], vu[k]) for k in ka}
        for k in ka:
            _, d, g = chains[k]
            state[(d, g)] = s0[k] * tiles[(a, d)][1][:, sls[k]] + dotg(vu[k], khbh[k], _TN)
        for k in ka:
            _, d, g = chains[k]
            y = ys[k][0:CH]
            for h in range(1, GH):
                y = y + ys[k][h * CH:(h + 1) * CH]
            y_refs[d][tiles[(a, d)][2], sls[k]] = y
    for (d, g), s in state.items():
        s_ref[d, g] = s

    @pl.when(jnp.logical_and(is_ctx, c == ctx_chunks_per_seq - 1))
    def _():
        for (d, g), s in state.items():
            for h in range(GH):
                sfin_ref[d, 0, g * GH + h] = s[h * HEAD:(h + 1) * HEAD, h * HEAD:(h + 1) * HEAD]


def _wkv(r, v, kk, lw, kd, b, state_f, state_b, n_ctx_seq=BATCH, t_ctx=SEQ, n_smp_seq=DEC_BATCH, t_smp=DEC_SEQ):
    ntok = n_ctx_seq * t_ctx + n_smp_seq * t_smp
    blk = SUB * CH
    cps_c, cps_s = t_ctx // blk, t_smp // blk
    ncc = n_ctx_seq * cps_c
    nq = ncc + n_smp_seq * cps_s

    def seq_of(q):
        return jnp.where(q < ncc, q // cps_c, n_ctx_seq + (q - ncc) // cps_s)

    def row_block(d, q):
        ctx = q < ncc
        cps = jnp.where(ctx, cps_c, cps_s)
        c = jnp.where(ctx, q % cps_c, (q - ncc) % cps_s)
        return q - c + (c if d == 0 else cps - 1 - c)

    tok = [pl.BlockSpec((blk, D), functools.partial(lambda q, d: (row_block(d, q), 0), d=d)) for d in range(2)]
    tokd = [pl.BlockSpec((blk, D), functools.partial(lambda q, d: (row_block(d, q), d), d=d)) for d in range(2)]
    msk = pl.BlockSpec((2, SR, SR), lambda q: (0, 0, 0))
    tri = pl.BlockSpec((2, CH, CH), lambda q: (0, 0, 0))
    s0spec = pl.BlockSpec((2, 1, NG, GW, GW), lambda q: (0, jnp.maximum(seq_of(q) - n_ctx_seq, 0), 0, 0, 0))
    sspec = pl.BlockSpec((2, 1, NHEAD, HEAD, HEAD),
                         lambda q: (0, jnp.minimum(seq_of(q), n_ctx_seq - 1), 0, 0, 0))

    s0 = jnp.stack([state_f, state_b]).reshape(2, n_smp_seq, NG, GH, HEAD, HEAD)
    eye = jnp.eye(GH, dtype=F32)
    s0 = (s0[:, :, :, :, :, None, :] * eye[None, None, None, :, None, :, None]).reshape(2, n_smp_seq, NG, GW, GW)

    strict, incl, trim = _scan_masks()
    y_f, y_b, s_fin = pl.pallas_call(
        functools.partial(_wkv_kernel, n_ctx_chunks=ncc, ctx_chunks_per_seq=cps_c, smp_chunks_per_seq=cps_s),
        out_shape=[jax.ShapeDtypeStruct((ntok, D), F32), jax.ShapeDtypeStruct((ntok, D), F32),
                   jax.ShapeDtypeStruct((2, n_ctx_seq, NHEAD, HEAD, HEAD), F32)],
        grid=(nq,),
        in_specs=[tok[0], tok[0], tok[0], tokd[0], tokd[0], tokd[0],
                  tok[1], tok[1], tok[1], tokd[1], tokd[1], tokd[1], msk, msk, tri, s0spec],
        out_specs=[tok[0], tok[1], sspec],
        scratch_shapes=[pltpu.VMEM((2, NG, GW, GW), F32)],
        compiler_params=pltpu.CompilerParams(
            dimension_semantics=("arbitrary",), vmem_limit_bytes=VMEM_LIMIT),
        name="wkv_scan",
    )(r, v, kk, lw, kd, b, r, v, kk, lw, kd, b, strict, incl, trim, s0)
    return y_f, y_b, s_fin[0][:, None], s_fin[1][:, None]


RO_TM = 512


def _rwkv_out_kernel(x_ref, yf_ref, yb_ref, r_ref, v_ref, kd_ref, g_ref, mod_ref, gnw_ref, gnb_ref,
                     rk_ref, wo_ref, gpost_ref, ones_ref, o_ref):
    ones = ones_ref[...]
    ys = yf_ref[...] + yb_ref[...]
    mean = _head_sum(ys, ones) * (1.0 / HEAD)
    c = ys - mean
    var = _head_sum(c * c, ones) * (1.0 / HEAD)
    yn = c * lax.rsqrt(var + GN_EPS) * gnw_ref[...] + gnb_ref[...]
    kd = kd_ref[...]
    bonus = _head_sum(r_ref[...] * (kd[:, :D] + kd[:, D:]) * rk_ref[...], ones) * v_ref[...]
    out = _dot((yn + bonus) * g_ref[...], wo_ref[...])
    o_ref[...] = x_ref[...] + mod_ref[0, 5:6, :] * _rms(out, gpost_ref[...])


def _rwkv_out(x, y_f, y_b, r, v, kd, g, mod, p, gpost):
    tm = RO_TM
    row = lambda i: (i, 0)
    full2 = lambda i: (0, 0)
    vec = pl.BlockSpec((1, D), full2)
    tok = pl.BlockSpec((tm, D), row)
    tok2 = pl.BlockSpec((tm, 2 * D), row)
    return pl.pallas_call(
        _rwkv_out_kernel,
        out_shape=jax.ShapeDtypeStruct((NTOK, D), F32),
        grid=(NTOK // tm,),
        in_specs=[tok, tok, tok, tok, tok, tok2, tok,
                  pl.BlockSpec((1, N_MOD, D), lambda i: (_cond_row(i * tm), 0, 0)),
                  vec, vec, vec, pl.BlockSpec((D, D), full2), vec, pl.BlockSpec((HSUM_W, HSUM_W), full2)],
        out_specs=tok,
        compiler_params=pltpu.CompilerParams(
            dimension_semantics=("arbitrary",), vmem_limit_bytes=VMEM_LIMIT),
        name="rwkv_out",
    )(x, y_f, y_b, r, v, kd, g, mod, p['gn_w'].reshape(1, D), p['gn_b'].reshape(1, D),
      p['r_k'].reshape(1, D), p['w_o'].astype(BF16), gpost.reshape(1, D), _head_ones())


def _box_matrix(length, win):
    t = np.arange(length)
    lo = np.clip(t - win // 2, 0, length)
    hi = np.clip(t + win - win // 2, 0, length)
    a = ((t[None, :] >= lo[:, None]) & (t[None, :] < hi[:, None])).astype(np.float32)
    return a, (hi - lo).astype(np.float32)


def _pool_constants(grid):
    mats, inv = [], []
    for win in POOL_WINDOWS:
        if grid:
            ar, cr = _box_matrix(DEC_SEQ // GRID_W, win)
            ac, cc = _box_matrix(GRID_W, win)
            mats.append(np.kron(ar, ac))
            inv.append(1.0 / np.kron(cr, cc))
        else:
            a, c = _box_matrix(SEQ, win)
            mats.append(a)
            inv.append(1.0 / c)
    mats = jnp.asarray(np.stack(mats), dtype=BF16)
    inv = jnp.asarray(np.stack(inv)[:, :, None], dtype=F32)
    return mats, inv


def _pool_kernel(xf_ref, xg_ref, mod_ref, gpre_ref, a_ref, inv_ref, pw_ref, ps_ref, o_ref, *, t):
    for j in range(xf_ref.shape[0] // t):
        rows = slice(j * t, (j + 1) * t)
        xf = xf_ref[rows, :]
        rs = lax.rsqrt(jnp.mean(xf * xf, axis=-1, keepdims=True) + RMS_EPS)
        part = (xg_ref[rows, :] * rs) * gpre_ref[...] * (1.0 + mod_ref[0, 4:5, :]) + mod_ref[0, 3:4, :]
        m = _dot(a_ref[0], part) * inv_ref[0]
        o_ref[rows, :] = _dot(m - part, pw_ref[0]) * ps_ref[...]


def _pool(x, mod, gpre, pool_w, pool_scale, t, seq0, nseq, grid, per_step=1):
    mats, inv = _pool_constants(grid)
    gc = POOL_GC
    tb = t * per_step
    seq0 //= per_step
    return pl.pallas_call(
        functools.partial(_pool_kernel, t=t),
        out_shape=jax.ShapeDtypeStruct((nseq * t, D), F32),
        grid=(4, nseq // per_step),
        in_specs=[
            pl.BlockSpec((tb, D), lambda g, s: (seq0 + s, 0)),
            pl.BlockSpec((tb, gc), lambda g, s: (seq0 + s, g)),
            pl.BlockSpec((1, N_MOD, gc), lambda g, s: (_cond_row((seq0 + s) * tb), 0, g)),
            pl.BlockSpec((1, gc), lambda g, s: (0, g)),
            pl.BlockSpec((1, t, t), lambda g, s: (g, 0, 0)),
            pl.BlockSpec((1, t, 1), lambda g, s: (g, 0, 0)),
            pl.BlockSpec((1, gc, gc), lambda g, s: (g, 0, 0)),
            pl.BlockSpec((1, gc), lambda g, s: (0, g)),
        ],
        out_specs=pl.BlockSpec((tb, gc), lambda g, s: (s, g)),
        compiler_params=pltpu.CompilerParams(
            dimension_semantics=("arbitrary", "arbitrary"), vmem_limit_bytes=VMEM_LIMIT),
        name="pool_grid" if grid else "pool_seq",
    )(x, x, mod, gpre.reshape(1, D), mats, inv, pool_w.astype(BF16), pool_scale.reshape(1, D))


def kernel(x_prompt, x_sample, c, state_ctx_fwd, state_ctx_bwd, c_ctx, w_mod, b_mod, norm_pre, norm_post,
           ffn_w1, ffn_w3, ffn_w2, rwkv_mu, rwkv_w_rkv, rwkv_w0, rwkv_w1, rwkv_w2, rwkv_a0, rwkv_a1,
           rwkv_a2, rwkv_g1, rwkv_g2, rwkv_k_k, rwkv_k_a, rwkv_r_k, rwkv_gn_w, rwkv_gn_b, rwkv_w_o,
           pool_w, pool_scale):
    x = (x_prompt.reshape(N_CTX, D), x_sample.reshape(N_SMP, D))
    cond =jnp.concatenate([c_ctx[None], c, jnp.zeros((COND_ROWS - 1 - DEC_BATCH, D), F32)])
    mod = _modulation(cond, w_mod, b_mod).reshape(DEPTH, COND_ROWS, N_MOD, D)
    wffn = (ffn_w1.astype(BF16), ffn_w3.astype(BF16), ffn_w2.astype(BF16))

    x = _ffn(x, mod[0], norm_pre[0, 0], norm_post[0, 0], wffn, (0, 0), 0)
    p = dict(mu=rwkv_mu[0], w_rkv=rwkv_w_rkv[0], w0=rwkv_w0[0], w1=rwkv_w1[0], w2=rwkv_w2[0],
             a0=rwkv_a0[0], a1=rwkv_a1[0], a2=rwkv_a2[0], g1=rwkv_g1[0], g2=rwkv_g2[0],
             k_k=rwkv_k_k[0], k_a=rwkv_k_a[0], r_k=rwkv_r_k[0], gn_w=rwkv_gn_w[0], gn_b=rwkv_gn_b[0],
             w_o=rwkv_w_o[0])
    r, v, kk, g, lw, kd, b = _rwkv_proj(x, mod[0], norm_pre[0, 1], p)
    y_f, y_b, new_f, new_b = _wkv(r, v, kk, lw, kd, b, state_ctx_fwd[:, 0], state_ctx_bwd[:, 0])
    x = _rwkv_out(x, y_f, y_b, r, v, kd, g, mod[0], p, norm_post[0, 1])
    x = _ffn(x, mod[0], norm_pre[0, 2], norm_post[0, 2], wffn, (0, 1), 6)

    x = _ffn(x, mod[1], norm_pre[1, 0], norm_post[1, 0], wffn, (1, 0), 0)
    pc = _pool(x, mod[1], norm_pre[1, 1], pool_w[0], pool_scale[0], SEQ, 0, BATCH, False, per_step=4)
    ps = _pool(x, mod[1], norm_pre[1, 1], pool_w[0], pool_scale[0], DEC_SEQ, N_CTX // DEC_SEQ, DEC_BATCH, True)
    y_ctx, y_smp = _ffn(x, mod[1], norm_pre[1, 2], norm_post[1, 2], wffn, (1, 1), 6,
                        pending=((pc, ps), norm_post[1, 1], 5), split_out=True)

    return (y_ctx.reshape(BATCH, SEQ, D), y_smp.reshape(DEC_BATCH, DEC_SEQ, D), new_f, new_b)
```

```python
import functools

import numpy as np
import jax
import jax.numpy as jnp
from jax import lax
from jax.experimental import pallas as pl
from jax.experimental.pallas import tpu as pltpu

D = 1024
HEAD = 64
NHEAD = D // HEAD
D_FF = 2816
DEPTH = 2
N_MOD = 9
POOL_WINDOWS = (2, 4, 8, 16)
POOL_GC = D // 4
GRID_W = 64
RMS_EPS = 1e-6
GN_EPS = 64e-5

BATCH, SEQ = 32, 256
DEC_BATCH, DEC_SEQ = 8, 2048
N_CTX = BATCH * SEQ
N_SMP = DEC_BATCH * DEC_SEQ
NTOK = N_CTX + N_SMP
COND_ROWS = 16

VMEM_LIMIT = 56 * 1024 * 1024

F32 = jnp.float32
BF16 = jnp.bfloat16


def _dot(a, b):
    return jnp.dot(a.astype(BF16), b.astype(BF16), preferred_element_type=F32)


def _rms(x, g):
    return (x * lax.rsqrt(jnp.mean(x * x, axis=-1, keepdims=True) + RMS_EPS)) * g


def _cond_row(start_row):
    return jnp.where(start_row < N_CTX, 0, 1 + (start_row - N_CTX) // DEC_SEQ)


def _split(x):
    hi = x.astype(BF16)
    return hi, (x - hi.astype(F32)).astype(BF16)


def _split_sum(x, ones):
    hi, lo = _split(x)
    return (jnp.dot(hi, ones, preferred_element_type=F32)
            + jnp.dot(lo, ones, preferred_element_type=F32))


HSUM_W = 256


def _head_sum(x, ones):
    return jnp.concatenate([_split_sum(x[:, i:i + HSUM_W], ones) for i in range(0, D, HSUM_W)], axis=1)


def _split_sum_left(ones, x):
    hi, lo = _split(x)
    return (jnp.dot(ones, hi, preferred_element_type=F32)
            + jnp.dot(ones, lo, preferred_element_type=F32))


def _mod_kernel(c_ref, w_ref, b_ref, o_ref):
    c = c_ref[...]
    o_ref[0] = _dot(jax.nn.silu(c), w_ref[0]) + b_ref[0]


def _modulation(cond, w_mod, b_mod):
    return pl.pallas_call(
        _mod_kernel,
        out_shape=jax.ShapeDtypeStruct((DEPTH, COND_ROWS, N_MOD * D), F32),
        grid=(DEPTH, N_MOD),
        in_specs=[
            pl.BlockSpec((COND_ROWS, D), lambda l, n: (0, 0)),
            pl.BlockSpec((1, D, D), lambda l, n: (l, 0, n)),
            pl.BlockSpec((1, 1, D), lambda l, n: (l, 0, n)),
        ],
        out_specs=pl.BlockSpec((1, COND_ROWS, D), lambda l, n: (l, 0, n)),
        compiler_params=pltpu.CompilerParams(
            dimension_semantics=("arbitrary", "arbitrary"), vmem_limit_bytes=VMEM_LIMIT),
        name="modulation",
    )(cond, w_mod, b_mod.reshape(DEPTH, 1, N_MOD * D))


FFN_TM = 512
FFN_SPLIT = 2


def _tok_operand(x, tm):
    if not isinstance(x, tuple):
        return [x], [pl.BlockSpec((tm, D), lambda i: (i, 0))]
    na = N_CTX // tm
    return list(x), [pl.BlockSpec((tm, D), lambda i: (jnp.minimum(i, na - 1), 0)),
                     pl.BlockSpec((tm, D), lambda i: (jnp.maximum(i - na, 0), 0))]


def _ffn_kernel(*refs, k0, pending_k, n_x, n_d, n_o, emit_rs):
    refs = list(refs)
    take = lambda n: [refs.pop(0) for _ in range(n)]
    x_refs = take(n_x)
    mod_ref, gpre_ref, gpost_ref, w1_ref, w3_ref, w2_ref = take(6)
    d_refs = take(n_d)
    gd_ref = take(1)[0] if n_d else None
    o_refs = take(n_o)
    rs_ref = take(1)[0] if emit_rs else None
    is_ctx = pl.program_id(0) < N_CTX // FFN_TM

    def read(rs, rows):
        return rs[0][rows, :] if len(rs) == 1 else jnp.where(is_ctx, rs[0][rows, :], rs[1][rows, :])

    rt = FFN_TM // FFN_SPLIT
    for s in range(FFN_SPLIT):
        rows = slice(s * rt, (s + 1) * rt)
        x = read(x_refs, rows)
        if pending_k is not None:
            x = x + mod_ref[0, pending_k:pending_k + 1, :] * _rms(read(d_refs, rows), gd_ref[...])
        h = _rms(x, gpre_ref[...]) * (1.0 + mod_ref[0, k0 + 1:k0 + 2, :]) + mod_ref[0, k0:k0 + 1, :]
        h = h.astype(BF16)
        u = jnp.dot(h, w1_ref[...], preferred_element_type=F32)
        g = jnp.dot(h, w3_ref[...], preferred_element_type=F32)
        a = (jax.nn.silu(u) * g).astype(BF16)
        f = jnp.dot(a, w2_ref[...], preferred_element_type=F32)
        y = x + 0.5 * mod_ref[0, k0 + 2:k0 + 3, :] * _rms(f, gpost_ref[...])
        if emit_rs:
            rs = lax.rsqrt(jnp.mean(y * y, axis=-1, keepdims=True) + RMS_EPS)
            rs_ref[rows, :] = jnp.broadcast_to(rs, (rt, RS_W))
        if n_o == 1:
            o_refs[0][rows, :] = y
        else:
            @pl.when(is_ctx)
            def _():
                o_refs[0][rows, :] = y

            @pl.when(jnp.logical_not(is_ctx))
            def _():
                o_refs[1][rows, :] = y


RS_W = 128


def _ffn(x, mod, gpre, gpost, weights, wsel, k0, pending=None, split_out=False, emit_rs=False):
    tm = FFN_TM
    vec = pl.BlockSpec((1, D), lambda i: (0, 0))
    resident = lambda rows, cols: pl.BlockSpec((None, None, rows, cols), lambda i: (*wsel, 0, 0),
                                               pipeline_mode=pl.Buffered(1))
    x_args, x_specs = _tok_operand(x, tm)
    args = x_args + [mod, gpre.reshape(1, D), gpost.reshape(1, D), *weights]
    in_specs = x_specs + [
        pl.BlockSpec((1, N_MOD, D), lambda i: (_cond_row(i * tm), 0, 0)), vec, vec,
        resident(D, D_FF), resident(D, D_FF), resident(D_FF, D),
    ]
    pending_k, n_d = None, 0
    if pending is not None:
        delta, gd, pending_k = pending
        d_args, d_specs = _tok_operand(delta, tm)
        n_d = len(d_args)
        args += d_args + [gd.reshape(1, D)]
        in_specs += d_specs + [vec]
    if split_out:
        out_shape = [jax.ShapeDtypeStruct((N_CTX, D), F32), jax.ShapeDtypeStruct((N_SMP, D), F32)]
        out_specs = _tok_operand((None, None), tm)[1]
    else:
        out_shape = [jax.ShapeDtypeStruct((NTOK, D), F32)]
        out_specs = _tok_operand(None, tm)[1]
    n_o = len(out_shape)
    if emit_rs:
        out_shape = out_shape + [jax.ShapeDtypeStruct((NTOK, RS_W), F32)]
        out_specs = out_specs + [pl.BlockSpec((tm, RS_W), lambda i: (i, 0))]
    elif n_o == 1:
        out_shape, out_specs = out_shape[0], out_specs[0]
    return pl.pallas_call(
        functools.partial(_ffn_kernel, k0=k0, pending_k=pending_k, n_x=len(x_args), n_d=n_d,
                          n_o=n_o, emit_rs=emit_rs),
        out_shape=out_shape,
        grid=(NTOK // tm,),
        in_specs=in_specs,
        out_specs=out_specs,
        compiler_params=pltpu.CompilerParams(
            dimension_semantics=("arbitrary",), vmem_limit_bytes=VMEM_LIMIT),
        name="ffn",
    )(*args)


RW_TM = 256
HALO = 8


def _rwkv_proj_kernel(x_ref, xp_ref, xn_ref, mod_ref, gpre_ref, mu_ref, wrkv_ref, w1_ref, w2_ref, w0_ref,
                      a1_ref, a2_ref, a0_ref, g1_ref, g2_ref, k_k_ref, k_a_ref, ones_ref,
                      r_out, v_out, kk_out, g_out, lw_out, kd_out, b_out):
    i = pl.program_id(0)
    shift = mod_ref[0, 3:4, :]
    scale = 1.0 + mod_ref[0, 4:5, :]
    gpre = gpre_ref[...]
    premod = lambda x: _rms(x, gpre) * scale + shift

    k = i - N_CTX // RW_TM
    per_seq = DEC_SEQ // RW_TM
    is_ctx = i < N_CTX // RW_TM
    first = jnp.logical_or(is_ctx, k % per_seq == 0)
    last = jnp.logical_or(is_ctx, k % per_seq == per_seq - 1)

    h = premod(x_ref[...])
    hp = jnp.where(first, 0.0, premod(xp_ref[...])[HALO - 1:HALO, :])
    hn = jnp.where(last, 0.0, premod(xn_ref[...])[0:1, :])
    rows = lax.broadcasted_iota(jnp.int32, (RW_TM, 1), 0)
    h_prev = jnp.where(rows == 0, hp, pltpu.roll(h, 1, 0))
    h_next = jnp.where(rows == RW_TM - 1, hn, pltpu.roll(h, RW_TM - 1, 0))
    xx = 0.5 * (h_prev + h_next) - h
    mix = lambda n: h + xx * mu_ref[n:n + 1, :]

    r = _dot(mix(0), wrkv_ref[0])
    kx = _dot(mix(2), wrkv_ref[1])
    v = _dot(mix(3), wrkv_ref[2])
    g = _dot(jax.nn.sigmoid(_dot(mix(5), g1_ref[...])), g2_ref[...])

    z = w0_ref[...] + _dot(jnp.tanh(_dot(mix(1), w1_ref[...])), w2_ref[...])
    a = jax.nn.sigmoid(a0_ref[...] + _dot(_dot(mix(4), a1_ref[...]), a2_ref[...]))

    kkraw = kx * k_k_ref[...]
    nrm = jnp.sqrt(_head_sum(kkraw * kkraw, ones_ref[...]))
    kk = kkraw / jnp.maximum(nrm, 1e-12)

    r_out[...] = r.astype(BF16)
    v_out[...] = v.astype(BF16)
    kk_out[...] = kk.astype(BF16)
    g_out[...] = g.astype(BF16)
    lw_out[...] = -float(np.exp(-0.5)) * jax.nn.sigmoid(z)
    ka = k_a_ref[...]
    for d in range(2):
        a_d = a[:, d * D:(d + 1) * D]
        kd_out[:, d * D:(d + 1) * D] = (kx * (1.0 + (a_d - 1.0) * ka)).astype(BF16)
        b_out[:, d * D:(d + 1) * D] = (kk * a_d).astype(BF16)


def _blockdiag2(m):
    z = jnp.zeros_like(m[0])
    return jnp.concatenate([jnp.concatenate([m[0], z], axis=1), jnp.concatenate([z, m[1]], axis=1)], axis=0)


def _head_ones():
    idx = np.arange(HSUM_W) // HEAD
    return jnp.asarray((idx[:, None] == idx[None, :]).astype(np.float32), dtype=BF16)


def _rwkv_proj(x, mod, gpre, p):
    tm = RW_TM
    nblk = NTOK // HALO
    row = lambda i: (i, 0)
    full2 = lambda i: (0, 0)
    full3 = lambda i: (0, 0, 0)
    w1c = jnp.concatenate([p['w1'][0], p['w1'][1]], axis=1).astype(BF16)
    a1c = jnp.concatenate([p['a1'][0], p['a1'][1]], axis=1).astype(BF16)
    w2bd = _blockdiag2(p['w2']).astype(BF16)
    a2bd = _blockdiag2(p['a2']).astype(BF16)
    return pl.pallas_call(
        _rwkv_proj_kernel,
        out_shape=([jax.ShapeDtypeStruct((NTOK, D), BF16)] * 4 + [jax.ShapeDtypeStruct((NTOK, 2 * D), F32)]
                   + [jax.ShapeDtypeStruct((NTOK, 2 * D), BF16)] * 2),
        grid=(NTOK // tm,),
        in_specs=[
            pl.BlockSpec((tm, D), row),
            pl.BlockSpec((HALO, D), lambda i: (jnp.maximum(i * (tm // HALO) - 1, 0), 0)),
            pl.BlockSpec((HALO, D), lambda i: (jnp.minimum((i + 1) * (tm // HALO), nblk - 1), 0)),
            pl.BlockSpec((1, N_MOD, D), lambda i: (_cond_row(i * tm), 0, 0)),
            pl.BlockSpec((1, D), full2),
            pl.BlockSpec((6, D), full2),
            pl.BlockSpec((3, D, D), full3),
            pl.BlockSpec((D, 128), full2),
            pl.BlockSpec((128, 2 * D), full2),
            pl.BlockSpec((1, 2 * D), full2),
            pl.BlockSpec((D, 128), full2),
            pl.BlockSpec((128, 2 * D), full2),
            pl.BlockSpec((1, 2 * D), full2),
            pl.BlockSpec((D, 128), full2),
            pl.BlockSpec((128, D), full2),
            pl.BlockSpec((1, D), full2),
            pl.BlockSpec((1, D), full2),
            pl.BlockSpec((HSUM_W, HSUM_W), full2),
        ],
        out_specs=[pl.BlockSpec((tm, D), row)] * 4 + [pl.BlockSpec((tm, 2 * D), row)] * 3,
        compiler_params=pltpu.CompilerParams(
            dimension_semantics=("arbitrary",), vmem_limit_bytes=VMEM_LIMIT),
        name="rwkv_proj",
    )(x, x, x, mod, gpre.reshape(1, D), p['mu'], p['w_rkv'].astype(BF16), w1c, w2bd,
      p['w0'].reshape(1, 2 * D), a1c, a2bd, p['a0'].reshape(1, 2 * D),
      p['g1'].astype(BF16), p['g2'].astype(BF16), p['k_k'].reshape(1, D), p['k_a'].reshape(1, D),
      _head_ones())


CH = 32
SUB = 4
GH = 4
GW = GH * HEAD
NG = NHEAD // GH
SR = GH * CH

_NT = (((1,), (1,)), ((), ()))
_TN = (((0,), (0,)), ((), ()))


def _scan_masks():
    rho = np.arange(SR)
    same = (rho[:, None] // CH) == (rho[None, :] // CH)
    lower = rho[None, :] < rho[:, None]
    strict = np.stack([same & lower, same & lower.T])
    incl = strict | np.eye(SR, dtype=bool)[None]
    t = np.arange(CH)
    tri = np.stack([t[None, :] <= t[:, None], t[None, :] >= t[:, None]])
    return (jnp.asarray(strict.astype(np.float32)), jnp.asarray(incl.astype(np.float32)),
            jnp.asarray(tri.astype(np.float32), dtype=BF16))


def _wkv_kernel(rf_ref, vf_ref, kkf_ref, lwf_ref, kdf_ref, bf_ref, rb_ref, vb_ref, kkb_ref, lwb_ref, kdb_ref,
                bb_ref, strict_ref, incl_ref, tri_ref, s0_ref, yf_ref, yb_ref, sfin_ref, s_ref,
                *, n_ctx_chunks, ctx_chunks_per_seq, smp_chunks_per_seq):
    q = pl.program_id(0)
    is_ctx = q < n_ctx_chunks
    c = jnp.where(is_ctx, q % ctx_chunks_per_seq, (q - n_ctx_chunks) % smp_chunks_per_seq)

    @pl.when(c == 0)
    def _():
        s_ref[...] = jnp.where(is_ctx, 0.0, s0_ref[:, 0])

    def decayed(d, a, r_ref, v_ref, kk_ref, lw_ref, kd_ref, b_ref):
        pos = a if d == 0 else SUB - 1 - a
        rows = slice(pos * CH, (pos + 1) * CH)
        lw = lw_ref[rows, :]
        cum = _split_sum_left(tri_ref[d], lw)
        tot = cum[CH - 1:CH, :] if d == 0 else cum[0:1, :]
        e_neg = jnp.exp(-cum)
        e_rem = jnp.exp(tot - cum)
        kd = kd_ref[rows, :]
        bb = b_ref[rows, :]
        return (kk_ref[rows, :] * jnp.exp(cum - lw), r_ref[rows, :] * jnp.exp(cum), kd * e_neg, bb * e_neg,
                v_ref[rows, :], kd * e_rem, bb * e_rem), jnp.exp(tot), rows

    refs_d = ((rf_ref, vf_ref, kkf_ref, lwf_ref, kdf_ref, bf_ref),
              (rb_ref, vb_ref, kkb_ref, lwb_ref, kdb_ref, bb_ref))
    tiles = {(a, d): decayed(d, a, *refs_d[d]) for a in range(SUB) for d in range(2)}
    y_refs = (yf_ref, yb_ref)
    strict_d = [strict_ref[d] > 0.5 for d in range(2)]
    incl_d = [incl_ref[d] > 0.5 for d in range(2)]

    lane_head = lax.broadcasted_iota(jnp.int32, (CH, GW), 1) // HEAD
    head_lanes = [lane_head == h for h in range(GH)]

    def stack(x):
        return jnp.concatenate([jnp.where(m, x, 0.0) for m in head_lanes], axis=0).astype(BF16)

    eye = (lax.broadcasted_iota(jnp.int32, (SR, SR), 0)
           == lax.broadcasted_iota(jnp.int32, (SR, SR), 1)).astype(F32)
    dot = functools.partial(jnp.dot, preferred_element_type=F32)
    dotg = functools.partial(lax.dot_general, preferred_element_type=F32)

    dg = [(d, g) for g in range(NG) for d in range(2)]
    sl = {g: slice(g * GW, (g + 1) * GW) for g in range(NG)}
    pre = {}
    state = {c_: s_ref[c_[0], c_[1]] for c_ in dg}

    def independent(chunks):
        adg = [(a, d, g) for a in chunks for d, g in dg]
        ops = {(a, d, g): [stack(x[:, sl[g]]) for x in tiles[(a, d)][0]] for a, d, g in adg}
        qr = {c_: jnp.concatenate([o[0], o[1]], axis=0) for c_, o in ops.items()}
        gm = {c_: dotg(qr[c_], jnp.concatenate([o[3], o[2]], axis=0), _NT) for c_, o in ops.items()}
        yield
        l = {c_: jnp.where(strict_d[c_[1]], gm[c_][:SR, :SR], 0.0) for c_ in adg}
        m = {c_: jnp.where(strict_d[c_[1]], gm[c_][:SR, SR:], 0.0).astype(BF16) for c_ in adg}
        ppb = {c_: jnp.concatenate([jnp.where(incl_d[c_[1]], gm[c_][SR:, SR:], 0.0),
                                    jnp.where(incl_d[c_[1]], gm[c_][SR:, :SR], 0.0)], axis=1).astype(BF16)
               for c_ in adg}
        mv = {c_: dot(m[c_], ops[c_][4]) for c_ in adg}
        yield
        n = {c_: eye - l[c_] for c_ in adg}
        pw = {c_: l[c_].astype(BF16) for c_ in adg}
        for _ in range(CH.bit_length() - 2):
            pw = {c_: dot(pw[c_], pw[c_]).astype(BF16) for c_ in adg}
            yield
            n = {c_: n[c_] + dot(n[c_].astype(BF16), pw[c_]) for c_ in adg}
            yield
        for a in chunks:
            of = lambda t_: {c_: t_[(a,) + c_] for c_ in dg}
            pre[a] = dict(qr=of(qr), mv=of(mv), ppb=of(ppb), tinv={c_: n[(a,) + c_].astype(BF16) for c_ in dg},
                          vs={c_: ops[(a,) + c_][4] for c_ in dg},
                          khbh={c_: jnp.concatenate([ops[(a,) + c_][5], ops[(a,) + c_][6]], axis=0) for c_ in dg})

    def dependent(a):
        p_ = pre[a]
        s0 = dict(state)
        zz = {c_: dotg(p_['qr'][c_], s0[c_].astype(BF16), _NT) for c_ in dg}
        yield
        u = {c_: dot(p_['tinv'][c_], (zz[c_][:SR] + p_['mv'][c_]).astype(BF16)) for c_ in dg}
        yield
        vu = {c_: jnp.concatenate([p_['vs'][c_], (-u[c_]).astype(BF16)], axis=0) for c_ in dg}
        ys = {c_: zz[c_][SR:] + dot(p_['ppb'][c_], vu[c_]) for c_ in dg}
        yield
        for d, g in dg:
            state[(d, g)] = s0[(d, g)] * tiles[(a, d)][1][:, sl[g]] + dotg(vu[(d, g)], p_['khbh'][(d, g)], _TN)
        for d, g in dg:
            y = ys[(d, g)][0:CH]
            for h in range(1, GH):
                y = y + ys[(d, g)][h * CH:(h + 1) * CH]
            y_refs[d][tiles[(a, d)][2], sl[g]] = y.astype(BF16)

    def run(*gens):
        gens = list(gens)
        while gens:
            for gen in list(gens):
                if next(gen, gens) is gens:
                    gens.remove(gen)

    run(independent(tuple(range(SUB))))
    for a in range(SUB):
        run(dependent(a))
    for (d, g), s in state.items():
        s_ref[d, g] = s

    @pl.when(jnp.logical_and(is_ctx, c == ctx_chunks_per_seq - 1))
    def _():
        for (d, g), s in state.items():
            for h in range(GH):
                sfin_ref[d, 0, g * GH + h] = s[h * HEAD:(h + 1) * HEAD, h * HEAD:(h + 1) * HEAD]


def _wkv(r, v, kk, lw, kd, b, state_f, state_b, n_ctx_seq=BATCH, t_ctx=SEQ, n_smp_seq=DEC_BATCH, t_smp=DEC_SEQ):
    ntok = n_ctx_seq * t_ctx + n_smp_seq * t_smp
    blk = SUB * CH
    cps_c, cps_s = t_ctx // blk, t_smp // blk
    ncc = n_ctx_seq * cps_c
    nq = ncc + n_smp_seq * cps_s

    def seq_of(q):
        return jnp.where(q < ncc, q // cps_c, n_ctx_seq + (q - ncc) // cps_s)

    def row_block(d, q):
        ctx = q < ncc
        cps = jnp.where(ctx, cps_c, cps_s)
        c = jnp.where(ctx, q % cps_c, (q - ncc) % cps_s)
        return q - c + (c if d == 0 else cps - 1 - c)

    tok = [pl.BlockSpec((blk, D), functools.partial(lambda q, d: (row_block(d, q), 0), d=d)) for d in range(2)]
    tokd = [pl.BlockSpec((blk, D), functools.partial(lambda q, d: (row_block(d, q), d), d=d)) for d in range(2)]
    msk = pl.BlockSpec((2, SR, SR), lambda q: (0, 0, 0))
    tri = pl.BlockSpec((2, CH, CH), lambda q: (0, 0, 0))
    s0spec = pl.BlockSpec((2, 1, NG, GW, GW), lambda q: (0, jnp.maximum(seq_of(q) - n_ctx_seq, 0), 0, 0, 0))
    sspec = pl.BlockSpec((2, 1, NHEAD, HEAD, HEAD),
                         lambda q: (0, jnp.minimum(seq_of(q), n_ctx_seq - 1), 0, 0, 0))

    s0 = jnp.stack([state_f, state_b]).reshape(2, n_smp_seq, NG, GH, HEAD, HEAD)
    eye = jnp.eye(GH, dtype=F32)
    s0 = (s0[:, :, :, :, :, None, :] * eye[None, None, None, :, None, :, None]).reshape(2, n_smp_seq, NG, GW, GW)

    strict, incl, trim = _scan_masks()
    y_f, y_b, s_fin = pl.pallas_call(
        functools.partial(_wkv_kernel, n_ctx_chunks=ncc, ctx_chunks_per_seq=cps_c, smp_chunks_per_seq=cps_s),
        out_shape=[jax.ShapeDtypeStruct((ntok, D), BF16), jax.ShapeDtypeStruct((ntok, D), BF16),
                   jax.ShapeDtypeStruct((2, n_ctx_seq, NHEAD, HEAD, HEAD), F32)],
        grid=(nq,),
        in_specs=[tok[0], tok[0], tok[0], tokd[0], tokd[0], tokd[0],
                  tok[1], tok[1], tok[1], tokd[1], tokd[1], tokd[1], msk, msk, tri, s0spec],
        out_specs=[tok[0], tok[1], sspec],
        scratch_shapes=[pltpu.VMEM((2, NG, GW, GW), F32)],
        compiler_params=pltpu.CompilerParams(
            dimension_semantics=("arbitrary",), vmem_limit_bytes=VMEM_LIMIT),
        name="wkv_scan",
    )(r, v, kk, lw, kd, b, r, v, kk, lw, kd, b, strict, incl, trim, s0)
    return y_f, y_b, s_fin[0][:, None], s_fin[1][:, None]


RO_TM = 512


def _rwkv_out_kernel(x_ref, yf_ref, yb_ref, r_ref, v_ref, kd_ref, g_ref, mod_ref, gnw_ref, gnb_ref,
                     rk_ref, wo_ref, gpost_ref, ones_ref, o_ref):
    ones = ones_ref[...]
    ys = yf_ref[...] + yb_ref[...]
    mean = _head_sum(ys, ones) * (1.0 / HEAD)
    c = ys - mean
    var = _head_sum(c * c, ones) * (1.0 / HEAD)
    yn = c * lax.rsqrt(var + GN_EPS) * gnw_ref[...] + gnb_ref[...]
    kd = kd_ref[...]
    bonus = _head_sum(r_ref[...] * (kd[:, :D] + kd[:, D:]) * rk_ref[...], ones) * v_ref[...]
    out = _dot((yn + bonus) * g_ref[...], wo_ref[...])
    o_ref[...] = x_ref[...] + mod_ref[0, 5:6, :] * _rms(out, gpost_ref[...])


def _rwkv_out(x, y_f, y_b, r, v, kd, g, mod, p, gpost):
    tm = RO_TM
    row = lambda i: (i, 0)
    full2 = lambda i: (0, 0)
    vec = pl.BlockSpec((1, D), full2)
    tok = pl.BlockSpec((tm, D), row)
    tok2 = pl.BlockSpec((tm, 2 * D), row)
    return pl.pallas_call(
        _rwkv_out_kernel,
        out_shape=jax.ShapeDtypeStruct((NTOK, D), F32),
        grid=(NTOK // tm,),
        in_specs=[tok, tok, tok, tok, tok, tok2, tok,
                  pl.BlockSpec((1, N_MOD, D), lambda i: (_cond_row(i * tm), 0, 0)),
                  vec, vec, vec, pl.BlockSpec((D, D), full2), vec, pl.BlockSpec((HSUM_W, HSUM_W), full2)],
        out_specs=tok,
        compiler_params=pltpu.CompilerParams(
            dimension_semantics=("arbitrary",), vmem_limit_bytes=VMEM_LIMIT),
        name="rwkv_out",
    )(x, y_f, y_b, r, v, kd, g, mod, p['gn_w'].reshape(1, D), p['gn_b'].reshape(1, D),
      p['r_k'].reshape(1, D), p['w_o'].astype(BF16), gpost.reshape(1, D), _head_ones())


def _box_matrix(length, win):
    t = np.arange(length)
    lo = np.clip(t - win // 2, 0, length)
    hi = np.clip(t + win - win // 2, 0, length)
    a = ((t[None, :] >= lo[:, None]) & (t[None, :] < hi[:, None])).astype(np.float32)
    return a, (hi - lo).astype(np.float32)


def _pool_constants(grid):
    mats, inv = [], []
    for win in POOL_WINDOWS:
        if grid:
            ar, cr = _box_matrix(DEC_SEQ // GRID_W, win)
            ac, cc = _box_matrix(GRID_W, win)
            mats.append(np.kron(ar, ac))
            inv.append(1.0 / np.kron(cr, cc))
        else:
            a, c = _box_matrix(SEQ, win)
            mats.append(a)
            inv.append(1.0 / c)
    mats = jnp.asarray(np.stack(mats).astype(BF16))
    inv = jnp.asarray(np.stack(inv)[:, :, None], dtype=F32)
    return mats, inv


def _pool_kernel(rs_ref, xg_ref, mod_ref, gpre_ref, a_ref, inv_ref, pw_ref, ps_ref, o_ref, *, t):
    for j in range(xg_ref.shape[0] // t):
        rows = slice(j * t, (j + 1) * t)
        part = (xg_ref[rows, :] * rs_ref[rows, 0:1]) * gpre_ref[...] * (1.0 + mod_ref[0, 4:5, :]) + mod_ref[0, 3:4, :]
        m = _dot(a_ref[0], part) * inv_ref[0]
        o_ref[rows, :] = _dot(m - part, pw_ref[0]) * ps_ref[...]


def _pool(x, rs, mod, gpre, pool_w, pool_scale, t, seq0, nseq, grid, per_step=1):
    mats, inv = _pool_constants(grid)
    gc = POOL_GC
    tb = t * per_step
    seq0 //= per_step
    return pl.pallas_call(
        functools.partial(_pool_kernel, t=t),
        out_shape=jax.ShapeDtypeStruct((nseq * t, D), F32),
        grid=(4, nseq // per_step),
        in_specs=[
            pl.BlockSpec((tb, RS_W), lambda g, s: (seq0 + s, 0)),
            pl.BlockSpec((tb, gc), lambda g, s: (seq0 + s, g)),
            pl.BlockSpec((1, N_MOD, gc), lambda g, s: (_cond_row((seq0 + s) * tb), 0, g)),
            pl.BlockSpec((1, gc), lambda g, s: (0, g)),
            pl.BlockSpec((1, t, t), lambda g, s: (g, 0, 0)),
            pl.BlockSpec((1, t, 1), lambda g, s: (g, 0, 0)),
            pl.BlockSpec((1, gc, gc), lambda g, s: (g, 0, 0)),
            pl.BlockSpec((1, gc), lambda g, s: (0, g)),
        ],
        out_specs=pl.BlockSpec((tb, gc), lambda g, s: (s, g)),
        compiler_params=pltpu.CompilerParams(
            dimension_semantics=("arbitrary", "arbitrary"), vmem_limit_bytes=VMEM_LIMIT),
        name="pool_grid" if grid else "pool_seq",
    )(rs, x, mod, gpre.reshape(1, D), mats, inv, pool_w.astype(BF16), pool_scale.reshape(1, D))


def kernel(x_prompt, x_sample, c, state_ctx_fwd, state_ctx_bwd, c_ctx, w_mod, b_mod, norm_pre, norm_post,
           ffn_w1, ffn_w3, ffn_w2, rwkv_mu, rwkv_w_rkv, rwkv_w0, rwkv_w1, rwkv_w2, rwkv_a0, rwkv_a1,
           rwkv_a2, rwkv_g1, rwkv_g2, rwkv_k_k, rwkv_k_a, rwkv_r_k, rwkv_gn_w, rwkv_gn_b, rwkv_w_o,
           pool_w, pool_scale):
    x = (x_prompt.reshape(N_CTX, D), x_sample.reshape(N_SMP, D))
    cond =jnp.concatenate([c_ctx[None], c, jnp.zeros((COND_ROWS - 1 - DEC_BATCH, D), F32)])
    mod = _modulation(cond, w_mod, b_mod).reshape(DEPTH, COND_ROWS, N_MOD, D)
    wffn = (ffn_w1.astype(BF16), ffn_w3.astype(BF16), ffn_w2.astype(BF16))

    x = _ffn(x, mod[0], norm_pre[0, 0], norm_post[0, 0], wffn, (0, 0), 0)
    p = dict(mu=rwkv_mu[0], w_rkv=rwkv_w_rkv[0], w0=rwkv_w0[0], w1=rwkv_w1[0], w2=rwkv_w2[0],
             a0=rwkv_a0[0], a1=rwkv_a1[0], a2=rwkv_a2[0], g1=rwkv_g1[0], g2=rwkv_g2[0],
             k_k=rwkv_k_k[0], k_a=rwkv_k_a[0], r_k=rwkv_r_k[0], gn_w=rwkv_gn_w[0], gn_b=rwkv_gn_b[0],
             w_o=rwkv_w_o[0])
    r, v, kk, g, lw, kd, b = _rwkv_proj(x, mod[0], norm_pre[0, 1], p)
    y_f, y_b, new_f, new_b = _wkv(r, v, kk, lw, kd, b, state_ctx_fwd[:, 0], state_ctx_bwd[:, 0])
    x = _rwkv_out(x, y_f, y_b, r, v, kd, g, mod[0], p, norm_post[0, 1])
    x = _ffn(x, mod[0], norm_pre[0, 2], norm_post[0, 2], wffn, (0, 1), 6)

    x, rs = _ffn(x, mod[1], norm_pre[1, 0], norm_post[1, 0], wffn, (1, 0), 0, emit_rs=True)
    pc = _pool(x, rs, mod[1], norm_pre[1, 1], pool_w[0], pool_scale[0], SEQ, 0, BATCH, False, per_step=4)
    ps = _pool(x, rs, mod[1], norm_pre[1, 1], pool_w[0], pool_scale[0], DEC_SEQ, N_CTX // DEC_SEQ, DEC_BATCH, True)
    y_ctx, y_smp = _ffn(x, mod[1], norm_pre[1, 2], norm_post[1, 2], wffn, (1, 1), 6,
                        pending=((pc, ps), norm_post[1, 1], 5), split_out=True)

    return (y_ctx.reshape(BATCH, SEQ, D), y_smp.reshape(DEC_BATCH, DEC_SEQ, D), new_f, new_b)
```

```python
import functools

import numpy as np
import jax
import jax.numpy as jnp
from jax import lax
from jax.experimental import pallas as pl
from jax.experimental.pallas import tpu as pltpu

D = 1024
HEAD = 64
NHEAD = D // HEAD
D_FF = 2816
DEPTH = 2
N_MOD = 9
POOL_WINDOWS = (2, 4, 8, 16)
POOL_GC = D // 4
GRID_W = 64
RMS_EPS = 1e-6
GN_EPS = 64e-5

BATCH, SEQ = 32, 256
DEC_BATCH, DEC_SEQ = 8, 2048
N_CTX = BATCH * SEQ
N_SMP = DEC_BATCH * DEC_SEQ
NTOK = N_CTX + N_SMP
COND_ROWS = 16

VMEM_LIMIT = 56 * 1024 * 1024

F32 = jnp.float32
BF16 = jnp.bfloat16


def _dot(a, b):
    return jnp.dot(a.astype(BF16), b.astype(BF16), preferred_element_type=F32)


def _rms(x, g):
    return (x * lax.rsqrt(jnp.mean(x * x, axis=-1, keepdims=True) + RMS_EPS)) * g


def _cond_row(start_row):
    return jnp.where(start_row < N_CTX, 0, 1 + (start_row - N_CTX) // DEC_SEQ)


def _split(x):
    hi = x.astype(BF16)
    return hi, (x - hi.astype(F32)).astype(BF16)


def _split_sum(x, ones):
    hi, lo = _split(x)
    return (jnp.dot(hi, ones, preferred_element_type=F32)
            + jnp.dot(lo, ones, preferred_element_type=F32))


HSUM_W = 256


def _head_sum(x, ones):
    return jnp.concatenate([_split_sum(x[:, i:i + HSUM_W], ones) for i in range(0, D, HSUM_W)], axis=1)


def _split_sum_left(ones, x):
    hi, lo = _split(x)
    return (jnp.dot(ones, hi, preferred_element_type=F32)
            + jnp.dot(ones, lo, preferred_element_type=F32))


def _mod_kernel(c_ref, w_ref, b_ref, o_ref):
    c = c_ref[...]
    o_ref[0] = _dot(jax.nn.silu(c), w_ref[0]) + b_ref[0]


def _modulation(cond, w_mod, b_mod):
    return pl.pallas_call(
        _mod_kernel,
        out_shape=jax.ShapeDtypeStruct((DEPTH, COND_ROWS, N_MOD * D), F32),
        grid=(DEPTH, N_MOD),
        in_specs=[
            pl.BlockSpec((COND_ROWS, D), lambda l, n: (0, 0)),
            pl.BlockSpec((1, D, D), lambda l, n: (l, 0, n)),
            pl.BlockSpec((1, 1, D), lambda l, n: (l, 0, n)),
        ],
        out_specs=pl.BlockSpec((1, COND_ROWS, D), lambda l, n: (l, 0, n)),
        compiler_params=pltpu.CompilerParams(
            dimension_semantics=("arbitrary", "arbitrary"), vmem_limit_bytes=VMEM_LIMIT),
        name="modulation",
    )(cond, w_mod, b_mod.reshape(DEPTH, 1, N_MOD * D))


FFN_ROWS = 256
FFN_TM = 1024
FFN_TM_PENDING = 512


def _tok_operand(x, tm):
    if not isinstance(x, tuple):
        return [x], [pl.BlockSpec((tm, D), lambda i: (i, 0))]
    na = N_CTX // tm
    return list(x), [pl.BlockSpec((tm, D), lambda i: (jnp.minimum(i, na - 1), 0)),
                     pl.BlockSpec((tm, D), lambda i: (jnp.maximum(i - na, 0), 0))]


def _ffn_kernel(*refs, tm, k0, pending_k, n_x, n_d, n_o, emit_rs):
    refs = list(refs)
    take = lambda n: [refs.pop(0) for _ in range(n)]
    x_refs = take(n_x)
    mod_ref, gpre_ref, gpost_ref, w1_ref, w3_ref, w2_ref = take(6)
    d_refs = take(n_d)
    gd_ref = take(1)[0] if n_d else None
    o_refs = take(n_o)
    rs_ref = take(1)[0] if emit_rs else None
    is_ctx = pl.program_id(0) < N_CTX // tm

    def read(rs, rows):
        return rs[0][rows, :] if len(rs) == 1 else jnp.where(is_ctx, rs[0][rows, :], rs[1][rows, :])

    rt = FFN_ROWS
    for s in range(tm // rt):
        rows = slice(s * rt, (s + 1) * rt)
        x = read(x_refs, rows)
        if pending_k is not None:
            x = x + mod_ref[0, pending_k:pending_k + 1, :] * _rms(read(d_refs, rows), gd_ref[...])
        h = _rms(x, gpre_ref[...]) * (1.0 + mod_ref[0, k0 + 1:k0 + 2, :]) + mod_ref[0, k0:k0 + 1, :]
        h = h.astype(BF16)
        u = jnp.dot(h, w1_ref[...], preferred_element_type=F32)
        g = jnp.dot(h, w3_ref[...], preferred_element_type=F32)
        a = (jax.nn.silu(u) * g).astype(BF16)
        f = jnp.dot(a, w2_ref[...], preferred_element_type=F32)
        y = x + 0.5 * mod_ref[0, k0 + 2:k0 + 3, :] * _rms(f, gpost_ref[...])
        if emit_rs:
            rs = lax.rsqrt(jnp.mean(y * y, axis=-1, keepdims=True) + RMS_EPS)
            rs_ref[rows, :] = jnp.broadcast_to(rs, (rt, RS_W))
        if n_o == 1:
            o_refs[0][rows, :] = y
        else:
            @pl.when(is_ctx)
            def _():
                o_refs[0][rows, :] = y

            @pl.when(jnp.logical_not(is_ctx))
            def _():
                o_refs[1][rows, :] = y


RS_W = 128


def _ffn(x, mod, gpre, gpost, weights, wsel, k0, pending=None, split_out=False, emit_rs=False):
    tm = FFN_TM if pending is None else FFN_TM_PENDING
    vec = pl.BlockSpec((1, D), lambda i: (0, 0))
    resident = lambda rows, cols: pl.BlockSpec((None, None, rows, cols), lambda i: (*wsel, 0, 0),
                                               pipeline_mode=pl.Buffered(1))
    x_args, x_specs = _tok_operand(x, tm)
    args = x_args + [mod, gpre.reshape(1, D), gpost.reshape(1, D), *weights]
    in_specs = x_specs + [
        pl.BlockSpec((1, N_MOD, D), lambda i: (_cond_row(i * tm), 0, 0)), vec, vec,
        resident(D, D_FF), resident(D, D_FF), resident(D_FF, D),
    ]
    pending_k, n_d = None, 0
    if pending is not None:
        delta, gd, pending_k = pending
        d_args, d_specs = _tok_operand(delta, tm)
        n_d = len(d_args)
        args += d_args + [gd.reshape(1, D)]
        in_specs += d_specs + [vec]
    if split_out:
        out_shape = [jax.ShapeDtypeStruct((N_CTX, D), F32), jax.ShapeDtypeStruct((N_SMP, D), F32)]
        out_specs = _tok_operand((None, None), tm)[1]
    else:
        out_shape = [jax.ShapeDtypeStruct((NTOK, D), F32)]
        out_specs = _tok_operand(None, tm)[1]
    n_o = len(out_shape)
    if emit_rs:
        out_shape = out_shape + [jax.ShapeDtypeStruct((NTOK, RS_W), F32)]
        out_specs = out_specs + [pl.BlockSpec((tm, RS_W), lambda i: (i, 0))]
    elif n_o == 1:
        out_shape, out_specs = out_shape[0], out_specs[0]
    return pl.pallas_call(
        functools.partial(_ffn_kernel, tm=tm, k0=k0, pending_k=pending_k, n_x=len(x_args), n_d=n_d,
                          n_o=n_o, emit_rs=emit_rs),
        out_shape=out_shape,
        grid=(NTOK // tm,),
        in_specs=in_specs,
        out_specs=out_specs,
        compiler_params=pltpu.CompilerParams(
            dimension_semantics=("arbitrary",), vmem_limit_bytes=VMEM_LIMIT),
        name="ffn",
    )(*args)


RW_TM = 256
HALO = 8


def _rwkv_proj_kernel(x_ref, xp_ref, xn_ref, mod_ref, gpre_ref, mu_ref, wrkv_ref, w1_ref, w2_ref, w0_ref,
                      a1_ref, a2_ref, a0_ref, g1_ref, g2_ref, k_k_ref, k_a_ref, ones_ref,
                      r_out, v_out, kk_out, g_out, lw_out, kd_out, b_out):
    i = pl.program_id(0)
    shift = mod_ref[0, 3:4, :]
    scale = 1.0 + mod_ref[0, 4:5, :]
    gpre = gpre_ref[...]
    premod = lambda x: _rms(x, gpre) * scale + shift

    k = i - N_CTX // RW_TM
    per_seq = DEC_SEQ // RW_TM
    is_ctx = i < N_CTX // RW_TM
    first = jnp.logical_or(is_ctx, k % per_seq == 0)
    last = jnp.logical_or(is_ctx, k % per_seq == per_seq - 1)

    h = premod(x_ref[...])
    hp = jnp.where(first, 0.0, premod(xp_ref[...])[HALO - 1:HALO, :])
    hn = jnp.where(last, 0.0, premod(xn_ref[...])[0:1, :])
    rows = lax.broadcasted_iota(jnp.int32, (RW_TM, 1), 0)
    h_prev = jnp.where(rows == 0, hp, pltpu.roll(h, 1, 0))
    h_next = jnp.where(rows == RW_TM - 1, hn, pltpu.roll(h, RW_TM - 1, 0))
    xx = 0.5 * (h_prev + h_next) - h
    mix = lambda n: h + xx * mu_ref[n:n + 1, :]

    r = _dot(mix(0), wrkv_ref[0])
    kx = _dot(mix(2), wrkv_ref[1])
    v = _dot(mix(3), wrkv_ref[2])
    g = _dot(jax.nn.sigmoid(_dot(mix(5), g1_ref[...])), g2_ref[...])

    z = w0_ref[...] + _dot(jnp.tanh(_dot(mix(1), w1_ref[...])), w2_ref[...])
    a = jax.nn.sigmoid(a0_ref[...] + _dot(_dot(mix(4), a1_ref[...]), a2_ref[...]))

    kkraw = kx * k_k_ref[...]
    nrm = jnp.sqrt(_head_sum(kkraw * kkraw, ones_ref[...]))
    kk = kkraw / jnp.maximum(nrm, 1e-12)

    r_out[...] = r.astype(BF16)
    v_out[...] = v.astype(BF16)
    kk_out[...] = kk.astype(BF16)
    g_out[...] = g.astype(BF16)
    lw_out[...] = -float(np.exp(-0.5)) * jax.nn.sigmoid(z)
    ka = k_a_ref[...]
    for d in range(2):
        a_d = a[:, d * D:(d + 1) * D]
        kd_out[:, d * D:(d + 1) * D] = (kx * (1.0 + (a_d - 1.0) * ka)).astype(BF16)
        b_out[:, d * D:(d + 1) * D] = (kk * a_d).astype(BF16)


def _blockdiag2(m):
    z = jnp.zeros_like(m[0])
    return jnp.concatenate([jnp.concatenate([m[0], z], axis=1), jnp.concatenate([z, m[1]], axis=1)], axis=0)


def _head_ones():
    idx = np.arange(HSUM_W) // HEAD
    return jnp.asarray((idx[:, None] == idx[None, :]).astype(np.float32), dtype=BF16)


def _rwkv_proj(x, mod, gpre, p):
    tm = RW_TM
    nblk = NTOK // HALO
    row = lambda i: (i, 0)
    full2 = lambda i: (0, 0)
    full3 = lambda i: (0, 0, 0)
    w1c = jnp.concatenate([p['w1'][0], p['w1'][1]], axis=1).astype(BF16)
    a1c = jnp.concatenate([p['a1'][0], p['a1'][1]], axis=1).astype(BF16)
    w2bd = _blockdiag2(p['w2']).astype(BF16)
    a2bd = _blockdiag2(p['a2']).astype(BF16)
    return pl.pallas_call(
        _rwkv_proj_kernel,
        out_shape=([jax.ShapeDtypeStruct((NTOK, D), BF16)] * 4 + [jax.ShapeDtypeStruct((NTOK, 2 * D), F32)]
                   + [jax.ShapeDtypeStruct((NTOK, 2 * D), BF16)] * 2),
        grid=(NTOK // tm,),
        in_specs=[
            pl.BlockSpec((tm, D), row),
            pl.BlockSpec((HALO, D), lambda i: (jnp.maximum(i * (tm // HALO) - 1, 0), 0)),
            pl.BlockSpec((HALO, D), lambda i: (jnp.minimum((i + 1) * (tm // HALO), nblk - 1), 0)),
            pl.BlockSpec((1, N_MOD, D), lambda i: (_cond_row(i * tm), 0, 0)),
            pl.BlockSpec((1, D), full2),
            pl.BlockSpec((6, D), full2),
            pl.BlockSpec((3, D, D), full3),
            pl.BlockSpec((D, 128), full2),
            pl.BlockSpec((128, 2 * D), full2),
            pl.BlockSpec((1, 2 * D), full2),
            pl.BlockSpec((D, 128), full2),
            pl.BlockSpec((128, 2 * D), full2),
            pl.BlockSpec((1, 2 * D), full2),
            pl.BlockSpec((D, 128), full2),
            pl.BlockSpec((128, D), full2),
            pl.BlockSpec((1, D), full2),
            pl.BlockSpec((1, D), full2),
            pl.BlockSpec((HSUM_W, HSUM_W), full2),
        ],
        out_specs=[pl.BlockSpec((tm, D), row)] * 4 + [pl.BlockSpec((tm, 2 * D), row)] * 3,
        compiler_params=pltpu.CompilerParams(
            dimension_semantics=("arbitrary",), vmem_limit_bytes=VMEM_LIMIT),
        name="rwkv_proj",
    )(x, x, x, mod, gpre.reshape(1, D), p['mu'], p['w_rkv'].astype(BF16), w1c, w2bd,
      p['w0'].reshape(1, 2 * D), a1c, a2bd, p['a0'].reshape(1, 2 * D),
      p['g1'].astype(BF16), p['g2'].astype(BF16), p['k_k'].reshape(1, D), p['k_a'].reshape(1, D),
      _head_ones())


CH = 32
SUB = 4
GH = 4
GW = GH * HEAD
NG = NHEAD // GH
SR = GH * CH

_NT = (((1,), (1,)), ((), ()))
_TN = (((0,), (0,)), ((), ()))


def _scan_masks():
    rho = np.arange(SR)
    same = (rho[:, None] // CH) == (rho[None, :] // CH)
    lower = rho[None, :] < rho[:, None]
    strict = np.stack([same & lower, same & lower.T])
    incl = strict | np.eye(SR, dtype=bool)[None]
    t = np.arange(CH)
    tri = np.stack([t[None, :] <= t[:, None], t[None, :] >= t[:, None]])
    return (jnp.asarray(strict.astype(np.float32)), jnp.asarray(incl.astype(np.float32)),
            jnp.asarray(tri.astype(np.float32), dtype=BF16))


def _wkv_kernel(rf_ref, vf_ref, kkf_ref, lwf_ref, kdf_ref, bf_ref, rb_ref, vb_ref, kkb_ref, lwb_ref, kdb_ref,
                bb_ref, strict_ref, incl_ref, tri_ref, s0f_ref, s0b_ref, yf_ref, yb_ref, sfinf_ref, sfinb_ref,
                s_ref,
                *, n_ctx_chunks, ctx_chunks_per_seq, smp_chunks_per_seq):
    q = pl.program_id(0)
    is_ctx = q < n_ctx_chunks
    c = jnp.where(is_ctx, q % ctx_chunks_per_seq, (q - n_ctx_chunks) % smp_chunks_per_seq)

    @pl.when(c == 0)
    def _():
        zero = jnp.zeros((HEAD, HEAD), F32)
        for d, s0_ref in enumerate((s0f_ref, s0b_ref)):
            for g in range(NG):
                bd = jnp.concatenate(
                    [jnp.concatenate([s0_ref[0, g * GH + h] if hh == h else zero for hh in range(GH)], axis=1)
                     for h in range(GH)], axis=0)
                s_ref[d, g] = jnp.where(is_ctx, 0.0, bd)

    def decayed(d, a, r_ref, v_ref, kk_ref, lw_ref, kd_ref, b_ref):
        pos = a if d == 0 else SUB - 1 - a
        rows = slice(pos * CH, (pos + 1) * CH)
        lw = lw_ref[rows, :]
        cum = _split_sum_left(tri_ref[d], lw)
        tot = cum[CH - 1:CH, :] if d == 0 else cum[0:1, :]
        e_neg = jnp.exp(-cum)
        e_rem = jnp.exp(tot - cum)
        kd = kd_ref[rows, :]
        bb = b_ref[rows, :]
        return (kk_ref[rows, :] * jnp.exp(cum - lw), r_ref[rows, :] * jnp.exp(cum), kd * e_neg, bb * e_neg,
                v_ref[rows, :], kd * e_rem, bb * e_rem), jnp.exp(tot), rows

    refs_d = ((rf_ref, vf_ref, kkf_ref, lwf_ref, kdf_ref, bf_ref),
              (rb_ref, vb_ref, kkb_ref, lwb_ref, kdb_ref, bb_ref))
    tiles = {(a, d): decayed(d, a, *refs_d[d]) for a in range(SUB) for d in range(2)}
    y_refs = (yf_ref, yb_ref)
    strict_d = [strict_ref[d] > 0.5 for d in range(2)]
    incl_d = [incl_ref[d] > 0.5 for d in range(2)]

    lane_head = lax.broadcasted_iota(jnp.int32, (CH, GW), 1) // HEAD
    head_lanes = [lane_head == h for h in range(GH)]

    def stack(x):
        return jnp.concatenate([jnp.where(m, x, 0.0) for m in head_lanes], axis=0).astype(BF16)

    eye = (lax.broadcasted_iota(jnp.int32, (SR, SR), 0)
           == lax.broadcasted_iota(jnp.int32, (SR, SR), 1)).astype(F32)
    dot = functools.partial(jnp.dot, preferred_element_type=F32)
    dotg = functools.partial(lax.dot_general, preferred_element_type=F32)

    dg = [(d, g) for g in range(NG) for d in range(2)]
    sl = {g: slice(g * GW, (g + 1) * GW) for g in range(NG)}
    pre = {}
    state = {c_: s_ref[c_[0], c_[1]] for c_ in dg}

    def independent(chunks):
        adg = [(a, d, g) for a in chunks for d, g in dg]
        ops = {(a, d, g): [stack(x[:, sl[g]]) for x in tiles[(a, d)][0]] for a, d, g in adg}
        qr = {c_: jnp.concatenate([o[0], o[1]], axis=0) for c_, o in ops.items()}
        gm = {c_: dotg(qr[c_], jnp.concatenate([o[3], o[2]], axis=0), _NT) for c_, o in ops.items()}
        yield
        l = {c_: jnp.where(strict_d[c_[1]], gm[c_][:SR, :SR], 0.0) for c_ in adg}
        m = {c_: jnp.where(strict_d[c_[1]], gm[c_][:SR, SR:], 0.0).astype(BF16) for c_ in adg}
        ppb = {c_: jnp.concatenate([jnp.where(incl_d[c_[1]], gm[c_][SR:, SR:], 0.0),
                                    jnp.where(incl_d[c_[1]], gm[c_][SR:, :SR], 0.0)], axis=1).astype(BF16)
               for c_ in adg}
        mv = {c_: dot(m[c_], ops[c_][4]) for c_ in adg}
        yield
        n = {c_: eye - l[c_] for c_ in adg}
        pw = {c_: l[c_].astype(BF16) for c_ in adg}
        for _ in range(CH.bit_length() - 2):
            pw = {c_: dot(pw[c_], pw[c_]).astype(BF16) for c_ in adg}
            yield
            n = {c_: n[c_] + dot(n[c_].astype(BF16), pw[c_]) for c_ in adg}
            yield
        for a in chunks:
            of = lambda t_: {c_: t_[(a,) + c_] for c_ in dg}
            pre[a] = dict(qr=of(qr), mv=of(mv), ppb=of(ppb), tinv={c_: n[(a,) + c_].astype(BF16) for c_ in dg},
                          vs={c_: ops[(a,) + c_][4] for c_ in dg},
                          khbh={c_: jnp.concatenate([ops[(a,) + c_][5], ops[(a,) + c_][6]], axis=0) for c_ in dg})

    def dependent(a):
        p_ = pre[a]
        s0 = dict(state)
        zz = {c_: dotg(p_['qr'][c_], s0[c_].astype(BF16), _NT) for c_ in dg}
        yield
        u = {c_: dot(p_['tinv'][c_], (zz[c_][:SR] + p_['mv'][c_]).astype(BF16)) for c_ in dg}
        yield
        vu = {c_: jnp.concatenate([p_['vs'][c_], (-u[c_]).astype(BF16)], axis=0) for c_ in dg}
        ys = {c_: zz[c_][SR:] + dot(p_['ppb'][c_], vu[c_]) for c_ in dg}
        yield
        for d, g in dg:
            state[(d, g)] = s0[(d, g)] * tiles[(a, d)][1][:, sl[g]] + dotg(vu[(d, g)], p_['khbh'][(d, g)], _TN)
        for d, g in dg:
            y = ys[(d, g)][0:CH]
            for h in range(1, GH):
                y = y + ys[(d, g)][h * CH:(h + 1) * CH]
            y_refs[d][tiles[(a, d)][2], sl[g]] = y.astype(BF16)

    def run(*gens):
        gens = list(gens)
        while gens:
            for gen in list(gens):
                if next(gen, gens) is gens:
                    gens.remove(gen)

    run(independent(tuple(range(SUB))))
    for a in range(SUB):
        run(dependent(a))
    for (d, g), s in state.items():
        s_ref[d, g] = s

    @pl.when(jnp.logical_and(is_ctx, c == ctx_chunks_per_seq - 1))
    def _():
        for (d, g), s in state.items():
            for h in range(GH):
                (sfinf_ref, sfinb_ref)[d][0, 0, g * GH + h] = s[h * HEAD:(h + 1) * HEAD, h * HEAD:(h + 1) * HEAD]


def _wkv(r, v, kk, lw, kd, b, state_f, state_b, n_ctx_seq=BATCH, t_ctx=SEQ, n_smp_seq=DEC_BATCH, t_smp=DEC_SEQ):
    ntok = n_ctx_seq * t_ctx + n_smp_seq * t_smp
    blk = SUB * CH
    cps_c, cps_s = t_ctx // blk, t_smp // blk
    ncc = n_ctx_seq * cps_c
    nq = ncc + n_smp_seq * cps_s

    def seq_of(q):
        return jnp.where(q < ncc, q // cps_c, n_ctx_seq + (q - ncc) // cps_s)

    def row_block(d, q):
        ctx = q < ncc
        cps = jnp.where(ctx, cps_c, cps_s)
        c = jnp.where(ctx, q % cps_c, (q - ncc) % cps_s)
        return q - c + (c if d == 0 else cps - 1 - c)

    tok = [pl.BlockSpec((blk, D), functools.partial(lambda q, d: (row_block(d, q), 0), d=d)) for d in range(2)]
    tokd = [pl.BlockSpec((blk, D), functools.partial(lambda q, d: (row_block(d, q), d), d=d)) for d in range(2)]
    msk = pl.BlockSpec((2, SR, SR), lambda q: (0, 0, 0))
    tri = pl.BlockSpec((2, CH, CH), lambda q: (0, 0, 0))
    s0spec = pl.BlockSpec((1, NHEAD, HEAD, HEAD), lambda q: (jnp.maximum(seq_of(q) - n_ctx_seq, 0), 0, 0, 0))
    sspec = pl.BlockSpec((1, 1, NHEAD, HEAD, HEAD),
                         lambda q: (jnp.minimum(seq_of(q), n_ctx_seq - 1), 0, 0, 0, 0))
    sfin = jax.ShapeDtypeStruct((n_ctx_seq, 1, NHEAD, HEAD, HEAD), F32)

    strict, incl, trim = _scan_masks()
    return pl.pallas_call(
        functools.partial(_wkv_kernel, n_ctx_chunks=ncc, ctx_chunks_per_seq=cps_c, smp_chunks_per_seq=cps_s),
        out_shape=[jax.ShapeDtypeStruct((ntok, D), BF16), jax.ShapeDtypeStruct((ntok, D), BF16), sfin, sfin],
        grid=(nq,),
        in_specs=[tok[0], tok[0], tok[0], tokd[0], tokd[0], tokd[0],
                  tok[1], tok[1], tok[1], tokd[1], tokd[1], tokd[1], msk, msk, tri, s0spec, s0spec],
        out_specs=[tok[0], tok[1], sspec, sspec],
        scratch_shapes=[pltpu.VMEM((2, NG, GW, GW), F32)],
        compiler_params=pltpu.CompilerParams(
            dimension_semantics=("arbitrary",), vmem_limit_bytes=VMEM_LIMIT),
        name="wkv_scan",
    )(r, v, kk, lw, kd, b, r, v, kk, lw, kd, b, strict, incl, trim, state_f, state_b)


RO_TM = 512


def _rwkv_out_kernel(x_ref, yf_ref, yb_ref, r_ref, v_ref, kd_ref, g_ref, mod_ref, gnw_ref, gnb_ref,
                     rk_ref, wo_ref, gpost_ref, ones_ref, o_ref):
    ones = ones_ref[...]
    ys = yf_ref[...] + yb_ref[...]
    mean = _head_sum(ys, ones) * (1.0 / HEAD)
    c = ys - mean
    var = _head_sum(c * c, ones) * (1.0 / HEAD)
    yn = c * lax.rsqrt(var + GN_EPS) * gnw_ref[...] + gnb_ref[...]
    kd = kd_ref[...]
    bonus = _head_sum(r_ref[...] * (kd[:, :D] + kd[:, D:]) * rk_ref[...], ones) * v_ref[...]
    out = _dot((yn + bonus) * g_ref[...], wo_ref[...])
    o_ref[...] = x_ref[...] + mod_ref[0, 5:6, :] * _rms(out, gpost_ref[...])


def _rwkv_out(x, y_f, y_b, r, v, kd, g, mod, p, gpost):
    tm = RO_TM
    row = lambda i: (i, 0)
    full2 = lambda i: (0, 0)
    vec = pl.BlockSpec((1, D), full2)
    tok = pl.BlockSpec((tm, D), row)
    tok2 = pl.BlockSpec((tm, 2 * D), row)
    return pl.pallas_call(
        _rwkv_out_kernel,
        out_shape=jax.ShapeDtypeStruct((NTOK, D), F32),
        grid=(NTOK // tm,),
        in_specs=[tok, tok, tok, tok, tok, tok2, tok,
                  pl.BlockSpec((1, N_MOD, D), lambda i: (_cond_row(i * tm), 0, 0)),
                  vec, vec, vec, pl.BlockSpec((D, D), full2), vec, pl.BlockSpec((HSUM_W, HSUM_W), full2)],
        out_specs=tok,
        compiler_params=pltpu.CompilerParams(
            dimension_semantics=("arbitrary",), vmem_limit_bytes=VMEM_LIMIT),
        name="rwkv_out",
    )(x, y_f, y_b, r, v, kd, g, mod, p['gn_w'].reshape(1, D), p['gn_b'].reshape(1, D),
      p['r_k'].reshape(1, D), p['w_o'].astype(BF16), gpost.reshape(1, D), _head_ones())


def _box_matrix(length, win):
    t = np.arange(length)
    lo = np.clip(t - win // 2, 0, length)
    hi = np.clip(t + win - win // 2, 0, length)
    a = ((t[None, :] >= lo[:, None]) & (t[None, :] < hi[:, None])).astype(np.float32)
    return a, (hi - lo).astype(np.float32)


def _pool_constants(grid):
    mats, inv = [], []
    for win in POOL_WINDOWS:
        if grid:
            ar, cr = _box_matrix(DEC_SEQ // GRID_W, win)
            ac, cc = _box_matrix(GRID_W, win)
            mats.append(np.kron(ar, ac))
            inv.append(1.0 / np.kron(cr, cc))
        else:
            a, c = _box_matrix(SEQ, win)
            mats.append(a)
            inv.append(1.0 / c)
    mats = jnp.asarray(np.stack(mats).astype(BF16))
    inv = jnp.asarray(np.stack(inv)[:, :, None], dtype=F32)
    return mats, inv


def _pool_kernel(rs_ref, xg_ref, mod_ref, gpre_ref, a_ref, inv_ref, pw_ref, ps_ref, o_ref, *, t):
    for j in range(xg_ref.shape[0] // t):
        rows = slice(j * t, (j + 1) * t)
        part = (xg_ref[rows, :] * rs_ref[rows, 0:1]) * gpre_ref[...] * (1.0 + mod_ref[0, 4:5, :]) + mod_ref[0, 3:4, :]
        m = _dot(a_ref[0], part) * inv_ref[0]
        o_ref[rows, :] = _dot(m - part, pw_ref[0]) * ps_ref[...]


def _pool(x, rs, mod, gpre, pool_w, pool_scale, t, seq0, nseq, grid, per_step=1):
    mats, inv = _pool_constants(grid)
    gc = POOL_GC
    tb = t * per_step
    seq0 //= per_step
    return pl.pallas_call(
        functools.partial(_pool_kernel, t=t),
        out_shape=jax.ShapeDtypeStruct((nseq * t, D), F32),
        grid=(4, nseq // per_step),
        in_specs=[
            pl.BlockSpec((tb, RS_W), lambda g, s: (seq0 + s, 0)),
            pl.BlockSpec((tb, gc), lambda g, s: (seq0 + s, g)),
            pl.BlockSpec((1, N_MOD, gc), lambda g, s: (_cond_row((seq0 + s) * tb), 0, g)),
            pl.BlockSpec((1, gc), lambda g, s: (0, g)),
            pl.BlockSpec((1, t, t), lambda g, s: (g, 0, 0)),
            pl.BlockSpec((1, t, 1), lambda g, s: (g, 0, 0)),
            pl.BlockSpec((1, gc, gc), lambda g, s: (g, 0, 0)),
            pl.BlockSpec((1, gc), lambda g, s: (0, g)),
        ],
        out_specs=pl.BlockSpec((tb, gc), lambda g, s: (s, g)),
        compiler_params=pltpu.CompilerParams(
            dimension_semantics=("arbitrary", "arbitrary"), vmem_limit_bytes=VMEM_LIMIT),
        name="pool_grid" if grid else "pool_seq",
    )(rs, x, mod, gpre.reshape(1, D), mats, inv, pool_w.astype(BF16), pool_scale.reshape(1, D))


def kernel(x_prompt, x_sample, c, state_ctx_fwd, state_ctx_bwd, c_ctx, w_mod, b_mod, norm_pre, norm_post,
           ffn_w1, ffn_w3, ffn_w2, rwkv_mu, rwkv_w_rkv, rwkv_w0, rwkv_w1, rwkv_w2, rwkv_a0, rwkv_a1,
           rwkv_a2, rwkv_g1, rwkv_g2, rwkv_k_k, rwkv_k_a, rwkv_r_k, rwkv_gn_w, rwkv_gn_b, rwkv_w_o,
           pool_w, pool_scale):
    x = (x_prompt.reshape(N_CTX, D), x_sample.reshape(N_SMP, D))
    cond =jnp.concatenate([c_ctx[None], c, jnp.zeros((COND_ROWS - 1 - DEC_BATCH, D), F32)])
    mod = _modulation(cond, w_mod, b_mod).reshape(DEPTH, COND_ROWS, N_MOD, D)
    wffn = (ffn_w1.astype(BF16), ffn_w3.astype(BF16), ffn_w2.astype(BF16))

    x = _ffn(x, mod[0], norm_pre[0, 0], norm_post[0, 0], wffn, (0, 0), 0)
    p = dict(mu=rwkv_mu[0], w_rkv=rwkv_w_rkv[0], w0=rwkv_w0[0], w1=rwkv_w1[0], w2=rwkv_w2[0],
             a0=rwkv_a0[0], a1=rwkv_a1[0], a2=rwkv_a2[0], g1=rwkv_g1[0], g2=rwkv_g2[0],
             k_k=rwkv_k_k[0], k_a=rwkv_k_a[0], r_k=rwkv_r_k[0], gn_w=rwkv_gn_w[0], gn_b=rwkv_gn_b[0],
             w_o=rwkv_w_o[0])
    r, v, kk, g, lw, kd, b = _rwkv_proj(x, mod[0], norm_pre[0, 1], p)
    y_f, y_b, new_f, new_b = _wkv(r, v, kk, lw, kd, b, state_ctx_fwd[:, 0], state_ctx_bwd[:, 0])
    x = _rwkv_out(x, y_f, y_b, r, v, kd, g, mod[0], p, norm_post[0, 1])
    x = _ffn(x, mod[0], norm_pre[0, 2], norm_post[0, 2], wffn, (0, 1), 6)

    x, rs = _ffn(x, mod[1], norm_pre[1, 0], norm_post[1, 0], wffn, (1, 0), 0, emit_rs=True)
    pc = _pool(x, rs, mod[1], norm_pre[1, 1], pool_w[0], pool_scale[0], SEQ, 0, BATCH, False, per_step=4)
    ps = _pool(x, rs, mod[1], norm_pre[1, 1], pool_w[0], pool_scale[0], DEC_SEQ, N_CTX // DEC_SEQ, DEC_BATCH, True)
    y_ctx, y_smp = _ffn(x, mod[1], norm_pre[1, 2], norm_post[1, 2], wffn, (1, 1), 6,
                        pending=((pc, ps), norm_post[1, 1], 5), split_out=True)

    return (y_ctx.reshape(BATCH, SEQ, D), y_smp.reshape(DEC_BATCH, DEC_SEQ, D), new_f, new_b)
```

```python
import functools

import numpy as np
import jax
import jax.numpy as jnp
from jax import lax
from jax.experimental import pallas as pl
from jax.experimental.pallas import tpu as pltpu

D = 1024
HEAD = 64
NHEAD = D // HEAD
D_FF = 2816
DEPTH = 2
N_MOD = 9
POOL_WINDOWS = (2, 4, 8, 16)
POOL_GC = D // 4
GRID_W = 64
RMS_EPS = 1e-6
GN_EPS = 64e-5

BATCH, SEQ = 32, 256
DEC_BATCH, DEC_SEQ = 8, 2048
N_CTX = BATCH * SEQ
N_SMP = DEC_BATCH * DEC_SEQ
NTOK = N_CTX + N_SMP
COND_ROWS = 16

VMEM_LIMIT = 56 * 1024 * 1024

F32 = jnp.float32
BF16 = jnp.bfloat16


def _dot(a, b):
    return jnp.dot(a.astype(BF16), b.astype(BF16), preferred_element_type=F32)


def _rms(x, g):
    return (x * lax.rsqrt(jnp.mean(x * x, axis=-1, keepdims=True) + RMS_EPS)) * g


def _cond_row(start_row):
    return jnp.where(start_row < N_CTX, 0, 1 + (start_row - N_CTX) // DEC_SEQ)


def _split(x):
    hi = x.astype(BF16)
    return hi, (x - hi.astype(F32)).astype(BF16)


def _split_sum(x, ones):
    hi, lo = _split(x)
    return (jnp.dot(hi, ones, preferred_element_type=F32)
            + jnp.dot(lo, ones, preferred_element_type=F32))


HSUM_W = 256


def _head_sum(x, ones):
    return jnp.concatenate([_split_sum(x[:, i:i + HSUM_W], ones) for i in range(0, D, HSUM_W)], axis=1)


def _split_sum_left(ones, x):
    hi, lo = _split(x)
    return (jnp.dot(ones, hi, preferred_element_type=F32)
            + jnp.dot(ones, lo, preferred_element_type=F32))


def _mod_kernel(c_ref, w_ref, b_ref, o_ref):
    c = c_ref[...]
    o_ref[0] = _dot(jax.nn.silu(c), w_ref[0]) + b_ref[0]


def _modulation(cond, w_mod, b_mod):
    return pl.pallas_call(
        _mod_kernel,
        out_shape=jax.ShapeDtypeStruct((DEPTH, COND_ROWS, N_MOD * D), F32),
        grid=(DEPTH, N_MOD),
        in_specs=[
            pl.BlockSpec((COND_ROWS, D), lambda l, n: (0, 0)),
            pl.BlockSpec((1, D, D), lambda l, n: (l, 0, n)),
            pl.BlockSpec((1, 1, D), lambda l, n: (l, 0, n)),
        ],
        out_specs=pl.BlockSpec((1, COND_ROWS, D), lambda l, n: (l, 0, n)),
        compiler_params=pltpu.CompilerParams(
            dimension_semantics=("arbitrary", "arbitrary"), vmem_limit_bytes=VMEM_LIMIT),
        name="modulation",
    )(cond, w_mod, b_mod.reshape(DEPTH, 1, N_MOD * D))


FFN_ROWS = 256
FFN_TM = 1024
FFN_TM_PENDING = 512


def _tok_operand(x, tm):
    if not isinstance(x, tuple):
        return [x], [pl.BlockSpec((tm, D), lambda i: (i, 0))]
    na = N_CTX // tm
    return list(x), [pl.BlockSpec((tm, D), lambda i: (jnp.minimum(i, na - 1), 0)),
                     pl.BlockSpec((tm, D), lambda i: (jnp.maximum(i - na, 0), 0))]


def _ffn_kernel(*refs, tm, k0, pending_k, n_x, n_d, n_o, emit_rs):
    refs = list(refs)
    take = lambda n: [refs.pop(0) for _ in range(n)]
    x_refs = take(n_x)
    mod_ref, gpre_ref, gpost_ref, w1_ref, w3_ref, w2_ref = take(6)
    d_refs = take(n_d)
    gd_ref = take(1)[0] if n_d else None
    o_refs = take(n_o)
    rs_ref = take(1)[0] if emit_rs else None
    is_ctx = pl.program_id(0) < N_CTX // tm

    def read(rs, rows):
        return rs[0][rows, :] if len(rs) == 1 else jnp.where(is_ctx, rs[0][rows, :], rs[1][rows, :])

    rt = FFN_ROWS
    for s in range(tm // rt):
        rows = slice(s * rt, (s + 1) * rt)
        x = read(x_refs, rows)
        if pending_k is not None:
            x = x + mod_ref[0, pending_k:pending_k + 1, :] * _rms(read(d_refs, rows), gd_ref[...])
        h = _rms(x, gpre_ref[...]) * (1.0 + mod_ref[0, k0 + 1:k0 + 2, :]) + mod_ref[0, k0:k0 + 1, :]
        h = h.astype(BF16)
        u = jnp.dot(h, w1_ref[...], preferred_element_type=F32)
        g = jnp.dot(h, w3_ref[...], preferred_element_type=F32)
        a = (jax.nn.silu(u) * g).astype(BF16)
        f = jnp.dot(a, w2_ref[...], preferred_element_type=F32)
        y = x + 0.5 * mod_ref[0, k0 + 2:k0 + 3, :] * _rms(f, gpost_ref[...])
        if emit_rs:
            rs = lax.rsqrt(jnp.mean(y * y, axis=-1, keepdims=True) + RMS_EPS)
            rs_ref[rows, :] = jnp.broadcast_to(rs, (rt, RS_W))
        if n_o == 1:
            o_refs[0][rows, :] = y
        else:
            @pl.when(is_ctx)
            def _():
                o_refs[0][rows, :] = y

            @pl.when(jnp.logical_not(is_ctx))
            def _():
                o_refs[1][rows, :] = y


RS_W = 128


def _ffn(x, mod, gpre, gpost, weights, wsel, k0, pending=None, split_out=False, emit_rs=False):
    tm = FFN_TM if pending is None else FFN_TM_PENDING
    vec = pl.BlockSpec((1, D), lambda i: (0, 0))
    resident = lambda rows, cols: pl.BlockSpec((None, None, rows, cols), lambda i: (*wsel, 0, 0),
                                               pipeline_mode=pl.Buffered(1))
    x_args, x_specs = _tok_operand(x, tm)
    args = x_args + [mod, gpre.reshape(1, D), gpost.reshape(1, D), *weights]
    in_specs = x_specs + [
        pl.BlockSpec((1, N_MOD, D), lambda i: (_cond_row(i * tm), 0, 0)), vec, vec,
        resident(D, D_FF), resident(D, D_FF), resident(D_FF, D),
    ]
    pending_k, n_d = None, 0
    if pending is not None:
        delta, gd, pending_k = pending
        d_args, d_specs = _tok_operand(delta, tm)
        n_d = len(d_args)
        args += d_args + [gd.reshape(1, D)]
        in_specs += d_specs + [vec]
    if split_out:
        out_shape = [jax.ShapeDtypeStruct((N_CTX, D), F32), jax.ShapeDtypeStruct((N_SMP, D), F32)]
        out_specs = _tok_operand((None, None), tm)[1]
    else:
        out_shape = [jax.ShapeDtypeStruct((NTOK, D), F32)]
        out_specs = _tok_operand(None, tm)[1]
    n_o = len(out_shape)
    if emit_rs:
        out_shape = out_shape + [jax.ShapeDtypeStruct((NTOK, RS_W), F32)]
        out_specs = out_specs + [pl.BlockSpec((tm, RS_W), lambda i: (i, 0))]
    elif n_o == 1:
        out_shape, out_specs = out_shape[0], out_specs[0]
    return pl.pallas_call(
        functools.partial(_ffn_kernel, tm=tm, k0=k0, pending_k=pending_k, n_x=len(x_args), n_d=n_d,
                          n_o=n_o, emit_rs=emit_rs),
        out_shape=out_shape,
        grid=(NTOK // tm,),
        in_specs=in_specs,
        out_specs=out_specs,
        compiler_params=pltpu.CompilerParams(
            dimension_semantics=("arbitrary",), vmem_limit_bytes=VMEM_LIMIT),
        name="ffn",
    )(*args)


RW_TM = 256
HALO = 8


def _rwkv_proj_kernel(x_ref, xp_ref, xn_ref, mod_ref, gpre_ref, mu_ref, wrkv_ref, w1_ref, w2_ref, w0_ref,
                      a1_ref, a2_ref, a0_ref, g1_ref, g2_ref, k_k_ref, k_a_ref, ones_ref,
                      r_out, v_out, kk_out, g_out, lw_out, kd_out, b_out):
    i = pl.program_id(0)
    shift = mod_ref[0, 3:4, :]
    scale = 1.0 + mod_ref[0, 4:5, :]
    gpre = gpre_ref[...]
    premod = lambda x: _rms(x, gpre) * scale + shift

    k = i - N_CTX // RW_TM
    per_seq = DEC_SEQ // RW_TM
    is_ctx = i < N_CTX // RW_TM
    first = jnp.logical_or(is_ctx, k % per_seq == 0)
    last = jnp.logical_or(is_ctx, k % per_seq == per_seq - 1)

    h = premod(x_ref[...])
    hp = jnp.where(first, 0.0, premod(xp_ref[...])[HALO - 1:HALO, :])
    hn = jnp.where(last, 0.0, premod(xn_ref[...])[0:1, :])
    rows = lax.broadcasted_iota(jnp.int32, (RW_TM, 1), 0)
    h_prev = jnp.where(rows == 0, hp, pltpu.roll(h, 1, 0))
    h_next = jnp.where(rows == RW_TM - 1, hn, pltpu.roll(h, RW_TM - 1, 0))
    xx = 0.5 * (h_prev + h_next) - h
    mix = lambda n: h + xx * mu_ref[n:n + 1, :]

    r = _dot(mix(0), wrkv_ref[0])
    kx = _dot(mix(2), wrkv_ref[1])
    v = _dot(mix(3), wrkv_ref[2])
    g = _dot(jax.nn.sigmoid(_dot(mix(5), g1_ref[...])), g2_ref[...])

    z = w0_ref[...] + _dot(jnp.tanh(_dot(mix(1), w1_ref[...])), w2_ref[...])
    a = jax.nn.sigmoid(a0_ref[...] + _dot(_dot(mix(4), a1_ref[...]), a2_ref[...]))

    kkraw = kx * k_k_ref[...]
    nrm = jnp.sqrt(_head_sum(kkraw * kkraw, ones_ref[...]))
    kk = kkraw / jnp.maximum(nrm, 1e-12)

    r_out[...] = r.astype(BF16)
    v_out[...] = v.astype(BF16)
    kk_out[...] = kk.astype(BF16)
    g_out[...] = g.astype(BF16)
    lw_out[...] = -float(np.exp(-0.5)) * jax.nn.sigmoid(z)
    ka = k_a_ref[...]
    for d in range(2):
        a_d = a[:, d * D:(d + 1) * D]
        kd_out[:, d * D:(d + 1) * D] = (kx * (1.0 + (a_d - 1.0) * ka)).astype(BF16)
        b_out[:, d * D:(d + 1) * D] = (kk * a_d).astype(BF16)


def _blockdiag2(m):
    z = jnp.zeros_like(m[0])
    return jnp.concatenate([jnp.concatenate([m[0], z], axis=1), jnp.concatenate([z, m[1]], axis=1)], axis=0)


def _head_ones():
    idx = np.arange(HSUM_W) // HEAD
    return jnp.asarray((idx[:, None] == idx[None, :]).astype(np.float32), dtype=BF16)


def _rwkv_proj(x, mod, gpre, p):
    tm = RW_TM
    nblk = NTOK // HALO
    row = lambda i: (i, 0)
    full2 = lambda i: (0, 0)
    full3 = lambda i: (0, 0, 0)
    w1c = jnp.concatenate([p['w1'][0], p['w1'][1]], axis=1).astype(BF16)
    a1c = jnp.concatenate([p['a1'][0], p['a1'][1]], axis=1).astype(BF16)
    w2bd = _blockdiag2(p['w2']).astype(BF16)
    a2bd = _blockdiag2(p['a2']).astype(BF16)
    return pl.pallas_call(
        _rwkv_proj_kernel,
        out_shape=([jax.ShapeDtypeStruct((NTOK, D), BF16)] * 4 + [jax.ShapeDtypeStruct((NTOK, 2 * D), F32)]
                   + [jax.ShapeDtypeStruct((NTOK, 2 * D), BF16)] * 2),
        grid=(NTOK // tm,),
        in_specs=[
            pl.BlockSpec((tm, D), row),
            pl.BlockSpec((HALO, D), lambda i: (jnp.maximum(i * (tm // HALO) - 1, 0), 0)),
            pl.BlockSpec((HALO, D), lambda i: (jnp.minimum((i + 1) * (tm // HALO), nblk - 1), 0)),
            pl.BlockSpec((1, N_MOD, D), lambda i: (_cond_row(i * tm), 0, 0)),
            pl.BlockSpec((1, D), full2),
            pl.BlockSpec((6, D), full2),
            pl.BlockSpec((3, D, D), full3),
            pl.BlockSpec((D, 128), full2),
            pl.BlockSpec((128, 2 * D), full2),
            pl.BlockSpec((1, 2 * D), full2),
            pl.BlockSpec((D, 128), full2),
            pl.BlockSpec((128, 2 * D), full2),
            pl.BlockSpec((1, 2 * D), full2),
            pl.BlockSpec((D, 128), full2),
            pl.BlockSpec((128, D), full2),
            pl.BlockSpec((1, D), full2),
            pl.BlockSpec((1, D), full2),
            pl.BlockSpec((HSUM_W, HSUM_W), full2),
        ],
        out_specs=[pl.BlockSpec((tm, D), row)] * 4 + [pl.BlockSpec((tm, 2 * D), row)] * 3,
        compiler_params=pltpu.CompilerParams(
            dimension_semantics=("arbitrary",), vmem_limit_bytes=VMEM_LIMIT),
        name="rwkv_proj",
    )(x, x, x, mod, gpre.reshape(1, D), p['mu'], p['w_rkv'].astype(BF16), w1c, w2bd,
      p['w0'].reshape(1, 2 * D), a1c, a2bd, p['a0'].reshape(1, 2 * D),
      p['g1'].astype(BF16), p['g2'].astype(BF16), p['k_k'].reshape(1, D), p['k_a'].reshape(1, D),
      _head_ones())


CH = 32
SUB = 4
GH = 2
GW = GH * HEAD
NG = NHEAD // GH
SR = GH * CH

_NT = (((1,), (1,)), ((), ()))
_TN = (((0,), (0,)), ((), ()))


def _scan_masks():
    rho = np.arange(SR)
    same = (rho[:, None] // CH) == (rho[None, :] // CH)
    lower = rho[None, :] < rho[:, None]
    strict = np.stack([same & lower, same & lower.T])
    incl = strict | np.eye(SR, dtype=bool)[None]
    t = np.arange(CH)
    tri = np.stack([t[None, :] <= t[:, None], t[None, :] >= t[:, None]])
    return (jnp.asarray(strict.astype(np.float32)), jnp.asarray(incl.astype(np.float32)),
            jnp.asarray(tri.astype(np.float32), dtype=BF16))


def _wkv_kernel(rf_ref, vf_ref, kkf_ref, lwf_ref, kdf_ref, bf_ref, rb_ref, vb_ref, kkb_ref, lwb_ref, kdb_ref,
                bb_ref, strict_ref, incl_ref, tri_ref, s0f_ref, s0b_ref, yf_ref, yb_ref, sfinf_ref, sfinb_ref,
                s_ref,
                *, n_ctx_chunks, ctx_chunks_per_seq, smp_chunks_per_seq):
    q = pl.program_id(0)
    is_ctx = q < n_ctx_chunks
    c = jnp.where(is_ctx, q % ctx_chunks_per_seq, (q - n_ctx_chunks) % smp_chunks_per_seq)

    @pl.when(c == 0)
    def _():
        zero = jnp.zeros((HEAD, HEAD), F32)
        for d, s0_ref in enumerate((s0f_ref, s0b_ref)):
            for g in range(NG):
                bd = jnp.concatenate(
                    [jnp.concatenate([s0_ref[0, g * GH + h] if hh == h else zero for hh in range(GH)], axis=1)
                     for h in range(GH)], axis=0)
                s_ref[d, g] = jnp.where(is_ctx, 0.0, bd)

    def decayed(d, a, r_ref, v_ref, kk_ref, lw_ref, kd_ref, b_ref):
        pos = a if d == 0 else SUB - 1 - a
        rows = slice(pos * CH, (pos + 1) * CH)
        lw = lw_ref[rows, :]
        cum = _split_sum_left(tri_ref[d], lw)
        tot = cum[CH - 1:CH, :] if d == 0 else cum[0:1, :]
        e_neg = jnp.exp(-cum)
        e_rem = jnp.exp(tot - cum)
        kd = kd_ref[rows, :]
        bb = b_ref[rows, :]
        return (kk_ref[rows, :] * jnp.exp(cum - lw), r_ref[rows, :] * jnp.exp(cum), kd * e_neg, bb * e_neg,
                v_ref[rows, :], kd * e_rem, bb * e_rem), jnp.exp(tot), rows

    refs_d = ((rf_ref, vf_ref, kkf_ref, lwf_ref, kdf_ref, bf_ref),
              (rb_ref, vb_ref, kkb_ref, lwb_ref, kdb_ref, bb_ref))
    tiles = {(a, d): decayed(d, a, *refs_d[d]) for a in range(SUB) for d in range(2)}
    y_refs = (yf_ref, yb_ref)
    strict_d = [strict_ref[d] > 0.5 for d in range(2)]
    incl_d = [incl_ref[d] > 0.5 for d in range(2)]

    lane_head = lax.broadcasted_iota(jnp.int32, (CH, GW), 1) // HEAD
    head_lanes = [lane_head == h for h in range(GH)]

    def stack(x):
        return jnp.concatenate([jnp.where(m, x, 0.0) for m in head_lanes], axis=0).astype(BF16)

    eye = (lax.broadcasted_iota(jnp.int32, (SR, SR), 0)
           == lax.broadcasted_iota(jnp.int32, (SR, SR), 1)).astype(F32)
    dot = functools.partial(jnp.dot, preferred_element_type=F32)
    dotg = functools.partial(lax.dot_general, preferred_element_type=F32)

    dg = [(d, g) for g in range(NG) for d in range(2)]
    sl = {g: slice(g * GW, (g + 1) * GW) for g in range(NG)}
    pre = {}
    state = {c_: s_ref[c_[0], c_[1]] for c_ in dg}

    def independent(chunks):
        adg = [(a, d, g) for a in chunks for d, g in dg]
        ops = {(a, d, g): [stack(x[:, sl[g]]) for x in tiles[(a, d)][0]] for a, d, g in adg}
        qr = {c_: jnp.concatenate([o[0], o[1]], axis=0) for c_, o in ops.items()}
        gm = {c_: dotg(qr[c_], jnp.concatenate([o[3], o[2]], axis=0), _NT) for c_, o in ops.items()}
        yield
        l = {c_: jnp.where(strict_d[c_[1]], gm[c_][:SR, :SR], 0.0) for c_ in adg}
        m = {c_: jnp.where(strict_d[c_[1]], gm[c_][:SR, SR:], 0.0).astype(BF16) for c_ in adg}
        ppb = {c_: jnp.concatenate([jnp.where(incl_d[c_[1]], gm[c_][SR:, SR:], 0.0),
                                    jnp.where(incl_d[c_[1]], gm[c_][SR:, :SR], 0.0)], axis=1).astype(BF16)
               for c_ in adg}
        mv = {c_: dot(m[c_], ops[c_][4]) for c_ in adg}
        yield
        n = {c_: eye - l[c_] for c_ in adg}
        pw = {c_: l[c_].astype(BF16) for c_ in adg}
        for _ in range(CH.bit_length() - 2):
            pw = {c_: dot(pw[c_], pw[c_]).astype(BF16) for c_ in adg}
            yield
            n = {c_: n[c_] + dot(n[c_].astype(BF16), pw[c_]) for c_ in adg}
            yield
        for a in chunks:
            of = lambda t_: {c_: t_[(a,) + c_] for c_ in dg}
            pre[a] = dict(qr=of(qr), mv=of(mv), ppb=of(ppb), tinv={c_: n[(a,) + c_].astype(BF16) for c_ in dg},
                          vs={c_: ops[(a,) + c_][4] for c_ in dg},
                          khbh={c_: jnp.concatenate([ops[(a,) + c_][5], ops[(a,) + c_][6]], axis=0) for c_ in dg})

    def dependent(a):
        p_ = pre[a]
        s0 = dict(state)
        zz = {c_: dotg(p_['qr'][c_], s0[c_].astype(BF16), _NT) for c_ in dg}
        yield
        u = {c_: dot(p_['tinv'][c_], (zz[c_][:SR] + p_['mv'][c_]).astype(BF16)) for c_ in dg}
        yield
        vu = {c_: jnp.concatenate([p_['vs'][c_], (-u[c_]).astype(BF16)], axis=0) for c_ in dg}
        ys = {c_: zz[c_][SR:] + dot(p_['ppb'][c_], vu[c_]) for c_ in dg}
        yield
        for d, g in dg:
            state[(d, g)] = s0[(d, g)] * tiles[(a, d)][1][:, sl[g]] + dotg(vu[(d, g)], p_['khbh'][(d, g)], _TN)
        for d, g in dg:
            y = ys[(d, g)][0:CH]
            for h in range(1, GH):
                y = y + ys[(d, g)][h * CH:(h + 1) * CH]
            y_refs[d][tiles[(a, d)][2], sl[g]] = y.astype(BF16)

    def run(*gens):
        gens = list(gens)
        while gens:
            for gen in list(gens):
                if next(gen, gens) is gens:
                    gens.remove(gen)

    run(independent(tuple(range(SUB))))
    for a in range(SUB):
        run(dependent(a))
    for (d, g), s in state.items():
        s_ref[d, g] = s

    @pl.when(jnp.logical_and(is_ctx, c == ctx_chunks_per_seq - 1))
    def _():
        for (d, g), s in state.items():
            for h in range(GH):
                (sfinf_ref, sfinb_ref)[d][0, 0, g * GH + h] = s[h * HEAD:(h + 1) * HEAD, h * HEAD:(h + 1) * HEAD]


def _wkv(r, v, kk, lw, kd, b, state_f, state_b, n_ctx_seq=BATCH, t_ctx=SEQ, n_smp_seq=DEC_BATCH, t_smp=DEC_SEQ):
    ntok = n_ctx_seq * t_ctx + n_smp_seq * t_smp
    blk = SUB * CH
    cps_c, cps_s = t_ctx // blk, t_smp // blk
    ncc = n_ctx_seq * cps_c
    nq = ncc + n_smp_seq * cps_s

    def seq_of(q):
        return jnp.where(q < ncc, q // cps_c, n_ctx_seq + (q - ncc) // cps_s)

    def row_block(d, q):
        ctx = q < ncc
        cps = jnp.where(ctx, cps_c, cps_s)
        c = jnp.where(ctx, q % cps_c, (q - ncc) % cps_s)
        return q - c + (c if d == 0 else cps - 1 - c)

    tok = [pl.BlockSpec((blk, D), functools.partial(lambda q, d: (row_block(d, q), 0), d=d)) for d in range(2)]
    tokd = [pl.BlockSpec((blk, D), functools.partial(lambda q, d: (row_block(d, q), d), d=d)) for d in range(2)]
    msk = pl.BlockSpec((2, SR, SR), lambda q: (0, 0, 0))
    tri = pl.BlockSpec((2, CH, CH), lambda q: (0, 0, 0))
    s0spec = pl.BlockSpec((1, NHEAD, HEAD, HEAD), lambda q: (jnp.maximum(seq_of(q) - n_ctx_seq, 0), 0, 0, 0))
    sspec = pl.BlockSpec((1, 1, NHEAD, HEAD, HEAD),
                         lambda q: (jnp.minimum(seq_of(q), n_ctx_seq - 1), 0, 0, 0, 0))
    sfin = jax.ShapeDtypeStruct((n_ctx_seq, 1, NHEAD, HEAD, HEAD), F32)

    strict, incl, trim = _scan_masks()
    return pl.pallas_call(
        functools.partial(_wkv_kernel, n_ctx_chunks=ncc, ctx_chunks_per_seq=cps_c, smp_chunks_per_seq=cps_s),
        out_shape=[jax.ShapeDtypeStruct((ntok, D), BF16), jax.ShapeDtypeStruct((ntok, D), BF16), sfin, sfin],
        grid=(nq,),
        in_specs=[tok[0], tok[0], tok[0], tokd[0], tokd[0], tokd[0],
                  tok[1], tok[1], tok[1], tokd[1], tokd[1], tokd[1], msk, msk, tri, s0spec, s0spec],
        out_specs=[tok[0], tok[1], sspec, sspec],
        scratch_shapes=[pltpu.VMEM((2, NG, GW, GW), F32)],
        compiler_params=pltpu.CompilerParams(
            dimension_semantics=("arbitrary",), vmem_limit_bytes=VMEM_LIMIT),
        name="wkv_scan",
    )(r, v, kk, lw, kd, b, r, v, kk, lw, kd, b, strict, incl, trim, state_f, state_b)


RO_TM = 512


def _rwkv_out_kernel(x_ref, yf_ref, yb_ref, r_ref, v_ref, kd_ref, g_ref, mod_ref, gnw_ref, gnb_ref,
                     rk_ref, wo_ref, gpost_ref, ones_ref, o_ref):
    ones = ones_ref[...]
    ys = yf_ref[...] + yb_ref[...]
    mean = _head_sum(ys, ones) * (1.0 / HEAD)
    c = ys - mean
    var = _head_sum(c * c, ones) * (1.0 / HEAD)
    yn = c * lax.rsqrt(var + GN_EPS) * gnw_ref[...] + gnb_ref[...]
    kd = kd_ref[...]
    bonus = _head_sum(r_ref[...] * (kd[:, :D] + kd[:, D:]) * rk_ref[...], ones) * v_ref[...]
    out = _dot((yn + bonus) * g_ref[...], wo_ref[...])
    o_ref[...] = x_ref[...] + mod_ref[0, 5:6, :] * _rms(out, gpost_ref[...])


def _rwkv_out(x, y_f, y_b, r, v, kd, g, mod, p, gpost):
    tm = RO_TM
    row = lambda i: (i, 0)
    full2 = lambda i: (0, 0)
    vec = pl.BlockSpec((1, D), full2)
    tok = pl.BlockSpec((tm, D), row)
    tok2 = pl.BlockSpec((tm, 2 * D), row)
    return pl.pallas_call(
        _rwkv_out_kernel,
        out_shape=jax.ShapeDtypeStruct((NTOK, D), F32),
        grid=(NTOK // tm,),
        in_specs=[tok, tok, tok, tok, tok, tok2, tok,
                  pl.BlockSpec((1, N_MOD, D), lambda i: (_cond_row(i * tm), 0, 0)),
                  vec, vec, vec, pl.BlockSpec((D, D), full2), vec, pl.BlockSpec((HSUM_W, HSUM_W), full2)],
        out_specs=tok,
        compiler_params=pltpu.CompilerParams(
            dimension_semantics=("arbitrary",), vmem_limit_bytes=VMEM_LIMIT),
        name="rwkv_out",
    )(x, y_f, y_b, r, v, kd, g, mod, p['gn_w'].reshape(1, D), p['gn_b'].reshape(1, D),
      p['r_k'].reshape(1, D), p['w_o'].astype(BF16), gpost.reshape(1, D), _head_ones())


def _box_matrix(length, win):
    t = np.arange(length)
    lo = np.clip(t - win // 2, 0, length)
    hi = np.clip(t + win - win // 2, 0, length)
    a = ((t[None, :] >= lo[:, None]) & (t[None, :] < hi[:, None])).astype(np.float32)
    return a, (hi - lo).astype(np.float32)


def _pool_constants(grid):
    mats, inv = [], []
    for win in POOL_WINDOWS:
        if grid:
            ar, cr = _box_matrix(DEC_SEQ // GRID_W, win)
            ac, cc = _box_matrix(GRID_W, win)
            mats.append(np.kron(ar, ac))
            inv.append(1.0 / np.kron(cr, cc))
        else:
            a, c = _box_matrix(SEQ, win)
            mats.append(a)
            inv.append(1.0 / c)
    mats = jnp.asarray(np.stack(mats).astype(BF16))
    inv = jnp.asarray(np.stack(inv)[:, :, None], dtype=F32)
    return mats, inv


def _pool_kernel(rs_ref, xg_ref, mod_ref, gpre_ref, a_ref, inv_ref, pw_ref, ps_ref, o_ref, *, t):
    for j in range(xg_ref.shape[0] // t):
        rows = slice(j * t, (j + 1) * t)
        part = (xg_ref[rows, :] * rs_ref[rows, 0:1]) * gpre_ref[...] * (1.0 + mod_ref[0, 4:5, :]) + mod_ref[0, 3:4, :]
        m = _dot(a_ref[0], part) * inv_ref[0]
        o_ref[rows, :] = _dot(m - part, pw_ref[0]) * ps_ref[...]


def _pool(x, rs, mod, gpre, pool_w, pool_scale, t, seq0, nseq, grid, per_step=1):
    mats, inv = _pool_constants(grid)
    gc = POOL_GC
    tb = t * per_step
    seq0 //= per_step
    return pl.pallas_call(
        functools.partial(_pool_kernel, t=t),
        out_shape=jax.ShapeDtypeStruct((nseq * t, D), F32),
        grid=(4, nseq // per_step),
        in_specs=[
            pl.BlockSpec((tb, RS_W), lambda g, s: (seq0 + s, 0)),
            pl.BlockSpec((tb, gc), lambda g, s: (seq0 + s, g)),
            pl.BlockSpec((1, N_MOD, gc), lambda g, s: (_cond_row((seq0 + s) * tb), 0, g)),
            pl.BlockSpec((1, gc), lambda g, s: (0, g)),
            pl.BlockSpec((1, t, t), lambda g, s: (g, 0, 0)),
            pl.BlockSpec((1, t, 1), lambda g, s: (g, 0, 0)),
            pl.BlockSpec((1, gc, gc), lambda g, s: (g, 0, 0)),
            pl.BlockSpec((1, gc), lambda g, s: (0, g)),
        ],
        out_specs=pl.BlockSpec((tb, gc), lambda g, s: (s, g)),
        compiler_params=pltpu.CompilerParams(
            dimension_semantics=("arbitrary", "arbitrary"), vmem_limit_bytes=VMEM_LIMIT),
        name="pool_grid" if grid else "pool_seq",
    )(rs, x, mod, gpre.reshape(1, D), mats, inv, pool_w.astype(BF16), pool_scale.reshape(1, D))


def kernel(x_prompt, x_sample, c, state_ctx_fwd, state_ctx_bwd, c_ctx, w_mod, b_mod, norm_pre, norm_post,
           ffn_w1, ffn_w3, ffn_w2, rwkv_mu, rwkv_w_rkv, rwkv_w0, rwkv_w1, rwkv_w2, rwkv_a0, rwkv_a1,
           rwkv_a2, rwkv_g1, rwkv_g2, rwkv_k_k, rwkv_k_a, rwkv_r_k, rwkv_gn_w, rwkv_gn_b, rwkv_w_o,
           pool_w, pool_scale):
    x = (x_prompt.reshape(N_CTX, D), x_sample.reshape(N_SMP, D))
    cond =jnp.concatenate([c_ctx[None], c, jnp.zeros((COND_ROWS - 1 - DEC_BATCH, D), F32)])
    mod = _modulation(cond, w_mod, b_mod).reshape(DEPTH, COND_ROWS, N_MOD, D)
    wffn = (ffn_w1.astype(BF16), ffn_w3.astype(BF16), ffn_w2.astype(BF16))

    x = _ffn(x, mod[0], norm_pre[0, 0], norm_post[0, 0], wffn, (0, 0), 0)
    p = dict(mu=rwkv_mu[0], w_rkv=rwkv_w_rkv[0], w0=rwkv_w0[0], w1=rwkv_w1[0], w2=rwkv_w2[0],
             a0=rwkv_a0[0], a1=rwkv_a1[0], a2=rwkv_a2[0], g1=rwkv_g1[0], g2=rwkv_g2[0],
             k_k=rwkv_k_k[0], k_a=rwkv_k_a[0], r_k=rwkv_r_k[0], gn_w=rwkv_gn_w[0], gn_b=rwkv_gn_b[0],
             w_o=rwkv_w_o[0])
    r, v, kk, g, lw, kd, b = _rwkv_proj(x, mod[0], norm_pre[0, 1], p)
    y_f, y_b, new_f, new_b = _wkv(r, v, kk, lw, kd, b, state_ctx_fwd[:, 0], state_ctx_bwd[:, 0])
    x = _rwkv_out(x, y_f, y_b, r, v, kd, g, mod[0], p, norm_post[0, 1])
    x = _ffn(x, mod[0], norm_pre[0, 2], norm_post[0, 2], wffn, (0, 1), 6)

    x, rs = _ffn(x, mod[1], norm_pre[1, 0], norm_post[1, 0], wffn, (1, 0), 0, emit_rs=True)
    pc = _pool(x, rs, mod[1], norm_pre[1, 1], pool_w[0], pool_scale[0], SEQ, 0, BATCH, False, per_step=4)
    ps = _pool(x, rs, mod[1], norm_pre[1, 1], pool_w[0], pool_scale[0], DEC_SEQ, N_CTX // DEC_SEQ, DEC_BATCH, True)
    y_ctx, y_smp = _ffn(x, mod[1], norm_pre[1, 2], norm_post[1, 2], wffn, (1, 1), 6,
                        pending=((pc, ps), norm_post[1, 1], 5), split_out=True)

    return (y_ctx.reshape(BATCH, SEQ, D), y_smp.reshape(DEC_BATCH, DEC_SEQ, D), new_f, new_b)
```

```python
import functools

import numpy as np
import jax
import jax.numpy as jnp
from jax import lax
from jax.experimental import pallas as pl
from jax.experimental.pallas import tpu as pltpu

D = 1024
HEAD = 64
NHEAD = D // HEAD
D_FF = 2816
DEPTH = 2
N_MOD = 9
POOL_WINDOWS = (2, 4, 8, 16)
POOL_GC = D // 4
GRID_W = 64
RMS_EPS = 1e-6
GN_EPS = 64e-5

BATCH, SEQ = 32, 256
DEC_BATCH, DEC_SEQ = 8, 2048
N_CTX = BATCH * SEQ
N_SMP = DEC_BATCH * DEC_SEQ
NTOK = N_CTX + N_SMP
COND_ROWS = 16

VMEM_LIMIT = 56 * 1024 * 1024

F32 = jnp.float32
BF16 = jnp.bfloat16


def _dot(a, b):
    return jnp.dot(a.astype(BF16), b.astype(BF16), preferred_element_type=F32)


def _rms(x, g):
    return (x * lax.rsqrt(jnp.mean(x * x, axis=-1, keepdims=True) + RMS_EPS)) * g


def _cond_row(start_row):
    return jnp.where(start_row < N_CTX, 0, 1 + (start_row - N_CTX) // DEC_SEQ)


def _split(x):
    hi = x.astype(BF16)
    return hi, (x - hi.astype(F32)).astype(BF16)


def _split_sum(x, ones):
    hi, lo = _split(x)
    return (jnp.dot(hi, ones, preferred_element_type=F32)
            + jnp.dot(lo, ones, preferred_element_type=F32))


HSUM_W = 256


def _head_sum(x, ones):
    return jnp.concatenate([_split_sum(x[:, i:i + HSUM_W], ones) for i in range(0, D, HSUM_W)], axis=1)


def _split_sum_left(ones, x):
    hi, lo = _split(x)
    return (jnp.dot(ones, hi, preferred_element_type=F32)
            + jnp.dot(ones, lo, preferred_element_type=F32))


def _mod_kernel(c_ref, w_ref, b_ref, o_ref):
    c = c_ref[...]
    o_ref[0] = _dot(jax.nn.silu(c), w_ref[0]) + b_ref[0]


def _modulation(cond, w_mod, b_mod):
    return pl.pallas_call(
        _mod_kernel,
        out_shape=jax.ShapeDtypeStruct((DEPTH, COND_ROWS, N_MOD * D), F32),
        grid=(DEPTH, N_MOD),
        in_specs=[
            pl.BlockSpec((COND_ROWS, D), lambda l, n: (0, 0)),
            pl.BlockSpec((1, D, D), lambda l, n: (l, 0, n)),
            pl.BlockSpec((1, 1, D), lambda l, n: (l, 0, n)),
        ],
        out_specs=pl.BlockSpec((1, COND_ROWS, D), lambda l, n: (l, 0, n)),
        compiler_params=pltpu.CompilerParams(
            dimension_semantics=("arbitrary", "arbitrary"), vmem_limit_bytes=VMEM_LIMIT),
        name="modulation",
    )(cond, w_mod, b_mod.reshape(DEPTH, 1, N_MOD * D))


FFN_ROWS = 256
FFN_TM = 1024
FFN_TM_PENDING = 512


def _tok_operand(x, tm):
    if not isinstance(x, tuple):
        return [x], [pl.BlockSpec((tm, D), lambda i: (i, 0))]
    na = N_CTX // tm
    return list(x), [pl.BlockSpec((tm, D), lambda i: (jnp.minimum(i, na - 1), 0)),
                     pl.BlockSpec((tm, D), lambda i: (jnp.maximum(i - na, 0), 0))]


def _ffn_kernel(*refs, tm, k0, pending_k, n_x, n_d, n_o, emit_rs):
    refs = list(refs)
    take = lambda n: [refs.pop(0) for _ in range(n)]
    x_refs = take(n_x)
    mod_ref, gpre_ref, gpost_ref, w1_ref, w3_ref, w2_ref = take(6)
    d_refs = take(n_d)
    gd_ref = take(1)[0] if n_d else None
    o_refs = take(n_o)
    rs_ref = take(1)[0] if emit_rs else None
    is_ctx = pl.program_id(0) < N_CTX // tm

    def read(rs, rows):
        return rs[0][rows, :] if len(rs) == 1 else jnp.where(is_ctx, rs[0][rows, :], rs[1][rows, :])

    rt = FFN_ROWS
    for s in range(tm // rt):
        rows = slice(s * rt, (s + 1) * rt)
        x = read(x_refs, rows)
        if pending_k is not None:
            x = x + mod_ref[0, pending_k:pending_k + 1, :] * _rms(read(d_refs, rows), gd_ref[...])
        h = _rms(x, gpre_ref[...]) * (1.0 + mod_ref[0, k0 + 1:k0 + 2, :]) + mod_ref[0, k0:k0 + 1, :]
        h = h.astype(BF16)
        u = jnp.dot(h, w1_ref[...], preferred_element_type=F32)
        g = jnp.dot(h, w3_ref[...], preferred_element_type=F32)
        a = (jax.nn.silu(u) * g).astype(BF16)
        f = jnp.dot(a, w2_ref[...], preferred_element_type=F32)
        y = x + 0.5 * mod_ref[0, k0 + 2:k0 + 3, :] * _rms(f, gpost_ref[...])
        if emit_rs:
            rs = lax.rsqrt(jnp.mean(y * y, axis=-1, keepdims=True) + RMS_EPS)
            rs_ref[rows, :] = jnp.broadcast_to(rs, (rt, RS_W))
        if n_o == 1:
            o_refs[0][rows, :] = y
        else:
            @pl.when(is_ctx)
            def _():
                o_refs[0][rows, :] = y

            @pl.when(jnp.logical_not(is_ctx))
            def _():
                o_refs[1][rows, :] = y


RS_W = 128


def _ffn(x, mod, gpre, gpost, weights, wsel, k0, pending=None, split_out=False, emit_rs=False):
    tm = FFN_TM if pending is None else FFN_TM_PENDING
    vec = pl.BlockSpec((1, D), lambda i: (0, 0))
    resident = lambda rows, cols: pl.BlockSpec((None, None, rows, cols), lambda i: (*wsel, 0, 0),
                                               pipeline_mode=pl.Buffered(1))
    x_args, x_specs = _tok_operand(x, tm)
    args = x_args + [mod, gpre.reshape(1, D), gpost.reshape(1, D), *weights]
    in_specs = x_specs + [
        pl.BlockSpec((1, N_MOD, D), lambda i: (_cond_row(i * tm), 0, 0)), vec, vec,
        resident(D, D_FF), resident(D, D_FF), resident(D_FF, D),
    ]
    pending_k, n_d = None, 0
    if pending is not None:
        delta, gd, pending_k = pending
        d_args, d_specs = _tok_operand(delta, tm)
        n_d = len(d_args)
        args += d_args + [gd.reshape(1, D)]
        in_specs += d_specs + [vec]
    if split_out:
        out_shape = [jax.ShapeDtypeStruct((N_CTX, D), F32), jax.ShapeDtypeStruct((N_SMP, D), F32)]
        out_specs = _tok_operand((None, None), tm)[1]
    else:
        out_shape = [jax.ShapeDtypeStruct((NTOK, D), F32)]
        out_specs = _tok_operand(None, tm)[1]
    n_o = len(out_shape)
    if emit_rs:
        out_shape = out_shape + [jax.ShapeDtypeStruct((NTOK, RS_W), F32)]
        out_specs = out_specs + [pl.BlockSpec((tm, RS_W), lambda i: (i, 0))]
    elif n_o == 1:
        out_shape, out_specs = out_shape[0], out_specs[0]
    return pl.pallas_call(
        functools.partial(_ffn_kernel, tm=tm, k0=k0, pending_k=pending_k, n_x=len(x_args), n_d=n_d,
                          n_o=n_o, emit_rs=emit_rs),
        out_shape=out_shape,
        grid=(NTOK // tm,),
        in_specs=in_specs,
        out_specs=out_specs,
        compiler_params=pltpu.CompilerParams(
            dimension_semantics=("arbitrary",), vmem_limit_bytes=VMEM_LIMIT),
        name="ffn",
    )(*args)


RW_TM = 256
HALO = 8


def _rwkv_proj_kernel(x_ref, xp_ref, xn_ref, mod_ref, gpre_ref, mu_ref, wrkv_ref, w1_ref, w2_ref, w0_ref,
                      a1_ref, a2_ref, a0_ref, g1_ref, g2_ref, k_k_ref, k_a_ref, ones_ref,
                      r_out, v_out, kk_out, g_out, lw_out, kd_out, b_out):
    i = pl.program_id(0)
    shift = mod_ref[0, 3:4, :]
    scale = 1.0 + mod_ref[0, 4:5, :]
    gpre = gpre_ref[...]
    premod = lambda x: _rms(x, gpre) * scale + shift

    k = i - N_CTX // RW_TM
    per_seq = DEC_SEQ // RW_TM
    is_ctx = i < N_CTX // RW_TM
    first = jnp.logical_or(is_ctx, k % per_seq == 0)
    last = jnp.logical_or(is_ctx, k % per_seq == per_seq - 1)

    h = premod(x_ref[...])
    hp = jnp.where(first, 0.0, premod(xp_ref[...])[HALO - 1:HALO, :])
    hn = jnp.where(last, 0.0, premod(xn_ref[...])[0:1, :])
    rows = lax.broadcasted_iota(jnp.int32, (RW_TM, 1), 0)
    h_prev = jnp.where(rows == 0, hp, pltpu.roll(h, 1, 0))
    h_next = jnp.where(rows == RW_TM - 1, hn, pltpu.roll(h, RW_TM - 1, 0))
    xx = 0.5 * (h_prev + h_next) - h
    mix = lambda n: h + xx * mu_ref[n:n + 1, :]

    r = _dot(mix(0), wrkv_ref[0])
    kx = _dot(mix(2), wrkv_ref[1])
    v = _dot(mix(3), wrkv_ref[2])
    g = _dot(jax.nn.sigmoid(_dot(mix(5), g1_ref[...])), g2_ref[...])

    z = w0_ref[...] + _dot(jnp.tanh(_dot(mix(1), w1_ref[...])), w2_ref[...])
    a = jax.nn.sigmoid(a0_ref[...] + _dot(_dot(mix(4), a1_ref[...]), a2_ref[...]))

    kkraw = kx * k_k_ref[...]
    nrm = jnp.sqrt(_head_sum(kkraw * kkraw, ones_ref[...]))
    kk = kkraw / jnp.maximum(nrm, 1e-12)

    r_out[...] = r.astype(BF16)
    v_out[...] = v.astype(BF16)
    kk_out[...] = kk.astype(BF16)
    g_out[...] = g.astype(BF16)
    lw_out[...] = -float(np.exp(-0.5)) * jax.nn.sigmoid(z)
    ka = k_a_ref[...]
    for d in range(2):
        a_d = a[:, d * D:(d + 1) * D]
        kd_out[:, d * D:(d + 1) * D] = (kx * (1.0 + (a_d - 1.0) * ka)).astype(BF16)
        b_out[:, d * D:(d + 1) * D] = (kk * a_d).astype(BF16)


def _blockdiag2(m):
    z = jnp.zeros_like(m[0])
    return jnp.concatenate([jnp.concatenate([m[0], z], axis=1), jnp.concatenate([z, m[1]], axis=1)], axis=0)


def _head_ones():
    idx = np.arange(HSUM_W) // HEAD
    return jnp.asarray((idx[:, None] == idx[None, :]).astype(np.float32), dtype=BF16)


def _rwkv_proj(x, mod, gpre, p):
    tm = RW_TM
    nblk = NTOK // HALO
    row = lambda i: (i, 0)
    full2 = lambda i: (0, 0)
    full3 = lambda i: (0, 0, 0)
    w1c = jnp.concatenate([p['w1'][0], p['w1'][1]], axis=1).astype(BF16)
    a1c = jnp.concatenate([p['a1'][0], p['a1'][1]], axis=1).astype(BF16)
    w2bd = _blockdiag2(p['w2']).astype(BF16)
    a2bd = _blockdiag2(p['a2']).astype(BF16)
    return pl.pallas_call(
        _rwkv_proj_kernel,
        out_shape=([jax.ShapeDtypeStruct((NTOK, D), BF16)] * 4 + [jax.ShapeDtypeStruct((NTOK, 2 * D), F32)]
                   + [jax.ShapeDtypeStruct((NTOK, 2 * D), BF16)] * 2),
        grid=(NTOK // tm,),
        in_specs=[
            pl.BlockSpec((tm, D), row),
            pl.BlockSpec((HALO, D), lambda i: (jnp.maximum(i * (tm // HALO) - 1, 0), 0)),
            pl.BlockSpec((HALO, D), lambda i: (jnp.minimum((i + 1) * (tm // HALO), nblk - 1), 0)),
            pl.BlockSpec((1, N_MOD, D), lambda i: (_cond_row(i * tm), 0, 0)),
            pl.BlockSpec((1, D), full2),
            pl.BlockSpec((6, D), full2),
            pl.BlockSpec((3, D, D), full3),
            pl.BlockSpec((D, 128), full2),
            pl.BlockSpec((128, 2 * D), full2),
            pl.BlockSpec((1, 2 * D), full2),
            pl.BlockSpec((D, 128), full2),
            pl.BlockSpec((128, 2 * D), full2),
            pl.BlockSpec((1, 2 * D), full2),
            pl.BlockSpec((D, 128), full2),
            pl.BlockSpec((128, D), full2),
            pl.BlockSpec((1, D), full2),
            pl.BlockSpec((1, D), full2),
            pl.BlockSpec((HSUM_W, HSUM_W), full2),
        ],
        out_specs=[pl.BlockSpec((tm, D), row)] * 4 + [pl.BlockSpec((tm, 2 * D), row)] * 3,
        compiler_params=pltpu.CompilerParams(
            dimension_semantics=("arbitrary",), vmem_limit_bytes=VMEM_LIMIT),
        name="rwkv_proj",
    )(x, x, x, mod, gpre.reshape(1, D), p['mu'], p['w_rkv'].astype(BF16), w1c, w2bd,
      p['w0'].reshape(1, 2 * D), a1c, a2bd, p['a0'].reshape(1, 2 * D),
      p['g1'].astype(BF16), p['g2'].astype(BF16), p['k_k'].reshape(1, D), p['k_a'].reshape(1, D),
      _head_ones())


CH = 32
SUB = 4
GH = 2
GW = GH * HEAD
NG = NHEAD // GH
SR = GH * CH

_NT = (((1,), (1,)), ((), ()))
_TN = (((0,), (0,)), ((), ()))


def _scan_masks():
    rho = np.arange(SR)
    same = (rho[:, None] // CH) == (rho[None, :] // CH)
    lower = rho[None, :] < rho[:, None]
    strict = np.stack([same & lower, same & lower.T])
    incl = strict | np.eye(SR, dtype=bool)[None]
    t = np.arange(CH)
    tri = np.stack([t[None, :] <= t[:, None], t[None, :] >= t[:, None]])
    return (jnp.asarray(strict.astype(np.float32)), jnp.asarray(incl.astype(np.float32)),
            jnp.asarray(tri.astype(np.float32), dtype=BF16))


def _wkv_kernel(rf_ref, vf_ref, kkf_ref, lwf_ref, kdf_ref, bf_ref, rb_ref, vb_ref, kkb_ref, lwb_ref, kdb_ref,
                bb_ref, strict_ref, incl_ref, tri_ref, s0f_ref, s0b_ref, yf_ref, yb_ref, sfinf_ref, sfinb_ref,
                s_ref,
                *, n_ctx_chunks, ctx_chunks_per_seq, smp_chunks_per_seq):
    q = pl.program_id(0)
    is_ctx = q < n_ctx_chunks
    c = jnp.where(is_ctx, q % ctx_chunks_per_seq, (q - n_ctx_chunks) % smp_chunks_per_seq)

    @pl.when(c == 0)
    def _():
        zero = jnp.zeros((HEAD, HEAD), F32)
        for d, s0_ref in enumerate((s0f_ref, s0b_ref)):
            for g in range(NG):
                bd = jnp.concatenate(
                    [jnp.concatenate([s0_ref[0, g * GH + h] if hh == h else zero for hh in range(GH)], axis=1)
                     for h in range(GH)], axis=0)
                s_ref[d, g] = jnp.where(is_ctx, 0.0, bd)

    def decayed(d, a, r_ref, v_ref, kk_ref, lw_ref, kd_ref, b_ref):
        pos = a if d == 0 else SUB - 1 - a
        rows = slice(pos * CH, (pos + 1) * CH)
        lw = lw_ref[rows, :]
        cum = _split_sum_left(tri_ref[d], lw)
        tot = cum[CH - 1:CH, :] if d == 0 else cum[0:1, :]
        e_neg = jnp.exp(-cum)
        e_rem = jnp.exp(tot - cum)
        kd = kd_ref[rows, :]
        bb = b_ref[rows, :]
        return (kk_ref[rows, :] * jnp.exp(cum - lw), r_ref[rows, :] * jnp.exp(cum), kd * e_neg, bb * e_neg,
                v_ref[rows, :], kd * e_rem, bb * e_rem), jnp.exp(tot), rows

    refs_d = ((rf_ref, vf_ref, kkf_ref, lwf_ref, kdf_ref, bf_ref),
              (rb_ref, vb_ref, kkb_ref, lwb_ref, kdb_ref, bb_ref))
    tiles = {(a, d): decayed(d, a, *refs_d[d]) for a in range(SUB) for d in range(2)}
    y_refs = (yf_ref, yb_ref)
    strict_d = [strict_ref[d] > 0.5 for d in range(2)]
    incl_d = [incl_ref[d] > 0.5 for d in range(2)]

    lane_head = lax.broadcasted_iota(jnp.int32, (CH, GW), 1) // HEAD
    head_lanes = [lane_head == h for h in range(GH)]

    def stack(x):
        return jnp.concatenate([jnp.where(m, x, 0.0) for m in head_lanes], axis=0).astype(BF16)

    eye = (lax.broadcasted_iota(jnp.int32, (SR, SR), 0)
           == lax.broadcasted_iota(jnp.int32, (SR, SR), 1)).astype(F32)
    dot = functools.partial(jnp.dot, preferred_element_type=F32)
    dotg = functools.partial(lax.dot_general, preferred_element_type=F32)

    dg = [(d, g) for g in range(NG) for d in range(2)]
    sl = {g: slice(g * GW, (g + 1) * GW) for g in range(NG)}
    pre = {}
    state = {c_: s_ref[c_[0], c_[1]] for c_ in dg}

    def independent(chunks):
        adg = [(a, d, g) for a in chunks for d, g in dg]
        ops = {(a, d, g): [stack(x[:, sl[g]]) for x in tiles[(a, d)][0]] for a, d, g in adg}
        qr = {c_: jnp.concatenate([o[0], o[1]], axis=0) for c_, o in ops.items()}
        gm = {c_: dotg(qr[c_], jnp.concatenate([o[3], o[2]], axis=0), _NT) for c_, o in ops.items()}
        yield
        l = {c_: jnp.where(strict_d[c_[1]], gm[c_][:SR, :SR], 0.0) for c_ in adg}
        m = {c_: jnp.where(strict_d[c_[1]], gm[c_][:SR, SR:], 0.0).astype(BF16) for c_ in adg}
        ppb = {c_: jnp.concatenate([jnp.where(incl_d[c_[1]], gm[c_][SR:, SR:], 0.0),
                                    jnp.where(incl_d[c_[1]], gm[c_][SR:, :SR], 0.0)], axis=1).astype(BF16)
               for c_ in adg}
        mv = {c_: dot(m[c_], ops[c_][4]) for c_ in adg}
        yield
        n = {c_: eye - l[c_] for c_ in adg}
        pw = {c_: l[c_].astype(BF16) for c_ in adg}
        for _ in range(CH.bit_length() - 2):
            pw = {c_: dot(pw[c_], pw[c_]).astype(BF16) for c_ in adg}
            yield
            n = {c_: n[c_] + dot(n[c_].astype(BF16), pw[c_]) for c_ in adg}
            yield
        for a in chunks:
            of = lambda t_: {c_: t_[(a,) + c_] for c_ in dg}
            pre[a] = dict(qr=of(qr), mv=of(mv), ppb=of(ppb), tinv={c_: n[(a,) + c_].astype(BF16) for c_ in dg},
                          vs={c_: ops[(a,) + c_][4] for c_ in dg},
                          khbh={c_: jnp.concatenate([ops[(a,) + c_][5], ops[(a,) + c_][6]], axis=0) for c_ in dg})

    def dependent(a):
        p_ = pre[a]
        s0 = dict(state)
        zz = {c_: dotg(p_['qr'][c_], s0[c_].astype(BF16), _NT) for c_ in dg}
        yield
        u = {c_: dot(p_['tinv'][c_], (zz[c_][:SR] + p_['mv'][c_]).astype(BF16)) for c_ in dg}
        yield
        vu = {c_: jnp.concatenate([p_['vs'][c_], (-u[c_]).astype(BF16)], axis=0) for c_ in dg}
        ys = {c_: zz[c_][SR:] + dot(p_['ppb'][c_], vu[c_]) for c_ in dg}
        yield
        for d, g in dg:
            state[(d, g)] = s0[(d, g)] * tiles[(a, d)][1][:, sl[g]] + dotg(vu[(d, g)], p_['khbh'][(d, g)], _TN)
        for d, g in dg:
            y = ys[(d, g)][0:CH]
            for h in range(1, GH):
                y = y + ys[(d, g)][h * CH:(h + 1) * CH]
            y_refs[d][tiles[(a, d)][2], sl[g]] = y.astype(BF16)

    def run(*gens):
        gens = list(gens)
        while gens:
            for gen in list(gens):
                if next(gen, gens) is gens:
                    gens.remove(gen)

    run(independent(tuple(range(SUB))))
    for a in range(SUB):
        run(dependent(a))
    for (d, g), s in state.items():
        s_ref[d, g] = s

    @pl.when(jnp.logical_and(is_ctx, c == ctx_chunks_per_seq - 1))
    def _():
        for (d, g), s in state.items():
            for h in range(GH):
                (sfinf_ref, sfinb_ref)[d][0, 0, g * GH + h] = s[h * HEAD:(h + 1) * HEAD, h * HEAD:(h + 1) * HEAD]


def _wkv(r, v, kk, lw, kd, b, state_f, state_b, n_ctx_seq=BATCH, t_ctx=SEQ, n_smp_seq=DEC_BATCH, t_smp=DEC_SEQ):
    ntok = n_ctx_seq * t_ctx + n_smp_seq * t_smp
    blk = SUB * CH
    cps_c, cps_s = t_ctx // blk, t_smp // blk
    ncc = n_ctx_seq * cps_c
    nq = ncc + n_smp_seq * cps_s

    def seq_of(q):
        return jnp.where(q < ncc, q // cps_c, n_ctx_seq + (q - ncc) // cps_s)

    def row_block(d, q):
        ctx = q < ncc
        cps = jnp.where(ctx, cps_c, cps_s)
        c = jnp.where(ctx, q % cps_c, (q - ncc) % cps_s)
        return q - c + (c if d == 0 else cps - 1 - c)

    tok = [pl.BlockSpec((blk, D), functools.partial(lambda q, d: (row_block(d, q), 0), d=d)) for d in range(2)]
    tokd = [pl.BlockSpec((blk, D), functools.partial(lambda q, d: (row_block(d, q), d), d=d)) for d in range(2)]
    msk = pl.BlockSpec((2, SR, SR), lambda q: (0, 0, 0))
    tri = pl.BlockSpec((2, CH, CH), lambda q: (0, 0, 0))
    s0spec = pl.BlockSpec((1, NHEAD, HEAD, HEAD), lambda q: (jnp.maximum(seq_of(q) - n_ctx_seq, 0), 0, 0, 0))
    sspec = pl.BlockSpec((1, 1, NHEAD, HEAD, HEAD),
                         lambda q: (jnp.minimum(seq_of(q), n_ctx_seq - 1), 0, 0, 0, 0))
    sfin = jax.ShapeDtypeStruct((n_ctx_seq, 1, NHEAD, HEAD, HEAD), F32)

    strict, incl, trim = _scan_masks()
    return pl.pallas_call(
        functools.partial(_wkv_kernel, n_ctx_chunks=ncc, ctx_chunks_per_seq=cps_c, smp_chunks_per_seq=cps_s),
        out_shape=[jax.ShapeDtypeStruct((ntok, D), BF16), jax.ShapeDtypeStruct((ntok, D), BF16), sfin, sfin],
        grid=(nq,),
        in_specs=[tok[0], tok[0], tok[0], tokd[0], tokd[0], tokd[0],
                  tok[1], tok[1], tok[1], tokd[1], tokd[1], tokd[1], msk, msk, tri, s0spec, s0spec],
        out_specs=[tok[0], tok[1], sspec, sspec],
        scratch_shapes=[pltpu.VMEM((2, NG, GW, GW), F32)],
        compiler_params=pltpu.CompilerParams(
            dimension_semantics=("arbitrary",), vmem_limit_bytes=VMEM_LIMIT),
        name="wkv_scan",
    )(r, v, kk, lw, kd, b, r, v, kk, lw, kd, b, strict, incl, trim, state_f, state_b)


RO_TM = 512


def _rwkv_out_kernel(x_ref, yf_ref, yb_ref, r_ref, v_ref, kd_ref, g_ref, mod_ref, gnw_ref, gnb_ref,
                     rk_ref, wo_ref, gpost_ref, ones_ref, o_ref):
    ones = ones_ref[...]
    ys = yf_ref[...] + yb_ref[...]
    mean = _head_sum(ys, ones) * (1.0 / HEAD)
    c = ys - mean
    var = _head_sum(c * c, ones) * (1.0 / HEAD)
    yn = c * lax.rsqrt(var + GN_EPS) * gnw_ref[...] + gnb_ref[...]
    kd = kd_ref[...]
    bonus = _head_sum(r_ref[...] * (kd[:, :D] + kd[:, D:]) * rk_ref[...], ones) * v_ref[...]
    out = _dot((yn + bonus) * g_ref[...], wo_ref[...])
    o_ref[...] = x_ref[...] + mod_ref[0, 5:6, :] * _rms(out, gpost_ref[...])


def _rwkv_out(x, y_f, y_b, r, v, kd, g, mod, p, gpost):
    tm = RO_TM
    row = lambda i: (i, 0)
    full2 = lambda i: (0, 0)
    vec = pl.BlockSpec((1, D), full2)
    tok = pl.BlockSpec((tm, D), row)
    tok2 = pl.BlockSpec((tm, 2 * D), row)
    return pl.pallas_call(
        _rwkv_out_kernel,
        out_shape=jax.ShapeDtypeStruct((NTOK, D), F32),
        grid=(NTOK // tm,),
        in_specs=[tok, tok, tok, tok, tok, tok2, tok,
                  pl.BlockSpec((1, N_MOD, D), lambda i: (_cond_row(i * tm), 0, 0)),
                  vec, vec, vec, pl.BlockSpec((D, D), full2), vec, pl.BlockSpec((HSUM_W, HSUM_W), full2)],
        out_specs=tok,
        compiler_params=pltpu.CompilerParams(
            dimension_semantics=("arbitrary",), vmem_limit_bytes=VMEM_LIMIT),
        name="rwkv_out",
    )(x, y_f, y_b, r, v, kd, g, mod, p['gn_w'].reshape(1, D), p['gn_b'].reshape(1, D),
      p['r_k'].reshape(1, D), p['w_o'].astype(BF16), gpost.reshape(1, D), _head_ones())


def _box_matrix(length, win):
    t = np.arange(length)
    lo = np.clip(t - win // 2, 0, length)
    hi = np.clip(t + win - win // 2, 0, length)
    a = ((t[None, :] >= lo[:, None]) & (t[None, :] < hi[:, None])).astype(np.float32)
    return a, (hi - lo).astype(np.float32)


def _pool_constants(grid):
    mats, inv = [], []
    for win in POOL_WINDOWS:
        if grid:
            ar, cr = _box_matrix(DEC_SEQ // GRID_W, win)
            ac, cc = _box_matrix(GRID_W, win)
            mats.append(np.kron(ar, ac))
            inv.append(1.0 / np.kron(cr, cc))
        else:
            a, c = _box_matrix(SEQ, win)
            mats.append(a)
            inv.append(1.0 / c)
    mats = jnp.asarray(np.stack(mats).astype(BF16))
    inv = jnp.asarray(np.stack(inv)[:, :, None], dtype=F32)
    return mats, inv


def _pool_kernel(rs_ref, xg_ref, mod_ref, gpre_ref, a_ref, inv_ref, pw_ref, ps_ref, o_ref, *, t):
    for j in range(xg_ref.shape[0] // t):
        rows = slice(j * t, (j + 1) * t)
        part = (xg_ref[rows, :] * rs_ref[rows, 0:1]) * gpre_ref[...] * (1.0 + mod_ref[0, 4:5, :]) + mod_ref[0, 3:4, :]
        m = _dot(a_ref[0], part) * inv_ref[0]
        o_ref[rows, :] = _dot(m - part, pw_ref[0]) * ps_ref[...]


def _pool(x, rs, mod, gpre, pool_w, pool_scale, t, seq0, nseq, grid, per_step=1):
    mats, inv = _pool_constants(grid)
    gc = POOL_GC
    tb = t * per_step
    seq0 //= per_step
    return pl.pallas_call(
        functools.partial(_pool_kernel, t=t),
        out_shape=jax.ShapeDtypeStruct((nseq * t, D), F32),
        grid=(4, nseq // per_step),
        in_specs=[
            pl.BlockSpec((tb, RS_W), lambda g, s: (seq0 + s, 0)),
            pl.BlockSpec((tb, gc), lambda g, s: (seq0 + s, g)),
            pl.BlockSpec((1, N_MOD, gc), lambda g, s: (_cond_row((seq0 + s) * tb), 0, g)),
            pl.BlockSpec((1, gc), lambda g, s: (0, g)),
            pl.BlockSpec((1, t, t), lambda g, s: (g, 0, 0)),
            pl.BlockSpec((1, t, 1), lambda g, s: (g, 0, 0)),
            pl.BlockSpec((1, gc, gc), lambda g, s: (g, 0, 0)),
            pl.BlockSpec((1, gc), lambda g, s: (0, g)),
        ],
        out_specs=pl.BlockSpec((tb, gc), lambda g, s: (s, g)),
        compiler_params=pltpu.CompilerParams(
            dimension_semantics=("arbitrary", "arbitrary"), vmem_limit_bytes=VMEM_LIMIT),
        name="pool_grid" if grid else "pool_seq",
    )(rs, x, mod, gpre.reshape(1, D), mats, inv, pool_w.astype(BF16), pool_scale.reshape(1, D))


def kernel(x_prompt, x_sample, c, state_ctx_fwd, state_ctx_bwd, c_ctx, w_mod, b_mod, norm_pre, norm_post,
           ffn_w1, ffn_w3, ffn_w2, rwkv_mu, rwkv_w_rkv, rwkv_w0, rwkv_w1, rwkv_w2, rwkv_a0, rwkv_a1,
           rwkv_a2, rwkv_g1, rwkv_g2, rwkv_k_k, rwkv_k_a, rwkv_r_k, rwkv_gn_w, rwkv_gn_b, rwkv_w_o,
           pool_w, pool_scale):
    x = (x_prompt.reshape(N_CTX, D), x_sample.reshape(N_SMP, D))
    cond = jnp.concatenate([c_ctx[None], c, jnp.zeros((COND_ROWS - 1 - DEC_BATCH, D), F32)])
    mod = _modulation(cond, w_mod, b_mod).reshape(DEPTH, COND_ROWS, N_MOD, D)
    wffn = (ffn_w1.astype(BF16), ffn_w3.astype(BF16), ffn_w2.astype(BF16))

    x = _ffn(x, mod[0], norm_pre[0, 0], norm_post[0, 0], wffn, (0, 0), 0)
    p = dict(mu=rwkv_mu[0], w_rkv=rwkv_w_rkv[0], w0=rwkv_w0[0], w1=rwkv_w1[0], w2=rwkv_w2[0],
             a0=rwkv_a0[0], a1=rwkv_a1[0], a2=rwkv_a2[0], g1=rwkv_g1[0], g2=rwkv_g2[0],
             k_k=rwkv_k_k[0], k_a=rwkv_k_a[0], r_k=rwkv_r_k[0], gn_w=rwkv_gn_w[0], gn_b=rwkv_gn_b[0],
             w_o=rwkv_w_o[0])
    r, v, kk, g, lw, kd, b = _rwkv_proj(x, mod[0], norm_pre[0, 1], p)
    y_f, y_b, new_f, new_b = _wkv(r, v, kk, lw, kd, b, state_ctx_fwd[:, 0], state_ctx_bwd[:, 0])
    x = _rwkv_out(x, y_f, y_b, r, v, kd, g, mod[0], p, norm_post[0, 1])
    x = _ffn(x, mod[0], norm_pre[0, 2], norm_post[0, 2], wffn, (0, 1), 6)

    x, rs = _ffn(x, mod[1], norm_pre[1, 0], norm_post[1, 0], wffn, (1, 0), 0, emit_rs=True)
    pc = _pool(x, rs, mod[1], norm_pre[1, 1], pool_w[0], pool_scale[0], SEQ, 0, BATCH, False, per_step=4)
    ps = _pool(x, rs, mod[1], norm_pre[1, 1], pool_w[0], pool_scale[0], DEC_SEQ, N_CTX // DEC_SEQ, DEC_BATCH, True)
    y_ctx, y_smp = _ffn(x, mod[1], norm_pre[1, 2], norm_post[1, 2], wffn, (1, 1), 6,
                        pending=((pc, ps), norm_post[1, 1], 5), split_out=True)

    return (y_ctx.reshape(BATCH, SEQ, D), y_smp.reshape(DEC_BATCH, DEC_SEQ, D), new_f, new_b)
```

```python
import functools

import numpy as np
import jax
import jax.numpy as jnp
from jax import lax
from jax.experimental import pallas as pl
from jax.experimental.pallas import tpu as pltpu

D = 1024
HEAD = 64
NHEAD = D // HEAD
D_FF = 2816
DEPTH = 2
N_MOD = 9
POOL_WINDOWS = (2, 4, 8, 16)
POOL_GC = D // 4
GRID_W = 64
RMS_EPS = 1e-6
GN_EPS = 64e-5

BATCH, SEQ = 32, 256
DEC_BATCH, DEC_SEQ = 8, 2048
N_CTX = BATCH * SEQ
N_SMP = DEC_BATCH * DEC_SEQ
NTOK = N_CTX + N_SMP
COND_ROWS = 16

VMEM_LIMIT = 56 * 1024 * 1024

F32 = jnp.float32
BF16 = jnp.bfloat16


def _dot(a, b):
    return jnp.dot(a.astype(BF16), b.astype(BF16), preferred_element_type=F32)


def _rms(x, g):
    return (x * lax.rsqrt(jnp.mean(x * x, axis=-1, keepdims=True) + RMS_EPS)) * g


def _cond_row(start_row):
    return jnp.where(start_row < N_CTX, 0, 1 + (start_row - N_CTX) // DEC_SEQ)


def _split(x):
    hi = x.astype(BF16)
    return hi, (x - hi.astype(F32)).astype(BF16)


def _split_sum(x, ones):
    hi, lo = _split(x)
    return (jnp.dot(hi, ones, preferred_element_type=F32)
            + jnp.dot(lo, ones, preferred_element_type=F32))


HSUM_W = 256


def _head_sum(x, ones):
    return jnp.concatenate([_split_sum(x[:, i:i + HSUM_W], ones) for i in range(0, D, HSUM_W)], axis=1)


def _split_sum_left(ones, x):
    hi, lo = _split(x)
    return (jnp.dot(ones, hi, preferred_element_type=F32)
            + jnp.dot(ones, lo, preferred_element_type=F32))


def _mod_kernel(c_ref, w_ref, b_ref, o_ref):
    c = c_ref[...]
    o_ref[0] = _dot(jax.nn.silu(c), w_ref[0]) + b_ref[0]


def _modulation(cond, w_mod, b_mod):
    return pl.pallas_call(
        _mod_kernel,
        out_shape=jax.ShapeDtypeStruct((DEPTH, COND_ROWS, N_MOD * D), F32),
        grid=(DEPTH, N_MOD),
        in_specs=[
            pl.BlockSpec((COND_ROWS, D), lambda l, n: (0, 0)),
            pl.BlockSpec((1, D, D), lambda l, n: (l, 0, n)),
            pl.BlockSpec((1, 1, D), lambda l, n: (l, 0, n)),
        ],
        out_specs=pl.BlockSpec((1, COND_ROWS, D), lambda l, n: (l, 0, n)),
        compiler_params=pltpu.CompilerParams(
            dimension_semantics=("arbitrary", "arbitrary"), vmem_limit_bytes=VMEM_LIMIT),
        name="modulation",
    )(cond, w_mod, b_mod.reshape(DEPTH, 1, N_MOD * D))


FFN_ROWS = 256
FFN_TM = 1024
FFN_TM_PENDING = 512


def _tok_operand(x, tm):
    if not isinstance(x, tuple):
        return [x], [pl.BlockSpec((tm, D), lambda i: (i, 0))]
    na = N_CTX // tm
    return list(x), [pl.BlockSpec((tm, D), lambda i: (jnp.minimum(i, na - 1), 0)),
                     pl.BlockSpec((tm, D), lambda i: (jnp.maximum(i - na, 0), 0))]


def _ffn_kernel(*refs, tm, k0, pending_k, n_x, n_d, n_o, emit_rs):
    refs = list(refs)
    take = lambda n: [refs.pop(0) for _ in range(n)]
    x_refs = take(n_x)
    mod_ref, gpre_ref, gpost_ref, w1_ref, w3_ref, w2_ref = take(6)
    d_refs = take(n_d)
    gd_ref = take(1)[0] if n_d else None
    o_refs = take(n_o)
    rs_ref = take(1)[0] if emit_rs else None
    is_ctx = pl.program_id(0) < N_CTX // tm

    def read(rs, rows):
        return rs[0][rows, :] if len(rs) == 1 else jnp.where(is_ctx, rs[0][rows, :], rs[1][rows, :])

    rt = FFN_ROWS
    for s in range(tm // rt):
        rows = slice(s * rt, (s + 1) * rt)
        x = read(x_refs, rows)
        if pending_k is not None:
            x = x + mod_ref[0, pending_k:pending_k + 1, :] * _rms(read(d_refs, rows), gd_ref[...])
        h = _rms(x, gpre_ref[...]) * (1.0 + mod_ref[0, k0 + 1:k0 + 2, :]) + mod_ref[0, k0:k0 + 1, :]
        h = h.astype(BF16)
        u = jnp.dot(h, w1_ref[...], preferred_element_type=F32)
        g = jnp.dot(h, w3_ref[...], preferred_element_type=F32)
        a = (jax.nn.silu(u) * g).astype(BF16)
        f = jnp.dot(a, w2_ref[...], preferred_element_type=F32)
        y = x + 0.5 * mod_ref[0, k0 + 2:k0 + 3, :] * _rms(f, gpost_ref[...])
        if emit_rs:
            rs = lax.rsqrt(jnp.mean(y * y, axis=-1, keepdims=True) + RMS_EPS)
            rs_ref[rows, :] = jnp.broadcast_to(rs, (rt, RS_W))
        if n_o == 1:
            o_refs[0][rows, :] = y
        else:
            @pl.when(is_ctx)
            def _():
                o_refs[0][rows, :] = y

            @pl.when(jnp.logical_not(is_ctx))
            def _():
                o_refs[1][rows, :] = y


RS_W = 128


def _ffn(x, mod, gpre, gpost, weights, wsel, k0, pending=None, split_out=False, emit_rs=False):
    tm = FFN_TM if pending is None else FFN_TM_PENDING
    vec = pl.BlockSpec((1, D), lambda i: (0, 0))
    resident = lambda rows, cols: pl.BlockSpec((None, None, rows, cols), lambda i: (*wsel, 0, 0),
                                               pipeline_mode=pl.Buffered(1))
    x_args, x_specs = _tok_operand(x, tm)
    args = x_args + [mod, gpre.reshape(1, D), gpost.reshape(1, D), *weights]
    in_specs = x_specs + [
        pl.BlockSpec((1, N_MOD, D), lambda i: (_cond_row(i * tm), 0, 0)), vec, vec,
        resident(D, D_FF), resident(D, D_FF), resident(D_FF, D),
    ]
    pending_k, n_d = None, 0
    if pending is not None:
        delta, gd, pending_k = pending
        d_args, d_specs = _tok_operand(delta, tm)
        n_d = len(d_args)
        args += d_args + [gd.reshape(1, D)]
        in_specs += d_specs + [vec]
    if split_out:
        out_shape = [jax.ShapeDtypeStruct((N_CTX, D), F32), jax.ShapeDtypeStruct((N_SMP, D), F32)]
        out_specs = _tok_operand((None, None), tm)[1]
    else:
        out_shape = [jax.ShapeDtypeStruct((NTOK, D), F32)]
        out_specs = _tok_operand(None, tm)[1]
    n_o = len(out_shape)
    if emit_rs:
        out_shape = out_shape + [jax.ShapeDtypeStruct((NTOK, RS_W), F32)]
        out_specs = out_specs + [pl.BlockSpec((tm, RS_W), lambda i: (i, 0))]
    elif n_o == 1:
        out_shape, out_specs = out_shape[0], out_specs[0]
    return pl.pallas_call(
        functools.partial(_ffn_kernel, tm=tm, k0=k0, pending_k=pending_k, n_x=len(x_args), n_d=n_d,
                          n_o=n_o, emit_rs=emit_rs),
        out_shape=out_shape,
        grid=(NTOK // tm,),
        in_specs=in_specs,
        out_specs=out_specs,
        compiler_params=pltpu.CompilerParams(
            dimension_semantics=("arbitrary",), vmem_limit_bytes=VMEM_LIMIT),
        name="ffn",
    )(*args)


RW_TM = 256
HALO = 8


def _rwkv_proj_kernel(x_ref, xp_ref, xn_ref, mod_ref, gpre_ref, mu_ref, wrkv_ref, w1_ref, w2_ref, w0_ref,
                      a1_ref, a2_ref, a0_ref, g1_ref, g2_ref, k_k_ref, k_a_ref, ones_ref,
                      r_out, v_out, kk_out, g_out, lw_out, kd_out, b_out):
    i = pl.program_id(0)
    shift = mod_ref[0, 3:4, :]
    scale = 1.0 + mod_ref[0, 4:5, :]
    gpre = gpre_ref[...]
    premod = lambda x: _rms(x, gpre) * scale + shift

    k = i - N_CTX // RW_TM
    per_seq = DEC_SEQ // RW_TM
    is_ctx = i < N_CTX // RW_TM
    first = jnp.logical_or(is_ctx, k % per_seq == 0)
    last = jnp.logical_or(is_ctx, k % per_seq == per_seq - 1)

    h = premod(x_ref[...])
    hp = jnp.where(first, 0.0, premod(xp_ref[...])[HALO - 1:HALO, :])
    hn = jnp.where(last, 0.0, premod(xn_ref[...])[0:1, :])
    rows = lax.broadcasted_iota(jnp.int32, (RW_TM, 1), 0)
    h_prev = jnp.where(rows == 0, hp, pltpu.roll(h, 1, 0))
    h_next = jnp.where(rows == RW_TM - 1, hn, pltpu.roll(h, RW_TM - 1, 0))
    xx = 0.5 * (h_prev + h_next) - h
    mix = lambda n: h + xx * mu_ref[n:n + 1, :]

    r = _dot(mix(0), wrkv_ref[0])
    kx = _dot(mix(2), wrkv_ref[1])
    v = _dot(mix(3), wrkv_ref[2])
    g = _dot(jax.nn.sigmoid(_dot(mix(5), g1_ref[...])), g2_ref[...])

    z = w0_ref[...] + _dot(jnp.tanh(_dot(mix(1), w1_ref[...])), w2_ref[...])
    a = jax.nn.sigmoid(a0_ref[...] + _dot(_dot(mix(4), a1_ref[...]), a2_ref[...]))

    kkraw = kx * k_k_ref[...]
    nrm = jnp.sqrt(_head_sum(kkraw * kkraw, ones_ref[...]))
    kk = kkraw / jnp.maximum(nrm, 1e-12)

    r_out[...] = r.astype(BF16)
    v_out[...] = v.astype(BF16)
    kk_out[...] = kk.astype(BF16)
    g_out[...] = g.astype(BF16)
    lw_out[...] = -float(np.exp(-0.5)) * jax.nn.sigmoid(z)
    ka = k_a_ref[...]
    for d in range(2):
        a_d = a[:, d * D:(d + 1) * D]
        kd_out[:, d * D:(d + 1) * D] = (kx * (1.0 + (a_d - 1.0) * ka)).astype(BF16)
        b_out[:, d * D:(d + 1) * D] = (kk * a_d).astype(BF16)


def _blockdiag2(m):
    z = jnp.zeros_like(m[0])
    return jnp.concatenate([jnp.concatenate([m[0], z], axis=1), jnp.concatenate([z, m[1]], axis=1)], axis=0)


def _head_ones():
    idx = np.arange(HSUM_W) // HEAD
    return jnp.asarray((idx[:, None] == idx[None, :]).astype(np.float32), dtype=BF16)


def _rwkv_proj(x, mod, gpre, p):
    tm = RW_TM
    nblk = NTOK // HALO
    row = lambda i: (i, 0)
    full2 = lambda i: (0, 0)
    full3 = lambda i: (0, 0, 0)
    w1c = jnp.concatenate([p['w1'][0], p['w1'][1]], axis=1).astype(BF16)
    a1c = jnp.concatenate([p['a1'][0], p['a1'][1]], axis=1).astype(BF16)
    w2bd = _blockdiag2(p['w2']).astype(BF16)
    a2bd = _blockdiag2(p['a2']).astype(BF16)
    return pl.pallas_call(
        _rwkv_proj_kernel,
        out_shape=([jax.ShapeDtypeStruct((NTOK, D), BF16)] * 4 + [jax.ShapeDtypeStruct((NTOK, 2 * D), F32)]
                   + [jax.ShapeDtypeStruct((NTOK, 2 * D), BF16)] * 2),
        grid=(NTOK // tm,),
        in_specs=[
            pl.BlockSpec((tm, D), row),
            pl.BlockSpec((HALO, D), lambda i: (jnp.maximum(i * (tm // HALO) - 1, 0), 0)),
            pl.BlockSpec((HALO, D), lambda i: (jnp.minimum((i + 1) * (tm // HALO), nblk - 1), 0)),
            pl.BlockSpec((1, N_MOD, D), lambda i: (_cond_row(i * tm), 0, 0)),
            pl.BlockSpec((1, D), full2),
            pl.BlockSpec((6, D), full2),
            pl.BlockSpec((3, D, D), full3),
            pl.BlockSpec((D, 128), full2),
            pl.BlockSpec((128, 2 * D), full2),
            pl.BlockSpec((1, 2 * D), full2),
            pl.BlockSpec((D, 128), full2),
            pl.BlockSpec((128, 2 * D), full2),
            pl.BlockSpec((1, 2 * D), full2),
            pl.BlockSpec((D, 128), full2),
            pl.BlockSpec((128, D), full2),
            pl.BlockSpec((1, D), full2),
            pl.BlockSpec((1, D), full2),
            pl.BlockSpec((HSUM_W, HSUM_W), full2),
        ],
        out_specs=[pl.BlockSpec((tm, D), row)] * 4 + [pl.BlockSpec((tm, 2 * D), row)] * 3,
        compiler_params=pltpu.CompilerParams(
            dimension_semantics=("arbitrary",), vmem_limit_bytes=VMEM_LIMIT),
        name="rwkv_proj",
    )(x, x, x, mod, gpre.reshape(1, D), p['mu'], p['w_rkv'].astype(BF16), w1c, w2bd,
      p['w0'].reshape(1, 2 * D), a1c, a2bd, p['a0'].reshape(1, 2 * D),
      p['g1'].astype(BF16), p['g2'].astype(BF16), p['k_k'].reshape(1, D), p['k_a'].reshape(1, D),
      _head_ones())


CH = 32
SUB = 8
GH = 2
GW = GH * HEAD
NG = NHEAD // GH
SR = GH * CH

_NT = (((1,), (1,)), ((), ()))
_TN = (((0,), (0,)), ((), ()))


def _scan_masks():
    rho = np.arange(SR)
    same = (rho[:, None] // CH) == (rho[None, :] // CH)
    lower = rho[None, :] < rho[:, None]
    strict = np.stack([same & lower, same & lower.T])
    incl = strict | np.eye(SR, dtype=bool)[None]
    t = np.arange(CH)
    tri = np.stack([t[None, :] <= t[:, None], t[None, :] >= t[:, None]])
    return (jnp.asarray(strict.astype(np.float32)), jnp.asarray(incl.astype(np.float32)),
            jnp.asarray(tri.astype(np.float32), dtype=BF16))


def _wkv_kernel(rf_ref, vf_ref, kkf_ref, lwf_ref, kdf_ref, bf_ref, rb_ref, vb_ref, kkb_ref, lwb_ref, kdb_ref,
                bb_ref, strict_ref, incl_ref, tri_ref, s0f_ref, s0b_ref, yf_ref, yb_ref, sfinf_ref, sfinb_ref,
                s_ref,
                *, n_ctx_chunks, ctx_chunks_per_seq, smp_chunks_per_seq):
    q = pl.program_id(0)
    is_ctx = q < n_ctx_chunks
    c = jnp.where(is_ctx, q % ctx_chunks_per_seq, (q - n_ctx_chunks) % smp_chunks_per_seq)

    @pl.when(c == 0)
    def _():
        zero = jnp.zeros((HEAD, HEAD), F32)
        for d, s0_ref in enumerate((s0f_ref, s0b_ref)):
            for g in range(NG):
                bd = jnp.concatenate(
                    [jnp.concatenate([s0_ref[0, g * GH + h] if hh == h else zero for hh in range(GH)], axis=1)
                     for h in range(GH)], axis=0)
                s_ref[d, g] = jnp.where(is_ctx, 0.0, bd)

    def decayed(d, a, r_ref, v_ref, kk_ref, lw_ref, kd_ref, b_ref):
        pos = a if d == 0 else SUB - 1 - a
        rows = slice(pos * CH, (pos + 1) * CH)
        lw = lw_ref[rows, :]
        cum = _split_sum_left(tri_ref[d], lw)
        tot = cum[CH - 1:CH, :] if d == 0 else cum[0:1, :]
        e_neg = jnp.exp(-cum)
        e_rem = jnp.exp(tot - cum)
        kd = kd_ref[rows, :]
        bb = b_ref[rows, :]
        return (kk_ref[rows, :] * jnp.exp(cum - lw), r_ref[rows, :] * jnp.exp(cum), kd * e_neg, bb * e_neg,
                v_ref[rows, :], kd * e_rem, bb * e_rem), jnp.exp(tot), rows

    refs_d = ((rf_ref, vf_ref, kkf_ref, lwf_ref, kdf_ref, bf_ref),
              (rb_ref, vb_ref, kkb_ref, lwb_ref, kdb_ref, bb_ref))
    tiles = {(a, d): decayed(d, a, *refs_d[d]) for a in range(SUB) for d in range(2)}
    y_refs = (yf_ref, yb_ref)
    strict_d = [strict_ref[d] > 0.5 for d in range(2)]
    incl_d = [incl_ref[d] > 0.5 for d in range(2)]

    lane_head = lax.broadcasted_iota(jnp.int32, (CH, GW), 1) // HEAD
    head_lanes = [lane_head == h for h in range(GH)]

    def stack(x):
        return jnp.concatenate([jnp.where(m, x, 0.0) for m in head_lanes], axis=0).astype(BF16)

    eye = (lax.broadcasted_iota(jnp.int32, (SR, SR), 0)
           == lax.broadcasted_iota(jnp.int32, (SR, SR), 1)).astype(F32)
    dot = functools.partial(jnp.dot, preferred_element_type=F32)
    dotg = functools.partial(lax.dot_general, preferred_element_type=F32)

    dg = [(d, g) for g in range(NG) for d in range(2)]
    sl = {g: slice(g * GW, (g + 1) * GW) for g in range(NG)}
    pre = {}
    state = {c_: s_ref[c_[0], c_[1]] for c_ in dg}

    def independent(chunks):
        adg = [(a, d, g) for a in chunks for d, g in dg]
        ops = {(a, d, g): [stack(x[:, sl[g]]) for x in tiles[(a, d)][0]] for a, d, g in adg}
        qr = {c_: jnp.concatenate([o[0], o[1]], axis=0) for c_, o in ops.items()}
        gm = {c_: dotg(qr[c_], jnp.concatenate([o[3], o[2]], axis=0), _NT) for c_, o in ops.items()}
        yield
        l = {c_: jnp.where(strict_d[c_[1]], gm[c_][:SR, :SR], 0.0) for c_ in adg}
        m = {c_: jnp.where(strict_d[c_[1]], gm[c_][:SR, SR:], 0.0).astype(BF16) for c_ in adg}
        ppb = {c_: jnp.concatenate([jnp.where(incl_d[c_[1]], gm[c_][SR:, SR:], 0.0),
                                    jnp.where(incl_d[c_[1]], gm[c_][SR:, :SR], 0.0)], axis=1).astype(BF16)
               for c_ in adg}
        mv = {c_: dot(m[c_], ops[c_][4]) for c_ in adg}
        yield
        n = {c_: eye - l[c_] for c_ in adg}
        pw = {c_: l[c_].astype(BF16) for c_ in adg}
        for _ in range(CH.bit_length() - 2):
            pw = {c_: dot(pw[c_], pw[c_]).astype(BF16) for c_ in adg}
            yield
            n = {c_: n[c_] + dot(n[c_].astype(BF16), pw[c_]) for c_ in adg}
            yield
        for a in chunks:
            of = lambda t_: {c_: t_[(a,) + c_] for c_ in dg}
            pre[a] = dict(qr=of(qr), mv=of(mv), ppb=of(ppb), tinv={c_: n[(a,) + c_].astype(BF16) for c_ in dg},
                          vs={c_: ops[(a,) + c_][4] for c_ in dg},
                          khbh={c_: jnp.concatenate([ops[(a,) + c_][5], ops[(a,) + c_][6]], axis=0) for c_ in dg})

    def dependent(a):
        p_ = pre[a]
        s0 = dict(state)
        zz = {c_: dotg(p_['qr'][c_], s0[c_].astype(BF16), _NT) for c_ in dg}
        yield
        u = {c_: dot(p_['tinv'][c_], (zz[c_][:SR] + p_['mv'][c_]).astype(BF16)) for c_ in dg}
        yield
        vu = {c_: jnp.concatenate([p_['vs'][c_], (-u[c_]).astype(BF16)], axis=0) for c_ in dg}
        ys = {c_: zz[c_][SR:] + dot(p_['ppb'][c_], vu[c_]) for c_ in dg}
        yield
        for d, g in dg:
            state[(d, g)] = s0[(d, g)] * tiles[(a, d)][1][:, sl[g]] + dotg(vu[(d, g)], p_['khbh'][(d, g)], _TN)
        for d, g in dg:
            y = ys[(d, g)][0:CH]
            for h in range(1, GH):
                y = y + ys[(d, g)][h * CH:(h + 1) * CH]
            y_refs[d][tiles[(a, d)][2], sl[g]] = y.astype(BF16)

    def run(*gens):
        gens = list(gens)
        while gens:
            for gen in list(gens):
                if next(gen, gens) is gens:
                    gens.remove(gen)

    run(independent(tuple(range(SUB))))
    for a in range(SUB):
        run(dependent(a))
    for (d, g), s in state.items():
        s_ref[d, g] = s

    @pl.when(jnp.logical_and(is_ctx, c == ctx_chunks_per_seq - 1))
    def _():
        for (d, g), s in state.items():
            for h in range(GH):
                (sfinf_ref, sfinb_ref)[d][0, 0, g * GH + h] = s[h * HEAD:(h + 1) * HEAD, h * HEAD:(h + 1) * HEAD]


def _wkv(r, v, kk, lw, kd, b, state_f, state_b, n_ctx_seq=BATCH, t_ctx=SEQ, n_smp_seq=DEC_BATCH, t_smp=DEC_SEQ):
    ntok = n_ctx_seq * t_ctx + n_smp_seq * t_smp
    blk = SUB * CH
    cps_c, cps_s = t_ctx // blk, t_smp // blk
    ncc = n_ctx_seq * cps_c
    nq = ncc + n_smp_seq * cps_s

    def seq_of(q):
        return jnp.where(q < ncc, q // cps_c, n_ctx_seq + (q - ncc) // cps_s)

    def row_block(d, q):
        ctx = q < ncc
        cps = jnp.where(ctx, cps_c, cps_s)
        c = jnp.where(ctx, q % cps_c, (q - ncc) % cps_s)
        return q - c + (c if d == 0 else cps - 1 - c)

    tok = [pl.BlockSpec((blk, D), functools.partial(lambda q, d: (row_block(d, q), 0), d=d)) for d in range(2)]
    tokd = [pl.BlockSpec((blk, D), functools.partial(lambda q, d: (row_block(d, q), d), d=d)) for d in range(2)]
    msk = pl.BlockSpec((2, SR, SR), lambda q: (0, 0, 0))
    tri = pl.BlockSpec((2, CH, CH), lambda q: (0, 0, 0))
    s0spec = pl.BlockSpec((1, NHEAD, HEAD, HEAD), lambda q: (jnp.maximum(seq_of(q) - n_ctx_seq, 0), 0, 0, 0))
    sspec = pl.BlockSpec((1, 1, NHEAD, HEAD, HEAD),
                         lambda q: (jnp.minimum(seq_of(q), n_ctx_seq - 1), 0, 0, 0, 0))
    sfin = jax.ShapeDtypeStruct((n_ctx_seq, 1, NHEAD, HEAD, HEAD), F32)

    strict, incl, trim = _scan_masks()
    return pl.pallas_call(
        functools.partial(_wkv_kernel, n_ctx_chunks=ncc, ctx_chunks_per_seq=cps_c, smp_chunks_per_seq=cps_s),
        out_shape=[jax.ShapeDtypeStruct((ntok, D), BF16), jax.ShapeDtypeStruct((ntok, D), BF16), sfin, sfin],
        grid=(nq,),
        in_specs=[tok[0], tok[0], tok[0], tokd[0], tokd[0], tokd[0],
                  tok[1], tok[1], tok[1], tokd[1], tokd[1], tokd[1], msk, msk, tri, s0spec, s0spec],
        out_specs=[tok[0], tok[1], sspec, sspec],
        scratch_shapes=[pltpu.VMEM((2, NG, GW, GW), F32)],
        compiler_params=pltpu.CompilerParams(
            dimension_semantics=("arbitrary",), vmem_limit_bytes=VMEM_LIMIT),
        name="wkv_scan",
    )(r, v, kk, lw, kd, b, r, v, kk, lw, kd, b, strict, incl, trim, state_f, state_b)


RO_TM = 512


def _rwkv_out_kernel(x_ref, yf_ref, yb_ref, r_ref, v_ref, kd_ref, g_ref, mod_ref, gnw_ref, gnb_ref,
                     rk_ref, wo_ref, gpost_ref, ones_ref, o_ref):
    ones = ones_ref[...]
    ys = yf_ref[...] + yb_ref[...]
    mean = _head_sum(ys, ones) * (1.0 / HEAD)
    c = ys - mean
    var = _head_sum(c * c, ones) * (1.0 / HEAD)
    yn = c * lax.rsqrt(var + GN_EPS) * gnw_ref[...] + gnb_ref[...]
    kd = kd_ref[...]
    bonus = _head_sum(r_ref[...] * (kd[:, :D] + kd[:, D:]) * rk_ref[...], ones) * v_ref[...]
    out = _dot((yn + bonus) * g_ref[...], wo_ref[...])
    o_ref[...] = x_ref[...] + mod_ref[0, 5:6, :] * _rms(out, gpost_ref[...])


def _rwkv_out(x, y_f, y_b, r, v, kd, g, mod, p, gpost):
    tm = RO_TM
    row = lambda i: (i, 0)
    full2 = lambda i: (0, 0)
    vec = pl.BlockSpec((1, D), full2)
    tok = pl.BlockSpec((tm, D), row)
    tok2 = pl.BlockSpec((tm, 2 * D), row)
    return pl.pallas_call(
        _rwkv_out_kernel,
        out_shape=jax.ShapeDtypeStruct((NTOK, D), F32),
        grid=(NTOK // tm,),
        in_specs=[tok, tok, tok, tok, tok, tok2, tok,
                  pl.BlockSpec((1, N_MOD, D), lambda i: (_cond_row(i * tm), 0, 0)),
                  vec, vec, vec, pl.BlockSpec((D, D), full2), vec, pl.BlockSpec((HSUM_W, HSUM_W), full2)],
        out_specs=tok,
        compiler_params=pltpu.CompilerParams(
            dimension_semantics=("arbitrary",), vmem_limit_bytes=VMEM_LIMIT),
        name="rwkv_out",
    )(x, y_f, y_b, r, v, kd, g, mod, p['gn_w'].reshape(1, D), p['gn_b'].reshape(1, D),
      p['r_k'].reshape(1, D), p['w_o'].astype(BF16), gpost.reshape(1, D), _head_ones())


def _box_matrix(length, win):
    t = np.arange(length)
    lo = np.clip(t - win // 2, 0, length)
    hi = np.clip(t + win - win // 2, 0, length)
    a = ((t[None, :] >= lo[:, None]) & (t[None, :] < hi[:, None])).astype(np.float32)
    return a, (hi - lo).astype(np.float32)


def _pool_constants(grid):
    mats, inv = [], []
    for win in POOL_WINDOWS:
        if grid:
            ar, cr = _box_matrix(DEC_SEQ // GRID_W, win)
            ac, cc = _box_matrix(GRID_W, win)
            mats.append(np.kron(ar, ac))
            inv.append(1.0 / np.kron(cr, cc))
        else:
            a, c = _box_matrix(SEQ, win)
            mats.append(a)
            inv.append(1.0 / c)
    mats = jnp.asarray(np.stack(mats).astype(BF16))
    inv = jnp.asarray(np.stack(inv)[:, :, None], dtype=F32)
    return mats, inv


def _pool_kernel(rs_ref, xg_ref, mod_ref, gpre_ref, a_ref, inv_ref, pw_ref, ps_ref, o_ref, *, t):
    for j in range(xg_ref.shape[0] // t):
        rows = slice(j * t, (j + 1) * t)
        part = (xg_ref[rows, :] * rs_ref[rows, 0:1]) * gpre_ref[...] * (1.0 + mod_ref[0, 4:5, :]) + mod_ref[0, 3:4, :]
        m = _dot(a_ref[0], part) * inv_ref[0]
        o_ref[rows, :] = _dot(m - part, pw_ref[0]) * ps_ref[...]


def _pool(x, rs, mod, gpre, pool_w, pool_scale, t, seq0, nseq, grid, per_step=1):
    mats, inv = _pool_constants(grid)
    gc = POOL_GC
    tb = t * per_step
    seq0 //= per_step
    return pl.pallas_call(
        functools.partial(_pool_kernel, t=t),
        out_shape=jax.ShapeDtypeStruct((nseq * t, D), F32),
        grid=(4, nseq // per_step),
        in_specs=[
            pl.BlockSpec((tb, RS_W), lambda g, s: (seq0 + s, 0)),
            pl.BlockSpec((tb, gc), lambda g, s: (seq0 + s, g)),
            pl.BlockSpec((1, N_MOD, gc), lambda g, s: (_cond_row((seq0 + s) * tb), 0, g)),
            pl.BlockSpec((1, gc), lambda g, s: (0, g)),
            pl.BlockSpec((1, t, t), lambda g, s: (g, 0, 0)),
            pl.BlockSpec((1, t, 1), lambda g, s: (g, 0, 0)),
            pl.BlockSpec((1, gc, gc), lambda g, s: (g, 0, 0)),
            pl.BlockSpec((1, gc), lambda g, s: (0, g)),
        ],
        out_specs=pl.BlockSpec((tb, gc), lambda g, s: (s, g)),
        compiler_params=pltpu.CompilerParams(
            dimension_semantics=("arbitrary", "arbitrary"), vmem_limit_bytes=VMEM_LIMIT),
        name="pool_grid" if grid else "pool_seq",
    )(rs, x, mod, gpre.reshape(1, D), mats, inv, pool_w.astype(BF16), pool_scale.reshape(1, D))


def kernel(x_prompt, x_sample, c, state_ctx_fwd, state_ctx_bwd, c_ctx, w_mod, b_mod, norm_pre, norm_post,
           ffn_w1, ffn_w3, ffn_w2, rwkv_mu, rwkv_w_rkv, rwkv_w0, rwkv_w1, rwkv_w2, rwkv_a0, rwkv_a1,
           rwkv_a2, rwkv_g1, rwkv_g2, rwkv_k_k, rwkv_k_a, rwkv_r_k, rwkv_gn_w, rwkv_gn_b, rwkv_w_o,
           pool_w, pool_scale):
    x = (x_prompt.reshape(N_CTX, D), x_sample.reshape(N_SMP, D))
    cond = jnp.concatenate([c_ctx[None], c, jnp.zeros((COND_ROWS - 1 - DEC_BATCH, D), F32)])
    mod = _modulation(cond, w_mod, b_mod).reshape(DEPTH, COND_ROWS, N_MOD, D)
    wffn = (ffn_w1.astype(BF16), ffn_w3.astype(BF16), ffn_w2.astype(BF16))

    x = _ffn(x, mod[0], norm_pre[0, 0], norm_post[0, 0], wffn, (0, 0), 0)
    p = dict(mu=rwkv_mu[0], w_rkv=rwkv_w_rkv[0], w0=rwkv_w0[0], w1=rwkv_w1[0], w2=rwkv_w2[0],
             a0=rwkv_a0[0], a1=rwkv_a1[0], a2=rwkv_a2[0], g1=rwkv_g1[0], g2=rwkv_g2[0],
             k_k=rwkv_k_k[0], k_a=rwkv_k_a[0], r_k=rwkv_r_k[0], gn_w=rwkv_gn_w[0], gn_b=rwkv_gn_b[0],
             w_o=rwkv_w_o[0])
    r, v, kk, g, lw, kd, b = _rwkv_proj(x, mod[0], norm_pre[0, 1], p)
    y_f, y_b, new_f, new_b = _wkv(r, v, kk, lw, kd, b, state_ctx_fwd[:, 0], state_ctx_bwd[:, 0])
    x = _rwkv_out(x, y_f, y_b, r, v, kd, g, mod[0], p, norm_post[0, 1])
    x = _ffn(x, mod[0], norm_pre[0, 2], norm_post[0, 2], wffn, (0, 1), 6)

    x, rs = _ffn(x, mod[1], norm_pre[1, 0], norm_post[1, 0], wffn, (1, 0), 0, emit_rs=True)
    pc = _pool(x, rs, mod[1], norm_pre[1, 1], pool_w[0], pool_scale[0], SEQ, 0, BATCH, False, per_step=4)
    ps = _pool(x, rs, mod[1], norm_pre[1, 1], pool_w[0], pool_scale[0], DEC_SEQ, N_CTX // DEC_SEQ, DEC_BATCH, True)
    y_ctx, y_smp = _ffn(x, mod[1], norm_pre[1, 2], norm_post[1, 2], wffn, (1, 1), 6,
                        pending=((pc, ps), norm_post[1, 1], 5), split_out=True)

    return (y_ctx.reshape(BATCH, SEQ, D), y_smp.reshape(DEC_BATCH, DEC_SEQ, D), new_f, new_b)
```

```python
import functools

import numpy as np
import jax
import jax.numpy as jnp
from jax import lax
from jax.experimental import pallas as pl
from jax.experimental.pallas import tpu as pltpu

D = 1024
HEAD = 64
NHEAD = D // HEAD
D_FF = 2816
DEPTH = 2
N_MOD = 9
POOL_WINDOWS = (2, 4, 8, 16)
POOL_GC = D // 4
GRID_W = 64
RMS_EPS = 1e-6
GN_EPS = 64e-5

BATCH, SEQ = 32, 256
DEC_BATCH, DEC_SEQ = 8, 2048
N_CTX = BATCH * SEQ
N_SMP = DEC_BATCH * DEC_SEQ
NTOK = N_CTX + N_SMP
COND_ROWS = 16

VMEM_LIMIT = 56 * 1024 * 1024

F32 = jnp.float32
BF16 = jnp.bfloat16


def _dot(a, b):
    return jnp.dot(a.astype(BF16), b.astype(BF16), preferred_element_type=F32)


def _rms(x, g):
    return (x * lax.rsqrt(jnp.mean(x * x, axis=-1, keepdims=True) + RMS_EPS)) * g


def _cond_row(start_row):
    return jnp.where(start_row < N_CTX, 0, 1 + (start_row - N_CTX) // DEC_SEQ)


def _split(x):
    hi = x.astype(BF16)
    return hi, (x - hi.astype(F32)).astype(BF16)


def _split_sum(x, ones):
    hi, lo = _split(x)
    return (jnp.dot(hi, ones, preferred_element_type=F32)
            + jnp.dot(lo, ones, preferred_element_type=F32))


HSUM_W = 256


def _head_sum(x, ones):
    return jnp.concatenate([_split_sum(x[:, i:i + HSUM_W], ones) for i in range(0, D, HSUM_W)], axis=1)


def _split_sum_left(ones, x):
    hi, lo = _split(x)
    return (jnp.dot(ones, hi, preferred_element_type=F32)
            + jnp.dot(ones, lo, preferred_element_type=F32))


def _mod_kernel(c_ref, w_ref, b_ref, o_ref):
    c = c_ref[...]
    o_ref[0] = _dot(jax.nn.silu(c), w_ref[0]) + b_ref[0]


def _modulation(cond, w_mod, b_mod):
    return pl.pallas_call(
        _mod_kernel,
        out_shape=jax.ShapeDtypeStruct((DEPTH, COND_ROWS, N_MOD * D), F32),
        grid=(DEPTH, N_MOD),
        in_specs=[
            pl.BlockSpec((COND_ROWS, D), lambda l, n: (0, 0)),
            pl.BlockSpec((1, D, D), lambda l, n: (l, 0, n)),
            pl.BlockSpec((1, 1, D), lambda l, n: (l, 0, n)),
        ],
        out_specs=pl.BlockSpec((1, COND_ROWS, D), lambda l, n: (l, 0, n)),
        compiler_params=pltpu.CompilerParams(
            dimension_semantics=("arbitrary", "arbitrary"), vmem_limit_bytes=VMEM_LIMIT),
        name="modulation",
    )(cond, w_mod, b_mod.reshape(DEPTH, 1, N_MOD * D))


FFN_ROWS = 256
FFN_TM = 1024
FFN_TM_PENDING = 512


def _tok_operand(x, tm):
    if not isinstance(x, tuple):
        return [x], [pl.BlockSpec((tm, D), lambda i: (i, 0))]
    na = N_CTX // tm
    return list(x), [pl.BlockSpec((tm, D), lambda i: (jnp.minimum(i, na - 1), 0)),
                     pl.BlockSpec((tm, D), lambda i: (jnp.maximum(i - na, 0), 0))]


def _ffn_kernel(*refs, tm, k0, pending_k, n_x, n_d, n_o, emit_rs):
    refs = list(refs)
    take = lambda n: [refs.pop(0) for _ in range(n)]
    x_refs = take(n_x)
    mod_ref, gpre_ref, gpost_ref, w1_ref, w3_ref, w2_ref = take(6)
    d_refs = take(n_d)
    gd_ref = take(1)[0] if n_d else None
    o_refs = take(n_o)
    rs_ref = take(1)[0] if emit_rs else None
    is_ctx = pl.program_id(0) < N_CTX // tm

    def read(rs, rows):
        return rs[0][rows, :] if len(rs) == 1 else jnp.where(is_ctx, rs[0][rows, :], rs[1][rows, :])

    rt = FFN_ROWS
    for s in range(tm // rt):
        rows = slice(s * rt, (s + 1) * rt)
        x = read(x_refs, rows)
        if pending_k is not None:
            x = x + mod_ref[0, pending_k:pending_k + 1, :] * _rms(read(d_refs, rows), gd_ref[...])
        h = _rms(x, gpre_ref[...]) * (1.0 + mod_ref[0, k0 + 1:k0 + 2, :]) + mod_ref[0, k0:k0 + 1, :]
        h = h.astype(BF16)
        u = jnp.dot(h, w1_ref[...], preferred_element_type=F32)
        g = jnp.dot(h, w3_ref[...], preferred_element_type=F32)
        a = (jax.nn.silu(u) * g).astype(BF16)
        f = jnp.dot(a, w2_ref[...], preferred_element_type=F32)
        y = x + 0.5 * mod_ref[0, k0 + 2:k0 + 3, :] * _rms(f, gpost_ref[...])
        if emit_rs:
            rs = lax.rsqrt(jnp.mean(y * y, axis=-1, keepdims=True) + RMS_EPS)
            rs_ref[rows, :] = jnp.broadcast_to(rs, (rt, RS_W))
        if n_o == 1:
            o_refs[0][rows, :] = y
        else:
            @pl.when(is_ctx)
            def _():
                o_refs[0][rows, :] = y

            @pl.when(jnp.logical_not(is_ctx))
            def _():
                o_refs[1][rows, :] = y


RS_W = 128


def _ffn(x, mod, gpre, gpost, weights, wsel, k0, pending=None, split_out=False, emit_rs=False):
    tm = FFN_TM if pending is None else FFN_TM_PENDING
    vec = pl.BlockSpec((1, D), lambda i: (0, 0))
    resident = lambda rows, cols: pl.BlockSpec((None, None, rows, cols), lambda i: (*wsel, 0, 0),
                                               pipeline_mode=pl.Buffered(1))
    x_args, x_specs = _tok_operand(x, tm)
    args = x_args + [mod, gpre.reshape(1, D), gpost.reshape(1, D), *weights]
    in_specs = x_specs + [
        pl.BlockSpec((1, N_MOD, D), lambda i: (_cond_row(i * tm), 0, 0)), vec, vec,
        resident(D, D_FF), resident(D, D_FF), resident(D_FF, D),
    ]
    pending_k, n_d = None, 0
    if pending is not None:
        delta, gd, pending_k = pending
        d_args, d_specs = _tok_operand(delta, tm)
        n_d = len(d_args)
        args += d_args + [gd.reshape(1, D)]
        in_specs += d_specs + [vec]
    if split_out:
        out_shape = [jax.ShapeDtypeStruct((N_CTX, D), F32), jax.ShapeDtypeStruct((N_SMP, D), F32)]
        out_specs = _tok_operand((None, None), tm)[1]
    else:
        out_shape = [jax.ShapeDtypeStruct((NTOK, D), F32)]
        out_specs = _tok_operand(None, tm)[1]
    n_o = len(out_shape)
    if emit_rs:
        out_shape = out_shape + [jax.ShapeDtypeStruct((NTOK, RS_W), F32)]
        out_specs = out_specs + [pl.BlockSpec((tm, RS_W), lambda i: (i, 0))]
    elif n_o == 1:
        out_shape, out_specs = out_shape[0], out_specs[0]
    return pl.pallas_call(
        functools.partial(_ffn_kernel, tm=tm, k0=k0, pending_k=pending_k, n_x=len(x_args), n_d=n_d,
                          n_o=n_o, emit_rs=emit_rs),
        out_shape=out_shape,
        grid=(NTOK // tm,),
        in_specs=in_specs,
        out_specs=out_specs,
        compiler_params=pltpu.CompilerParams(
            dimension_semantics=("arbitrary",), vmem_limit_bytes=VMEM_LIMIT),
        name="ffn",
    )(*args)


RW_TM = 512
HALO = 8


def _rwkv_proj_kernel(x_ref, xp_ref, xn_ref, mod_ref, gpre_ref, mu_ref, wrkv_ref, w1_ref, w2_ref, w0_ref,
                      a1_ref, a2_ref, a0_ref, g1_ref, g2_ref, k_k_ref, k_a_ref, ones_ref,
                      r_out, v_out, kk_out, g_out, lw_out, kd_out, b_out):
    i = pl.program_id(0)
    shift = mod_ref[0, 3:4, :]
    scale = 1.0 + mod_ref[0, 4:5, :]
    gpre = gpre_ref[...]
    premod = lambda x: _rms(x, gpre) * scale + shift

    k = i - N_CTX // RW_TM
    per_seq = DEC_SEQ // RW_TM
    is_ctx = i < N_CTX // RW_TM
    first = jnp.logical_or(is_ctx, k % per_seq == 0)
    last = jnp.logical_or(is_ctx, k % per_seq == per_seq - 1)

    h = premod(x_ref[...])
    hp = jnp.where(first, 0.0, premod(xp_ref[...])[HALO - 1:HALO, :])
    hn = jnp.where(last, 0.0, premod(xn_ref[...])[0:1, :])
    rows = lax.broadcasted_iota(jnp.int32, (RW_TM, 1), 0)
    h_prev = jnp.where(rows == 0, hp, pltpu.roll(h, 1, 0))
    h_next = jnp.where(rows == RW_TM - 1, hn, pltpu.roll(h, RW_TM - 1, 0))
    if RW_TM > SEQ:
        h_prev = jnp.where(jnp.logical_and(is_ctx, rows % SEQ == 0), 0.0, h_prev)
        h_next = jnp.where(jnp.logical_and(is_ctx, rows % SEQ == SEQ - 1), 0.0, h_next)
    xx = 0.5 * (h_prev + h_next) - h
    mix = lambda n: h + xx * mu_ref[n:n + 1, :]

    r = _dot(mix(0), wrkv_ref[0])
    kx = _dot(mix(2), wrkv_ref[1])
    v = _dot(mix(3), wrkv_ref[2])
    g = _dot(jax.nn.sigmoid(_dot(mix(5), g1_ref[...])), g2_ref[...])

    z = w0_ref[...] + _dot(jnp.tanh(_dot(mix(1), w1_ref[...])), w2_ref[...])
    a = jax.nn.sigmoid(a0_ref[...] + _dot(_dot(mix(4), a1_ref[...]), a2_ref[...]))

    kkraw = kx * k_k_ref[...]
    nrm = jnp.sqrt(_head_sum(kkraw * kkraw, ones_ref[...]))
    kk = kkraw / jnp.maximum(nrm, 1e-12)

    r_out[...] = r.astype(BF16)
    v_out[...] = v.astype(BF16)
    kk_out[...] = kk.astype(BF16)
    g_out[...] = g.astype(BF16)
    lw_out[...] = -float(np.exp(-0.5)) * jax.nn.sigmoid(z)
    ka = k_a_ref[...]
    for d in range(2):
        a_d = a[:, d * D:(d + 1) * D]
        kd_out[:, d * D:(d + 1) * D] = (kx * (1.0 + (a_d - 1.0) * ka)).astype(BF16)
        b_out[:, d * D:(d + 1) * D] = (kk * a_d).astype(BF16)


def _blockdiag2(m):
    z = jnp.zeros_like(m[0])
    return jnp.concatenate([jnp.concatenate([m[0], z], axis=1), jnp.concatenate([z, m[1]], axis=1)], axis=0)


def _head_ones():
    idx = np.arange(HSUM_W) // HEAD
    return jnp.asarray((idx[:, None] == idx[None, :]).astype(np.float32), dtype=BF16)


def _rwkv_proj(x, mod, gpre, p):
    tm = RW_TM
    nblk = NTOK // HALO
    row = lambda i: (i, 0)
    full2 = lambda i: (0, 0)
    full3 = lambda i: (0, 0, 0)
    w1c = jnp.concatenate([p['w1'][0], p['w1'][1]], axis=1).astype(BF16)
    a1c = jnp.concatenate([p['a1'][0], p['a1'][1]], axis=1).astype(BF16)
    w2bd = _blockdiag2(p['w2']).astype(BF16)
    a2bd = _blockdiag2(p['a2']).astype(BF16)
    return pl.pallas_call(
        _rwkv_proj_kernel,
        out_shape=([jax.ShapeDtypeStruct((NTOK, D), BF16)] * 4 + [jax.ShapeDtypeStruct((NTOK, 2 * D), F32)]
                   + [jax.ShapeDtypeStruct((NTOK, 2 * D), BF16)] * 2),
        grid=(NTOK // tm,),
        in_specs=[
            pl.BlockSpec((tm, D), row),
            pl.BlockSpec((HALO, D), lambda i: (jnp.maximum(i * (tm // HALO) - 1, 0), 0)),
            pl.BlockSpec((HALO, D), lambda i: (jnp.minimum((i + 1) * (tm // HALO), nblk - 1), 0)),
            pl.BlockSpec((1, N_MOD, D), lambda i: (_cond_row(i * tm), 0, 0)),
            pl.BlockSpec((1, D), full2),
            pl.BlockSpec((6, D), full2),
            pl.BlockSpec((3, D, D), full3),
            pl.BlockSpec((D, 128), full2),
            pl.BlockSpec((128, 2 * D), full2),
            pl.BlockSpec((1, 2 * D), full2),
            pl.BlockSpec((D, 128), full2),
            pl.BlockSpec((128, 2 * D), full2),
            pl.BlockSpec((1, 2 * D), full2),
            pl.BlockSpec((D, 128), full2),
            pl.BlockSpec((128, D), full2),
            pl.BlockSpec((1, D), full2),
            pl.BlockSpec((1, D), full2),
            pl.BlockSpec((HSUM_W, HSUM_W), full2),
        ],
        out_specs=[pl.BlockSpec((tm, D), row)] * 4 + [pl.BlockSpec((tm, 2 * D), row)] * 3,
        compiler_params=pltpu.CompilerParams(
            dimension_semantics=("arbitrary",), vmem_limit_bytes=VMEM_LIMIT),
        name="rwkv_proj",
    )(x, x, x, mod, gpre.reshape(1, D), p['mu'], p['w_rkv'].astype(BF16), w1c, w2bd,
      p['w0'].reshape(1, 2 * D), a1c, a2bd, p['a0'].reshape(1, 2 * D),
      p['g1'].astype(BF16), p['g2'].astype(BF16), p['k_k'].reshape(1, D), p['k_a'].reshape(1, D),
      _head_ones())


CH = 32
SUB = 8
GH = 2
GW = GH * HEAD
NG = NHEAD // GH
SR = GH * CH

_NT = (((1,), (1,)), ((), ()))
_TN = (((0,), (0,)), ((), ()))


def _scan_masks():
    rho = np.arange(SR)
    same = (rho[:, None] // CH) == (rho[None, :] // CH)
    lower = rho[None, :] < rho[:, None]
    strict = np.stack([same & lower, same & lower.T])
    incl = strict | np.eye(SR, dtype=bool)[None]
    t = np.arange(CH)
    tri = np.stack([t[None, :] <= t[:, None], t[None, :] >= t[:, None]])
    return (jnp.asarray(strict.astype(np.float32)), jnp.asarray(incl.astype(np.float32)),
            jnp.asarray(tri.astype(np.float32), dtype=BF16))


def _wkv_kernel(rf_ref, vf_ref, kkf_ref, lwf_ref, kdf_ref, bf_ref, rb_ref, vb_ref, kkb_ref, lwb_ref, kdb_ref,
                bb_ref, strict_ref, incl_ref, tri_ref, s0f_ref, s0b_ref, yf_ref, yb_ref, sfinf_ref, sfinb_ref,
                s_ref,
                *, n_ctx_chunks, ctx_chunks_per_seq, smp_chunks_per_seq):
    q = pl.program_id(0)
    is_ctx = q < n_ctx_chunks
    c = jnp.where(is_ctx, q % ctx_chunks_per_seq, (q - n_ctx_chunks) % smp_chunks_per_seq)

    @pl.when(c == 0)
    def _():
        zero = jnp.zeros((HEAD, HEAD), F32)
        for d, s0_ref in enumerate((s0f_ref, s0b_ref)):
            for g in range(NG):
                bd = jnp.concatenate(
                    [jnp.concatenate([s0_ref[0, g * GH + h] if hh == h else zero for hh in range(GH)], axis=1)
                     for h in range(GH)], axis=0)
                s_ref[d, g] = jnp.where(is_ctx, 0.0, bd)

    def decayed(d, a, r_ref, v_ref, kk_ref, lw_ref, kd_ref, b_ref):
        pos = a if d == 0 else SUB - 1 - a
        rows = slice(pos * CH, (pos + 1) * CH)
        lw = lw_ref[rows, :]
        cum = _split_sum_left(tri_ref[d], lw)
        tot = cum[CH - 1:CH, :] if d == 0 else cum[0:1, :]
        e_neg = jnp.exp(-cum)
        e_rem = jnp.exp(tot - cum)
        kd = kd_ref[rows, :]
        bb = b_ref[rows, :]
        return (kk_ref[rows, :] * jnp.exp(cum - lw), r_ref[rows, :] * jnp.exp(cum), kd * e_neg, bb * e_neg,
                v_ref[rows, :], kd * e_rem, bb * e_rem), jnp.exp(tot), rows

    refs_d = ((rf_ref, vf_ref, kkf_ref, lwf_ref, kdf_ref, bf_ref),
              (rb_ref, vb_ref, kkb_ref, lwb_ref, kdb_ref, bb_ref))
    tiles = {(a, d): decayed(d, a, *refs_d[d]) for a in range(SUB) for d in range(2)}
    y_refs = (yf_ref, yb_ref)
    strict_d = [strict_ref[d] > 0.5 for d in range(2)]
    incl_d = [incl_ref[d] > 0.5 for d in range(2)]

    lane_head = lax.broadcasted_iota(jnp.int32, (CH, GW), 1) // HEAD
    head_lanes = [lane_head == h for h in range(GH)]

    def stack(x):
        return jnp.concatenate([jnp.where(m, x, 0.0) for m in head_lanes], axis=0).astype(BF16)

    eye = (lax.broadcasted_iota(jnp.int32, (SR, SR), 0)
           == lax.broadcasted_iota(jnp.int32, (SR, SR), 1)).astype(F32)
    dot = functools.partial(jnp.dot, preferred_element_type=F32)
    dotg = functools.partial(lax.dot_general, preferred_element_type=F32)

    dg = [(d, g) for g in range(NG) for d in range(2)]
    sl = {g: slice(g * GW, (g + 1) * GW) for g in range(NG)}
    pre = {}
    state = {c_: s_ref[c_[0], c_[1]] for c_ in dg}

    def independent(chunks):
        adg = [(a, d, g) for a in chunks for d, g in dg]
        ops = {(a, d, g): [stack(x[:, sl[g]]) for x in tiles[(a, d)][0]] for a, d, g in adg}
        qr = {c_: jnp.concatenate([o[0], o[1]], axis=0) for c_, o in ops.items()}
        gm = {c_: dotg(qr[c_], jnp.concatenate([o[3], o[2]], axis=0), _NT) for c_, o in ops.items()}
        yield
        l = {c_: jnp.where(strict_d[c_[1]], gm[c_][:SR, :SR], 0.0) for c_ in adg}
        m = {c_: jnp.where(strict_d[c_[1]], gm[c_][:SR, SR:], 0.0).astype(BF16) for c_ in adg}
        ppb = {c_: jnp.concatenate([jnp.where(incl_d[c_[1]], gm[c_][SR:, SR:], 0.0),
                                    jnp.where(incl_d[c_[1]], gm[c_][SR:, :SR], 0.0)], axis=1).astype(BF16)
               for c_ in adg}
        mv = {c_: dot(m[c_], ops[c_][4]) for c_ in adg}
        yield
        n = {c_: eye - l[c_] for c_ in adg}
        pw = {c_: l[c_].astype(BF16) for c_ in adg}
        for _ in range(CH.bit_length() - 2):
            pw = {c_: dot(pw[c_], pw[c_]).astype(BF16) for c_ in adg}
            yield
            n = {c_: n[c_] + dot(n[c_].astype(BF16), pw[c_]) for c_ in adg}
            yield
        for a in chunks:
            of = lambda t_: {c_: t_[(a,) + c_] for c_ in dg}
            pre[a] = dict(qr=of(qr), mv=of(mv), ppb=of(ppb), tinv={c_: n[(a,) + c_].astype(BF16) for c_ in dg},
                          vs={c_: ops[(a,) + c_][4] for c_ in dg},
                          khbh={c_: jnp.concatenate([ops[(a,) + c_][5], ops[(a,) + c_][6]], axis=0) for c_ in dg})

    def dependent(a):
        p_ = pre[a]
        s0 = dict(state)
        zz = {c_: dotg(p_['qr'][c_], s0[c_].astype(BF16), _NT) for c_ in dg}
        yield
        u = {c_: dot(p_['tinv'][c_], (zz[c_][:SR] + p_['mv'][c_]).astype(BF16)) for c_ in dg}
        yield
        vu = {c_: jnp.concatenate([p_['vs'][c_], (-u[c_]).astype(BF16)], axis=0) for c_ in dg}
        ys = {c_: zz[c_][SR:] + dot(p_['ppb'][c_], vu[c_]) for c_ in dg}
        yield
        for d, g in dg:
            state[(d, g)] = s0[(d, g)] * tiles[(a, d)][1][:, sl[g]] + dotg(vu[(d, g)], p_['khbh'][(d, g)], _TN)
        for d, g in dg:
            y = ys[(d, g)][0:CH]
            for h in range(1, GH):
                y = y + ys[(d, g)][h * CH:(h + 1) * CH]
            y_refs[d][tiles[(a, d)][2], sl[g]] = y.astype(BF16)

    def run(*gens):
        gens = list(gens)
        while gens:
            for gen in list(gens):
                if next(gen, gens) is gens:
                    gens.remove(gen)

    run(independent(tuple(range(SUB))))
    for a in range(SUB):
        run(dependent(a))
    for (d, g), s in state.items():
        s_ref[d, g] = s

    @pl.when(jnp.logical_and(is_ctx, c == ctx_chunks_per_seq - 1))
    def _():
        for (d, g), s in state.items():
            for h in range(GH):
                (sfinf_ref, sfinb_ref)[d][0, 0, g * GH + h] = s[h * HEAD:(h + 1) * HEAD, h * HEAD:(h + 1) * HEAD]


def _wkv(r, v, kk, lw, kd, b, state_f, state_b, n_ctx_seq=BATCH, t_ctx=SEQ, n_smp_seq=DEC_BATCH, t_smp=DEC_SEQ):
    ntok = n_ctx_seq * t_ctx + n_smp_seq * t_smp
    blk = SUB * CH
    cps_c, cps_s = t_ctx // blk, t_smp // blk
    ncc = n_ctx_seq * cps_c
    nq = ncc + n_smp_seq * cps_s

    def seq_of(q):
        return jnp.where(q < ncc, q // cps_c, n_ctx_seq + (q - ncc) // cps_s)

    def row_block(d, q):
        ctx = q < ncc
        cps = jnp.where(ctx, cps_c, cps_s)
        c = jnp.where(ctx, q % cps_c, (q - ncc) % cps_s)
        return q - c + (c if d == 0 else cps - 1 - c)

    tok = [pl.BlockSpec((blk, D), functools.partial(lambda q, d: (row_block(d, q), 0), d=d)) for d in range(2)]
    tokd = [pl.BlockSpec((blk, D), functools.partial(lambda q, d: (row_block(d, q), d), d=d)) for d in range(2)]
    msk = pl.BlockSpec((2, SR, SR), lambda q: (0, 0, 0))
    tri = pl.BlockSpec((2, CH, CH), lambda q: (0, 0, 0))
    s0spec = pl.BlockSpec((1, NHEAD, HEAD, HEAD), lambda q: (jnp.maximum(seq_of(q) - n_ctx_seq, 0), 0, 0, 0))
    sspec = pl.BlockSpec((1, 1, NHEAD, HEAD, HEAD),
                         lambda q: (jnp.minimum(seq_of(q), n_ctx_seq - 1), 0, 0, 0, 0))
    sfin = jax.ShapeDtypeStruct((n_ctx_seq, 1, NHEAD, HEAD, HEAD), F32)

    strict, incl, trim = _scan_masks()
    return pl.pallas_call(
        functools.partial(_wkv_kernel, n_ctx_chunks=ncc, ctx_chunks_per_seq=cps_c, smp_chunks_per_seq=cps_s),
        out_shape=[jax.ShapeDtypeStruct((ntok, D), BF16), jax.ShapeDtypeStruct((ntok, D), BF16), sfin, sfin],
        grid=(nq,),
        in_specs=[tok[0], tok[0], tok[0], tokd[0], tokd[0], tokd[0],
                  tok[1], tok[1], tok[1], tokd[1], tokd[1], tokd[1], msk, msk, tri, s0spec, s0spec],
        out_specs=[tok[0], tok[1], sspec, sspec],
        scratch_shapes=[pltpu.VMEM((2, NG, GW, GW), F32)],
        compiler_params=pltpu.CompilerParams(
            dimension_semantics=("arbitrary",), vmem_limit_bytes=VMEM_LIMIT),
        name="wkv_scan",
    )(r, v, kk, lw, kd, b, r, v, kk, lw, kd, b, strict, incl, trim, state_f, state_b)


RO_TM = 512


def _rwkv_out_kernel(x_ref, yf_ref, yb_ref, r_ref, v_ref, kd_ref, g_ref, mod_ref, gnw_ref, gnb_ref,
                     rk_ref, wo_ref, gpost_ref, ones_ref, o_ref):
    ones = ones_ref[...]
    ys = yf_ref[...] + yb_ref[...]
    mean = _head_sum(ys, ones) * (1.0 / HEAD)
    c = ys - mean
    var = _head_sum(c * c, ones) * (1.0 / HEAD)
    yn = c * lax.rsqrt(var + GN_EPS) * gnw_ref[...] + gnb_ref[...]
    kd = kd_ref[...]
    bonus = _head_sum(r_ref[...] * (kd[:, :D] + kd[:, D:]) * rk_ref[...], ones) * v_ref[...]
    out = _dot((yn + bonus) * g_ref[...], wo_ref[...])
    o_ref[...] = x_ref[...] + mod_ref[0, 5:6, :] * _rms(out, gpost_ref[...])


def _rwkv_out(x, y_f, y_b, r, v, kd, g, mod, p, gpost):
    tm = RO_TM
    row = lambda i: (i, 0)
    full2 = lambda i: (0, 0)
    vec = pl.BlockSpec((1, D), full2)
    tok = pl.BlockSpec((tm, D), row)
    tok2 = pl.BlockSpec((tm, 2 * D), row)
    return pl.pallas_call(
        _rwkv_out_kernel,
        out_shape=jax.ShapeDtypeStruct((NTOK, D), F32),
        grid=(NTOK // tm,),
        in_specs=[tok, tok, tok, tok, tok, tok2, tok,
                  pl.BlockSpec((1, N_MOD, D), lambda i: (_cond_row(i * tm), 0, 0)),
                  vec, vec, vec, pl.BlockSpec((D, D), full2), vec, pl.BlockSpec((HSUM_W, HSUM_W), full2)],
        out_specs=tok,
        compiler_params=pltpu.CompilerParams(
            dimension_semantics=("arbitrary",), vmem_limit_bytes=VMEM_LIMIT),
        name="rwkv_out",
    )(x, y_f, y_b, r, v, kd, g, mod, p['gn_w'].reshape(1, D), p['gn_b'].reshape(1, D),
      p['r_k'].reshape(1, D), p['w_o'].astype(BF16), gpost.reshape(1, D), _head_ones())


def _box_matrix(length, win):
    t = np.arange(length)
    lo = np.clip(t - win // 2, 0, length)
    hi = np.clip(t + win - win // 2, 0, length)
    a = ((t[None, :] >= lo[:, None]) & (t[None, :] < hi[:, None])).astype(np.float32)
    return a, (hi - lo).astype(np.float32)


def _pool_constants(grid):
    mats, inv = [], []
    for win in POOL_WINDOWS:
        if grid:
            ar, cr = _box_matrix(DEC_SEQ // GRID_W, win)
            ac, cc = _box_matrix(GRID_W, win)
            mats.append(np.kron(ar, ac))
            inv.append(1.0 / np.kron(cr, cc))
        else:
            a, c = _box_matrix(SEQ, win)
            mats.append(a)
            inv.append(1.0 / c)
    mats = jnp.asarray(np.stack(mats).astype(BF16))
    inv = jnp.asarray(np.stack(inv)[:, :, None], dtype=F32)
    return mats, inv


def _pool_kernel(rs_ref, xg_ref, mod_ref, gpre_ref, a_ref, inv_ref, pw_ref, ps_ref, o_ref, *, t):
    for j in range(xg_ref.shape[0] // t):
        rows = slice(j * t, (j + 1) * t)
        part = (xg_ref[rows, :] * rs_ref[rows, 0:1]) * gpre_ref[...] * (1.0 + mod_ref[0, 4:5, :]) + mod_ref[0, 3:4, :]
        m = _dot(a_ref[0], part) * inv_ref[0]
        o_ref[rows, :] = _dot(m - part, pw_ref[0]) * ps_ref[...]


def _pool(x, rs, mod, gpre, pool_w, pool_scale, t, seq0, nseq, grid, per_step=1):
    mats, inv = _pool_constants(grid)
    gc = POOL_GC
    tb = t * per_step
    seq0 //= per_step
    return pl.pallas_call(
        functools.partial(_pool_kernel, t=t),
        out_shape=jax.ShapeDtypeStruct((nseq * t, D), F32),
        grid=(4, nseq // per_step),
        in_specs=[
            pl.BlockSpec((tb, RS_W), lambda g, s: (seq0 + s, 0)),
            pl.BlockSpec((tb, gc), lambda g, s: (seq0 + s, g)),
            pl.BlockSpec((1, N_MOD, gc), lambda g, s: (_cond_row((seq0 + s) * tb), 0, g)),
            pl.BlockSpec((1, gc), lambda g, s: (0, g)),
            pl.BlockSpec((1, t, t), lambda g, s: (g, 0, 0)),
            pl.BlockSpec((1, t, 1), lambda g, s: (g, 0, 0)),
            pl.BlockSpec((1, gc, gc), lambda g, s: (g, 0, 0)),
            pl.BlockSpec((1, gc), lambda g, s: (0, g)),
        ],
        out_specs=pl.BlockSpec((tb, gc), lambda g, s: (s, g)),
        compiler_params=pltpu.CompilerParams(
            dimension_semantics=("arbitrary", "arbitrary"), vmem_limit_bytes=VMEM_LIMIT),
        name="pool_grid" if grid else "pool_seq",
    )(rs, x, mod, gpre.reshape(1, D), mats, inv, pool_w.astype(BF16), pool_scale.reshape(1, D))


def kernel(x_prompt, x_sample, c, state_ctx_fwd, state_ctx_bwd, c_ctx, w_mod, b_mod, norm_pre, norm_post,
           ffn_w1, ffn_w3, ffn_w2, rwkv_mu, rwkv_w_rkv, rwkv_w0, rwkv_w1, rwkv_w2, rwkv_a0, rwkv_a1,
           rwkv_a2, rwkv_g1, rwkv_g2, rwkv_k_k, rwkv_k_a, rwkv_r_k, rwkv_gn_w, rwkv_gn_b, rwkv_w_o,
           pool_w, pool_scale):
    x = (x_prompt.reshape(N_CTX, D), x_sample.reshape(N_SMP, D))
    cond = jnp.concatenate([c_ctx[None], c, jnp.zeros((COND_ROWS - 1 - DEC_BATCH, D), F32)])
    mod = _modulation(cond, w_mod, b_mod).reshape(DEPTH, COND_ROWS, N_MOD, D)
    wffn = (ffn_w1.astype(BF16), ffn_w3.astype(BF16), ffn_w2.astype(BF16))

    x = _ffn(x, mod[0], norm_pre[0, 0], norm_post[0, 0], wffn, (0, 0), 0)
    p = dict(mu=rwkv_mu[0], w_rkv=rwkv_w_rkv[0], w0=rwkv_w0[0], w1=rwkv_w1[0], w2=rwkv_w2[0],
             a0=rwkv_a0[0], a1=rwkv_a1[0], a2=rwkv_a2[0], g1=rwkv_g1[0], g2=rwkv_g2[0],
             k_k=rwkv_k_k[0], k_a=rwkv_k_a[0], r_k=rwkv_r_k[0], gn_w=rwkv_gn_w[0], gn_b=rwkv_gn_b[0],
             w_o=rwkv_w_o[0])
    r, v, kk, g, lw, kd, b = _rwkv_proj(x, mod[0], norm_pre[0, 1], p)
    y_f, y_b, new_f, new_b = _wkv(r, v, kk, lw, kd, b, state_ctx_fwd[:, 0], state_ctx_bwd[:, 0])
    x = _rwkv_out(x, y_f, y_b, r, v, kd, g, mod[0], p, norm_post[0, 1])
    x = _ffn(x, mod[0], norm_pre[0, 2], norm_post[0, 2], wffn, (0, 1), 6)

    x, rs = _ffn(x, mod[1], norm_pre[1, 0], norm_post[1, 0], wffn, (1, 0), 0, emit_rs=True)
    pc = _pool(x, rs, mod[1], norm_pre[1, 1], pool_w[0], pool_scale[0], SEQ, 0, BATCH, False, per_step=4)
    ps = _pool(x, rs, mod[1], norm_pre[1, 1], pool_w[0], pool_scale[0], DEC_SEQ, N_CTX // DEC_SEQ, DEC_BATCH, True)
    y_ctx, y_smp = _ffn(x, mod[1], norm_pre[1, 2], norm_post[1, 2], wffn, (1, 1), 6,
                        pending=((pc, ps), norm_post[1, 1], 5), split_out=True)

    return (y_ctx.reshape(BATCH, SEQ, D), y_smp.reshape(DEC_BATCH, DEC_SEQ, D), new_f, new_b)
```
